```python
import math, functools
import jax, jax.numpy as jnp
from jax import lax
import numpy as np

D_MODEL = 1024
BATCH = 8
SEQ = 4096
DEPTH = 1
DEC_BATCH = 32
DEC_SEQ = 8
PAST_LEN = 16384
PAGE_SIZE = 128

NSA_HEADS = 8
NSA_KV_HEADS = 2
HEAD_DIM = 64
NSA_Q_W = NSA_HEADS * HEAD_DIM
NSA_KV_W = NSA_KV_HEADS * HEAD_DIM
CMP_BLOCK = 64
SEL_BLOCK = 64
N_SEL = 16
WINDOW = 512
CMP_HIDDEN = 128
Q_BLOCK = 64
RWKV_HEADS = 8
RWKV_HEAD_DIM = 64
RWKV_W = RWKV_HEADS * RWKV_HEAD_DIM
DECAY_LORA = 64
ICLR_LORA = 64
GATE_LORA = 128
NSA_COLS = NSA_Q_W + 6 * NSA_KV_W + 3 * NSA_HEADS
RWKV_COLS = 3 * RWKV_W + DECAY_LORA + ICLR_LORA + GATE_LORA
IN_COLS = NSA_COLS + RWKV_COLS
MIX_W = NSA_Q_W + RWKV_W
N_MEM = 256
X_HEADS = 4
X_HEAD_DIM = 128
D_FF = 2816
RMS_EPS = 1e-6
LNX_EPS = 64e-5
FORCED_SCORE = 1e9

kernel_name = 'hymba_nsa_rwkv7_macaron_step'


def rms_norm(x, g):
    xf = x.astype(jnp.float32)
    y = xf * lax.rsqrt(jnp.mean(xf * xf, axis=-1, keepdims=True) + RMS_EPS)
    return (y * g.astype(jnp.float32)).astype(x.dtype)


def masked_softmax(s, mask):
    s = jnp.where(mask, s, -jnp.inf)
    m = jnp.max(s, axis=-1, keepdims=True)
    p = jnp.exp(s - jnp.where(jnp.isfinite(m), m, 0.0))
    return p / jnp.maximum(jnp.sum(p, axis=-1, keepdims=True), 1e-30)


def alibi_slopes():
    h = jnp.arange(NSA_HEADS, dtype=jnp.float32)
    return jnp.exp2(-8.0 * (h + 1.0) / NSA_HEADS).reshape(NSA_KV_HEADS, NSA_HEADS // NSA_KV_HEADS)


def swiglu_half(x, g, w_gu, w_down):
    gate, up = jnp.split(rms_norm(x, g) @ w_gu, 2, axis=-1)
    return 0.5 * ((jax.nn.silu(gate) * up) @ w_down)


def nsa_project(p, q_g, ks_g, kw_g, gate_b):
    n, t, _ = p.shape
    G, R = NSA_KV_HEADS, NSA_HEADS // NSA_KV_HEADS
    offs = [NSA_Q_W + i * NSA_KV_W for i in range(7)]
    q, kc, vc, ks, vs, kw, vw, gl = jnp.split(p, offs, axis=-1)
    kv = lambda a: a.reshape(n, t, G, HEAD_DIM)
    q = rms_norm(q.reshape(n, t, G, R, HEAD_DIM), q_g)
    gates = jax.nn.sigmoid((gl + gate_b).astype(jnp.float32)).astype(p.dtype).reshape(n, t, G, R, 3)
    return (q, kv(kc), kv(vc), rms_norm(kv(ks), ks_g), kv(vs), rms_norm(kv(kw), kw_g), kv(vw), gates)


def nsa_compress(rows, pe, w1, w2):
    n, l, g, d = rows.shape
    blk = rows.reshape(n, l // CMP_BLOCK, CMP_BLOCK, g, d) + pe[:, None, :]
    h = jax.nn.gelu(jnp.einsum('nbjgd,jdh->nbgh', blk, w1))
    return h @ w2


def nsa_cmp_branch(q, t, kc, vc):
    nb = kc.shape[1]
    end = (jnp.arange(nb, dtype=jnp.int32) + 1) * CMP_BLOCK - 1
    dist = (t[:, None] - end[None, :]).astype(jnp.float32)
    s = jnp.einsum('ntgrd,nbgd->ngrtb', q, kc).astype(jnp.float32) * HEAD_DIM ** -0.5
    s = s - alibi_slopes()[:, :, None, None] * dist
    p = masked_softmax(s, dist >= 0)
    o = jnp.einsum('ngrtb,nbgd->ntgrd', p.astype(vc.dtype), vc)
    return o, jnp.sum(p, axis=2)


def nsa_select(imp, t):
    nb = imp.shape[-1]
    b = jnp.arange(nb, dtype=jnp.int32)
    cur = (t // SEL_BLOCK)[:, None]
    forced = (b == 0) | (b == cur) | (b == cur - 1)
    score = jnp.where(forced, FORCED_SCORE, jnp.where(b <= cur, imp, -FORCED_SCORE))
    _, idx = lax.top_k(score, min(N_SEL, nb))
    return idx


def nsa_sel_branch(q, t, idx, k_sel, v_sel):
    pos = idx[..., None] * SEL_BLOCK + jnp.arange(SEL_BLOCK, dtype=jnp.int32)
    dist = (t[:, None, None] - pos).astype(jnp.float32)
    s = jnp.einsum('ntgrd,ngtksd->ngrtks', q, k_sel).astype(jnp.float32) * HEAD_DIM ** -0.5
    s = s - alibi_slopes()[:, :, None, None, None] * dist[:, :, None]
    n, g, r, tq, k, sb = s.shape
    p = masked_softmax(s.reshape(n, g, r, tq, k * sb), (dist >= 0).reshape(n, g, 1, tq, k * sb))
    return jnp.einsum('ngrtks,ngtksd->ntgrd', p.reshape(s.shape).astype(v_sel.dtype), v_sel)


def nsa_win_branch(q, t, kw, vw, pos):
    dist = (t[:, None] - pos[None, :]).astype(jnp.float32)
    mask = (dist >= 0) & (dist < WINDOW) & (pos[None, :] >= 0)
    s = jnp.einsum('ntgrd,nlgd->ngrtl', q, kw).astype(jnp.float32) * HEAD_DIM ** -0.5
    s = s - alibi_slopes()[:, :, None, None] * dist
    p = masked_softmax(s, mask)
    return jnp.einsum('ngrtl,nlgd->ntgrd', p.astype(vw.dtype), vw)


def nsa_gate(g, o_c, o_s, o_w):
    return g[..., 0:1] * o_c + g[..., 1:2] * o_s + g[..., 2:3] * o_w


def nsa_prompt(parts, cmp_w):
    q, kc_rows, vc_rows, ks, vs, kw, vw, gates = parts
    pe_k, pe_v, wk1, wk2, wv1, wv2, kc_g = cmp_w
    n, s = q.shape[:2]
    G = NSA_KV_HEADS
    kc = rms_norm(nsa_compress(kc_rows, pe_k, wk1, wk2), kc_g)
    vc = nsa_compress(vc_rows, pe_v, wv1, wv2)
    nb = s // SEL_BLOCK
    to_blocks = lambda a: a.reshape(n, nb, SEL_BLOCK, G, HEAD_DIM).transpose(0, 3, 1, 2, 4)
    ks_blk, vs_blk = to_blocks(ks), to_blocks(vs)
    pad = ((0, 0), (WINDOW, 0), (0, 0), (0, 0))
    kw_pad, vw_pad = jnp.pad(kw, pad), jnp.pad(vw, pad)
    nq = s // Q_BLOCK
    qblk = lambda a: jnp.moveaxis(a.reshape(n, nq, Q_BLOCK, *a.shape[2:]), 1, 0)
    b_ix = jnp.arange(n)[:, None, None, None]
    g_ix = jnp.arange(G)[None, :, None, None]

    def one_block(args):
        qb, gb, i = args
        start = i * Q_BLOCK
        t = start + jnp.arange(Q_BLOCK, dtype=jnp.int32)
        o_c, imp = nsa_cmp_branch(qb, t, kc, vc)
        idx = nsa_select(imp, t)
        o_s = nsa_sel_branch(qb, t, idx, ks_blk[b_ix, g_ix, idx], vs_blk[b_ix, g_ix, idx])
        pos = start - WINDOW + jnp.arange(WINDOW + Q_BLOCK, dtype=jnp.int32)
        o_w = nsa_win_branch(qb, t,
                             lax.dynamic_slice_in_dim(kw_pad, start, WINDOW + Q_BLOCK, axis=1),
                             lax.dynamic_slice_in_dim(vw_pad, start, WINDOW + Q_BLOCK, axis=1), pos)
        return nsa_gate(gb, o_c, o_s, o_w)

    o = lax.map(one_block, (qblk(q), qblk(gates), jnp.arange(nq, dtype=jnp.int32)))
    o = jnp.moveaxis(o, 0, 1).reshape(n, s, NSA_Q_W)
    w = min(WINDOW, s)
    return (o, jnp.stack([kc_rows, vc_rows, ks, vs], axis=2),
            jnp.stack([kw[:, s - w:], vw[:, s - w:]], axis=2))


def nsa_sample(parts, cmp_w, cache_kv, layer, page_table, cache_win):
    q, kc_new, vc_new, ks_new, vs_new, kw_new, vw_new, gates = parts
    pe_k, pe_v, wk1, wk2, wv1, wv2, kc_g = cmp_w
    n, tn = q.shape[:2]
    G = NSA_KV_HEADS
    past = page_table.shape[1] * PAGE_SIZE
    t = past + jnp.arange(tn, dtype=jnp.int32)
    n_new = -(-tn // SEL_BLOCK)
    pad = ((0, 0), (0, n_new * SEL_BLOCK - tn), (0, 0), (0, 0))
    page_rows = jnp.arange(PAGE_SIZE, dtype=jnp.int32)[None, None, :]
    past_rows = lambda c: cache_kv[layer, page_table[:, :, None], page_rows, c].reshape(n, past, G, HEAD_DIM)
    kc = rms_norm(jnp.concatenate([nsa_compress(past_rows(0), pe_k, wk1, wk2),
                                   nsa_compress(jnp.pad(kc_new, pad), pe_k, wk1, wk2)], axis=1), kc_g)
    vc = jnp.concatenate([nsa_compress(past_rows(1), pe_v, wv1, wv2),
                          nsa_compress(jnp.pad(vc_new, pad), pe_v, wv1, wv2)], axis=1)
    o_c, imp = nsa_cmp_branch(q, t, kc, vc)
    idx = nsa_select(imp, t)
    nb_past = past // SEL_BLOCK
    blk_per_page = PAGE_SIZE // SEL_BLOCK
    b_ix = jnp.arange(n)[:, None, None, None]
    g_ix = jnp.arange(G)[None, :, None, None]
    blk_p = jnp.minimum(idx, nb_past - 1)
    phys = page_table[b_ix, blk_p // blk_per_page]
    prow = (blk_p % blk_per_page)[..., None] * SEL_BLOCK + jnp.arange(SEL_BLOCK, dtype=jnp.int32)
    blk_n = jnp.clip(idx - nb_past, 0, n_new - 1)
    is_new = (idx >= nb_past)[..., None, None]

    def gather_sel(c, rows_new):
        from_pool = cache_kv[layer, phys[..., None], prow, c, g_ix[..., None]]
        new_blk = jnp.pad(rows_new, pad).reshape(n, n_new, SEL_BLOCK, G, HEAD_DIM).transpose(0, 3, 1, 2, 4)
        return jnp.where(is_new, new_blk[b_ix, g_ix, blk_n], from_pool)

    o_s = nsa_sel_branch(q, t, idx, gather_sel(2, ks_new), gather_sel(3, vs_new))
    w_buf = cache_win.shape[2]
    win_all = jnp.concatenate([cache_win[layer], jnp.stack([kw_new, vw_new], axis=2)], axis=1)
    pos = past - w_buf + jnp.arange(w_buf + tn, dtype=jnp.int32)
    o_w = nsa_win_branch(q, t, win_all[:, :, 0], win_all[:, :, 1], pos)
    o = nsa_gate(gates, o_c, o_s, o_w).reshape(n, tn, NSA_Q_W)
    return (o, jnp.stack([kc_new, vc_new, ks_new, vs_new], axis=2), win_all[:, tn:])


def rwkv_mix(p, shift_prev, s0, mu, w0, w_decay2, a0, w_iclr2, w_gate2, k_k, k_a, r_k, lnx_g, lnx_b):
    f32 = jnp.float32
    n, tn, _ = p.shape
    prev = jnp.concatenate([shift_prev[:, None, :], p[:, :-1]], axis=1)
    xs = p + (prev - p) * mu
    offs = [RWKV_W, 2 * RWKV_W, 3 * RWKV_W, 3 * RWKV_W + DECAY_LORA, 3 * RWKV_W + DECAY_LORA + ICLR_LORA]
    r, k, v, xw, xa, xg = jnp.split(xs, offs, axis=-1)
    wlog = -jax.nn.softplus(-(w0 + jnp.tanh(xw) @ w_decay2).astype(f32)) - 0.5
    decay = jnp.exp(-jnp.exp(wlog))
    a = jax.nn.sigmoid((a0 + xa @ w_iclr2).astype(f32))
    g = jax.nn.sigmoid(xg) @ w_gate2
    heads = lambda z: z.reshape(n, tn, RWKV_HEADS, RWKV_HEAD_DIM).astype(f32)
    r_h, v_h, a_h, d_h = heads(r), heads(v), heads(a), heads(decay)
    kk = heads(k * k_k)
    kk = kk * lax.rsqrt(jnp.maximum(jnp.sum(kk * kk, axis=-1, keepdims=True), 1e-24))
    k_h = heads(k) * (1.0 + (a_h - 1.0) * k_a.reshape(RWKV_HEADS, RWKV_HEAD_DIM).astype(f32))

    def step(S, inp):
        rt, dt, kt, vt, kkt, at = inp
        sa = jnp.einsum('nhij,nhj->nhi', S, -kkt)
        S = S * dt[:, :, None, :] + sa[..., None] * (kkt * at)[:, :, None, :] + vt[..., None] * kt[:, :, None, :]
        return S, jnp.einsum('nhij,nhj->nhi', S, rt)

    tm = lambda z: jnp.moveaxis(z, 1, 0)
    S, out = lax.scan(step, s0.astype(f32), (tm(r_h), tm(d_h), tm(k_h), tm(v_h), tm(kk), tm(a_h)))
    out = jnp.moveaxis(out, 0, 1)
    mean = jnp.mean(out, axis=-1, keepdims=True)
    var = jnp.mean(jnp.square(out - mean), axis=-1, keepdims=True)
    out = ((out - mean) * lax.rsqrt(var + LNX_EPS)).reshape(n, tn, RWKV_W) * lnx_g + lnx_b
    bonus = jnp.sum(r_h * k_h * r_k.astype(f32), axis=-1, keepdims=True) * v_h
    o = (out + bonus.reshape(n, tn, RWKV_W)) * g
    return o.astype(p.dtype), S.astype(s0.dtype), p[:, -1]


def mem_kv(mem, norm_mem_g, w_xkv, xk_g):
    n, m, _ = mem.shape
    kv = (rms_norm(mem, norm_mem_g) @ w_xkv).reshape(n, m, 2, X_HEADS, X_HEAD_DIM)
    return jnp.stack([rms_norm(kv[:, :, 0], xk_g), kv[:, :, 1]], axis=2)


def cross_attn(x, mkv, norm_x_g, w_xq, xq_g, w_xo):
    n, tn, _ = x.shape
    q = rms_norm((rms_norm(x, norm_x_g) @ w_xq).reshape(n, tn, X_HEADS, X_HEAD_DIM), xq_g)
    s = jnp.einsum('nthd,nmhd->nhtm', q, mkv[:, :, 0]).astype(jnp.float32) * X_HEAD_DIM ** -0.5
    p = jax.nn.softmax(s, axis=-1)
    o = jnp.einsum('nhtm,nmhd->nthd', p.astype(x.dtype), mkv[:, :, 1]).reshape(n, tn, X_HEADS * X_HEAD_DIM)
    return o @ w_xo


def layer_forward(x, mkv, shift0, s0, nsa_attend, lw):
    (norm_ffn1_g, w_ffn1_gu, w_ffn1_down, norm_mix_g, w_in, w_out, q_norm_g, kc_norm_g, ks_norm_g,
     kw_norm_g, gate_b, cmp_pe_k, cmp_pe_v, w_cmp_k1, w_cmp_k2, w_cmp_v1, w_cmp_v2, rwkv_mu, rwkv_w0,
     w_decay2, rwkv_a0, w_iclr2, w_gate2, rwkv_k_k, rwkv_k_a, rwkv_r_k, lnx_g, lnx_b, norm_x_g, w_xq,
     xq_norm_g, w_xo, norm_ffn2_g, w_ffn2_gu, w_ffn2_down, norm_out_g) = lw
    x = x + swiglu_half(x, norm_ffn1_g, w_ffn1_gu, w_ffn1_down)
    p = rms_norm(x, norm_mix_g) @ w_in
    parts = nsa_project(p[..., :NSA_COLS], q_norm_g, ks_norm_g, kw_norm_g, gate_b)
    o_nsa, kv_rows, win_state = nsa_attend(parts, (cmp_pe_k, cmp_pe_v, w_cmp_k1, w_cmp_k2, w_cmp_v1, w_cmp_v2, kc_norm_g))
    o_rwkv, s_new, shift_new = rwkv_mix(p[..., NSA_COLS:], shift0, s0, rwkv_mu, rwkv_w0, w_decay2, rwkv_a0,
                                        w_iclr2, w_gate2, rwkv_k_k, rwkv_k_a, rwkv_r_k, lnx_g, lnx_b)
    x = x + jnp.concatenate([o_nsa, o_rwkv], axis=-1) @ w_out
    x = x + cross_attn(x, mkv, norm_x_g, w_xq, xq_norm_g, w_xo)
    x = x + swiglu_half(x, norm_ffn2_g, w_ffn2_gu, w_ffn2_down)
    return rms_norm(x, norm_out_g), kv_rows, win_state, s_new, shift_new


def setup_inputs(seed: int = 0) -> dict:
    key = jax.random.key(seed)
    ks = iter(jax.random.split(key, 64))
    f32 = jnp.float32
    nrm = lambda shape, scale: jax.random.normal(next(ks), shape, f32) * scale
    gain = lambda d: 1.0 + nrm((DEPTH, d), 0.02)
    uni = lambda shape, lo, hi: jax.random.uniform(next(ks), shape, f32, lo, hi)
    G = NSA_KV_HEADS
    n_pages = PAST_LEN // PAGE_SIZE
    n_used = DEC_BATCH * n_pages
    n_phys = n_used + n_used // 4
    w_buf = min(WINDOW, PAST_LEN)
    page_table = jax.random.permutation(next(ks), n_phys)[:n_used].reshape(DEC_BATCH, n_pages).astype(jnp.int32)
    return {
        'x_prompt': nrm((BATCH, SEQ, D_MODEL), 1.0),
        'x_sample': nrm((DEC_BATCH, DEC_SEQ, D_MODEL), 1.0),
        'cache_nsa_kv': nrm((DEPTH, n_phys, PAGE_SIZE, 4, G, HEAD_DIM), 1.0),
        'cache_nsa_win': nrm((DEPTH, DEC_BATCH, w_buf, 2, G, HEAD_DIM), 1.0),
        'state_rwkv_s': nrm((DEPTH, DEC_BATCH, RWKV_HEADS, RWKV_HEAD_DIM, RWKV_HEAD_DIM), 0.5),
        'state_rwkv_shift': nrm((DEPTH, DEC_BATCH, RWKV_COLS), 1.0),
        'cache_mem_kv': nrm((DEPTH, DEC_BATCH, N_MEM, 2, X_HEADS, X_HEAD_DIM), 1.0),
        'page_table': page_table,
        'mem_prompt': nrm((BATCH, N_MEM, D_MODEL), 1.0),
        'norm_ffn1_g': gain(D_MODEL),
        'w_ffn1_gu': nrm((DEPTH, D_MODEL, 2 * D_FF), D_MODEL ** -0.5),
        'w_ffn1_down': nrm((DEPTH, D_FF, D_MODEL), D_FF ** -0.5),
        'norm_mix_g': gain(D_MODEL),
        'w_in': nrm((DEPTH, D_MODEL, IN_COLS), D_MODEL ** -0.5),
        'w_out': nrm((DEPTH, MIX_W, D_MODEL), MIX_W ** -0.5),
        'q_norm_g': gain(HEAD_DIM),
        'kc_norm_g': gain(HEAD_DIM),
        'ks_norm_g': gain(HEAD_DIM),
        'kw_norm_g': gain(HEAD_DIM),
        'gate_b': nrm((DEPTH, 3 * NSA_HEADS), 0.1),
        'cmp_pe_k': nrm((DEPTH, CMP_BLOCK, HEAD_DIM), 0.1),
        'cmp_pe_v': nrm((DEPTH, CMP_BLOCK, HEAD_DIM), 0.1),
        'w_cmp_k1': nrm((DEPTH, CMP_BLOCK, HEAD_DIM, CMP_HIDDEN), (CMP_BLOCK * HEAD_DIM) ** -0.5),
        'w_cmp_k2': nrm((DEPTH, CMP_HIDDEN, HEAD_DIM), CMP_HIDDEN ** -0.5),
        'w_cmp_v1': nrm((DEPTH, CMP_BLOCK, HEAD_DIM, CMP_HIDDEN), (CMP_BLOCK * HEAD_DIM) ** -0.5),
        'w_cmp_v2': nrm((DEPTH, CMP_HIDDEN, HEAD_DIM), CMP_HIDDEN ** -0.5),
        'rwkv_mu': uni((DEPTH, RWKV_COLS), 0.0, 1.0),
        'rwkv_w0': uni((DEPTH, RWKV_W), -6.0, -1.0),
        'w_decay2': nrm((DEPTH, DECAY_LORA, RWKV_W), 0.5 * DECAY_LORA ** -0.5),
        'rwkv_a0': nrm((DEPTH, RWKV_W), 0.1),
        'w_iclr2': nrm((DEPTH, ICLR_LORA, RWKV_W), 0.5 * ICLR_LORA ** -0.5),
        'w_gate2': nrm((DEPTH, GATE_LORA, RWKV_W), GATE_LORA ** -0.5),
        'rwkv_k_k': 0.85 + nrm((DEPTH, RWKV_W), 0.02),
        'rwkv_k_a': 1.0 + nrm((DEPTH, RWKV_W), 0.02),
        'rwkv_r_k': nrm((DEPTH, RWKV_HEADS, RWKV_HEAD_DIM), 0.1),
        'lnx_g': gain(RWKV_W),
        'lnx_b': nrm((DEPTH, RWKV_W), 0.02),
        'norm_x_g': gain(D_MODEL),
        'norm_mem_g': gain(D_MODEL),
        'w_xq': nrm((DEPTH, D_MODEL, X_HEADS * X_HEAD_DIM), D_MODEL ** -0.5),
        'w_xkv': nrm((DEPTH, D_MODEL, 2 * X_HEADS * X_HEAD_DIM), D_MODEL ** -0.5),
        'xq_norm_g': gain(X_HEAD_DIM),
        'xk_norm_g': gain(X_HEAD_DIM),
        'w_xo': nrm((DEPTH, X_HEADS * X_HEAD_DIM, D_MODEL), (X_HEADS * X_HEAD_DIM) ** -0.5),
        'norm_ffn2_g': gain(D_MODEL),
        'w_ffn2_gu': nrm((DEPTH, D_MODEL, 2 * D_FF), D_MODEL ** -0.5),
        'w_ffn2_down': nrm((DEPTH, D_FF, D_MODEL), D_FF ** -0.5),
        'norm_out_g': gain(D_MODEL),
    }


def reference(x_prompt, x_sample, cache_nsa_kv, cache_nsa_win, state_rwkv_s, state_rwkv_shift, cache_mem_kv,
              page_table, mem_prompt, norm_ffn1_g, w_ffn1_gu, w_ffn1_down, norm_mix_g, w_in, w_out, q_norm_g,
              kc_norm_g, ks_norm_g, kw_norm_g, gate_b, cmp_pe_k, cmp_pe_v, w_cmp_k1, w_cmp_k2, w_cmp_v1,
              w_cmp_v2, rwkv_mu, rwkv_w0, w_decay2, rwkv_a0, w_iclr2, w_gate2, rwkv_k_k, rwkv_k_a, rwkv_r_k,
              lnx_g, lnx_b, norm_x_g, norm_mem_g, w_xq, w_xkv, xq_norm_g, xk_norm_g, w_xo, norm_ffn2_g,
              w_ffn2_gu, w_ffn2_down, norm_out_g):
    n_p = x_prompt.shape[0]
    y_p, y_s = x_prompt, x_sample
    kv_p, kv_s, win_p, win_s, rs_p, rs_s, sh_p, sh_s, mkv_p = [], [], [], [], [], [], [], [], []
    for l in range(DEPTH):
        lw = (norm_ffn1_g[l], w_ffn1_gu[l], w_ffn1_down[l], norm_mix_g[l], w_in[l], w_out[l], q_norm_g[l],
              kc_norm_g[l], ks_norm_g[l], kw_norm_g[l], gate_b[l], cmp_pe_k[l], cmp_pe_v[l], w_cmp_k1[l],
              w_cmp_k2[l], w_cmp_v1[l], w_cmp_v2[l], rwkv_mu[l], rwkv_w0[l], w_decay2[l], rwkv_a0[l],
              w_iclr2[l], w_gate2[l], rwkv_k_k[l], rwkv_k_a[l], rwkv_r_k[l], lnx_g[l], lnx_b[l], norm_x_g[l],
              w_xq[l], xq_norm_g[l], w_xo[l], norm_ffn2_g[l], w_ffn2_gu[l], w_ffn2_down[l], norm_out_g[l])
        mkv = mem_kv(mem_prompt, norm_mem_g[l], w_xkv[l], xk_norm_g[l])
        y_p, a_kv, a_win, a_s, a_sh = layer_forward(
            y_p, mkv, jnp.zeros((n_p, RWKV_COLS), x_prompt.dtype),
            jnp.zeros((n_p, RWKV_HEADS, RWKV_HEAD_DIM, RWKV_HEAD_DIM), x_prompt.dtype), nsa_prompt, lw)
        kv_p.append(a_kv); win_p.append(a_win); rs_p.append(a_s); sh_p.append(a_sh); mkv_p.append(mkv)
        attend = functools.partial(nsa_sample, cache_kv=cache_nsa_kv, layer=l, page_table=page_table,
                                   cache_win=cache_nsa_win)
        y_s, b_kv, b_win, b_s, b_sh = layer_forward(y_s, cache_mem_kv[l], state_rwkv_shift[l], state_rwkv_s[l],
                                                     attend, lw)
        kv_s.append(b_kv); win_s.append(b_win); rs_s.append(b_s); sh_s.append(b_sh)
    return (y_p, y_s, jnp.stack(kv_p), jnp.stack(kv_s), jnp.stack(win_p), jnp.stack(win_s),
            jnp.stack(rs_p), jnp.stack(rs_s), jnp.stack(sh_p), jnp.stack(sh_s), jnp.stack(mkv_p))
```

```python
import functools
import math

import jax
import jax.numpy as jnp
from jax import lax
from jax.experimental import pallas as pl
from jax.experimental.pallas import tpu as pltpu

F32 = jnp.float32
BF16 = jnp.bfloat16

D_MODEL = 1024
NSA_HEADS = 8
NSA_KV_HEADS = 2
NSA_REP = NSA_HEADS // NSA_KV_HEADS
HEAD_DIM = 64
CMP_BLOCK = 64
N_SEL = 16
WINDOW = 512
CMP_HIDDEN = 128
PAGE_SIZE = 128
RWKV_HEADS = 8
RWKV_HEAD_DIM = 64
RWKV_W = RWKV_HEADS * RWKV_HEAD_DIM
DECAY_LORA = 64
ICLR_LORA = 64
GATE_LORA = 128
NSA_Q_W = NSA_HEADS * HEAD_DIM
NSA_KV_W = NSA_KV_HEADS * HEAD_DIM
NSA_COLS = NSA_Q_W + 6 * NSA_KV_W + 3 * NSA_HEADS
RWKV_COLS = 3 * RWKV_W + DECAY_LORA + ICLR_LORA + GATE_LORA
X_HEADS = 4
X_HEAD_DIM = 128
D_FF = 2816
RMS_EPS = 1e-6
LNX_EPS = 64e-5
FORCED_SCORE = 1e9

LANES = 128
NSA_PAD = 1408
P_COLS = NSA_PAD + RWKV_COLS
TQ = 128
KC = 128
VMEM_LIMIT = 56 * 1024 * 1024

ALIBI = tuple(2.0 ** (-8.0 * (h + 1.0) / NSA_HEADS) for h in range(NSA_HEADS))


def _cparams(sem):
    return pltpu.CompilerParams(dimension_semantics=sem, vmem_limit_bytes=VMEM_LIMIT)


def _rms(x, g):
    return x * lax.rsqrt(jnp.mean(x * x, axis=-1, keepdims=True) + RMS_EPS) * g


def _seg_ones(width, seg):
    r = lax.broadcasted_iota(jnp.int32, (width, width), 0) // seg
    c = lax.broadcasted_iota(jnp.int32, (width, width), 1) // seg
    return (r == c).astype(F32)


def _seg_sum(x, seg):
    ones = _seg_ones(LANES, seg)
    parts = [jnp.dot(x[:, c:c + LANES], ones, precision=lax.Precision.HIGHEST, preferred_element_type=F32)
             for c in range(0, x.shape[1], LANES)]
    return parts[0] if len(parts) == 1 else jnp.concatenate(parts, axis=1)


def _seg_rms(x, g, seg):
    if seg == LANES:
        parts = [_rms(x[:, c:c + LANES], g[:, c:c + LANES]) for c in range(0, x.shape[1], LANES)]
        return parts[0] if len(parts) == 1 else jnp.concatenate(parts, axis=1)
    return x * lax.rsqrt(_seg_sum(x * x, seg) * (1.0 / seg) + RMS_EPS) * g


def _ffn_kernel(x_ref, g_ref, wg_ref, wu_ref, wd_ref, go_ref, o_ref, xn_ref, acc_ref, *, final_norm):
    f = pl.program_id(1)

    @pl.when(f == 0)
    def _():
        xn_ref[...] = _rms(x_ref[...], g_ref[...]).astype(BF16)
        acc_ref[...] = jnp.zeros_like(acc_ref)

    xn = xn_ref[...]
    gate = jnp.dot(xn, wg_ref[...], preferred_element_type=F32)
    up = jnp.dot(xn, wu_ref[...], preferred_element_type=F32)
    h = (gate * jax.nn.sigmoid(gate) * up).astype(BF16)
    acc_ref[...] += jnp.dot(h, wd_ref[...], preferred_element_type=F32)

    @pl.when(f == pl.num_programs(1) - 1)
    def _():
        y = x_ref[...] + 0.5 * acc_ref[...]
        if final_norm:
            y = _rms(y, go_ref[...])
        o_ref[...] = y


def _ffn(x, g, w_gu, w_down, g_out, *, final_norm, tm, tf):
    m = x.shape[0]
    nf = D_FF // tf
    return pl.pallas_call(
        functools.partial(_ffn_kernel, final_norm=final_norm),
        grid=(m // tm, nf),
        in_specs=[
            pl.BlockSpec((tm, D_MODEL), lambda i, f: (i, 0)),
            pl.BlockSpec((1, D_MODEL), lambda i, f: (0, 0)),
            pl.BlockSpec((D_MODEL, tf), lambda i, f: (0, f)),
            pl.BlockSpec((D_MODEL, tf), lambda i, f: (0, nf + f)),
            pl.BlockSpec((tf, D_MODEL), lambda i, f: (f, 0)),
            pl.BlockSpec((1, D_MODEL), lambda i, f: (0, 0)),
        ],
        out_specs=pl.BlockSpec((tm, D_MODEL), lambda i, f: (i, 0)),
        out_shape=jax.ShapeDtypeStruct((m, D_MODEL), F32),
        scratch_shapes=[pltpu.VMEM((tm, D_MODEL), BF16), pltpu.VMEM((tm, D_MODEL), F32)],
        compiler_params=_cparams(("parallel", "arbitrary")),
    )(x, g, w_gu, w_gu, w_down, g_out)


def _proj_kernel(x_ref, g_ref, w_ref, hg_ref, gb_ref, on_ref, or_ref):
    xn = _rms(x_ref[...], g_ref[...]).astype(BF16)
    p = jnp.dot(xn, w_ref[...], preferred_element_type=F32)
    or_ref[...] = p[:, NSA_PAD:]
    on_ref[...] = p[:, 0:NSA_PAD]
    hg = hg_ref[...]
    on_ref[:, 0:NSA_Q_W] = _seg_rms(p[:, 0:NSA_Q_W], hg[:, 0:NSA_Q_W], HEAD_DIM)
    for c in (NSA_Q_W + 2 * NSA_KV_W, NSA_Q_W + 4 * NSA_KV_W):
        on_ref[:, c:c + NSA_KV_W] = _seg_rms(p[:, c:c + NSA_KV_W], hg[:, c:c + NSA_KV_W], HEAD_DIM)
    c = NSA_Q_W + 6 * NSA_KV_W
    on_ref[:, c:c + LANES] = jax.nn.sigmoid(p[:, c:c + LANES] + gb_ref[...])


def _proj(x, g, w_pad, head_g, gate_b, *, tm):
    m = x.shape[0]
    return pl.pallas_call(
        _proj_kernel,
        grid=(m // tm,),
        in_specs=[
            pl.BlockSpec((tm, D_MODEL), lambda i: (i, 0)),
            pl.BlockSpec((1, D_MODEL), lambda i: (0, 0)),
            pl.BlockSpec((D_MODEL, P_COLS), lambda i: (0, 0)),
            pl.BlockSpec((1, NSA_PAD), lambda i: (0, 0)),
            pl.BlockSpec((1, LANES), lambda i: (0, 0)),
        ],
        out_specs=[pl.BlockSpec((tm, NSA_PAD), lambda i: (i, 0)), pl.BlockSpec((tm, RWKV_COLS), lambda i: (i, 0))],
        out_shape=[jax.ShapeDtypeStruct((m, NSA_PAD), F32), jax.ShapeDtypeStruct((m, RWKV_COLS), F32)],
        compiler_params=_cparams(("parallel",)),
    )(x, g, w_pad, head_g, gate_b)


def _compress_core(xk_ref, xv_ref, nblk, pe_ref, w1k_ref, w1v_ref, w2k_ref, w2v_ref, kcg_ref):
    def body(j, carry):
        acc_k, acc_v = carry
        pe = pe_ref[j]
        xk = xk_ref[pl.ds(j, nblk, stride=CMP_BLOCK), :] + pe[:, 0:LANES]
        xv = xv_ref[pl.ds(j, nblk, stride=CMP_BLOCK), :] + pe[:, LANES:2 * LANES]
        acc_k = acc_k + jnp.dot(xk.astype(BF16), w1k_ref[j], preferred_element_type=F32)
        acc_v = acc_v + jnp.dot(xv.astype(BF16), w1v_ref[j], preferred_element_type=F32)
        return acc_k, acc_v

    zero = jnp.zeros((nblk, 2 * CMP_HIDDEN), F32)
    acc_k, acc_v = lax.fori_loop(0, CMP_BLOCK, body, (zero, zero))
    hk = jax.nn.gelu(acc_k).astype(BF16)
    hv = jax.nn.gelu(acc_v).astype(BF16)
    kc = jnp.dot(hk, w2k_ref[...], preferred_element_type=F32)
    vc = jnp.dot(hv, w2v_ref[...], preferred_element_type=F32)
    return _seg_rms(kc, kcg_ref[...], HEAD_DIM), vc


def _compress_kernel(xk_ref, xv_ref, pe_ref, w1k_ref, w1v_ref, w2k_ref, w2v_ref, kcg_ref, kc_ref, vc_ref, *, nblk):
    kc, vc = _compress_core(xk_ref, xv_ref, nblk, pe_ref, w1k_ref, w1v_ref, w2k_ref, w2v_ref, kcg_ref)
    kc_ref[...] = kc
    vc_ref[...] = vc


def _cmp_weight_specs(imap):
    return [
        pl.BlockSpec((CMP_BLOCK, 1, 2 * LANES), imap(3)),
        pl.BlockSpec((CMP_BLOCK, LANES, 2 * CMP_HIDDEN), imap(3)),
        pl.BlockSpec((CMP_BLOCK, LANES, 2 * CMP_HIDDEN), imap(3)),
        pl.BlockSpec((2 * CMP_HIDDEN, LANES), imap(2)),
        pl.BlockSpec((2 * CMP_HIDDEN, LANES), imap(2)),
        pl.BlockSpec((1, LANES), imap(2)),
    ]


def _compress_rows(rows, col_block, cw, *, nblk):
    m = rows.shape[0]
    steps = m // (nblk * CMP_BLOCK)
    imap = lambda nd: (lambda i: (0,) * nd)
    return pl.pallas_call(
        functools.partial(_compress_kernel, nblk=nblk),
        grid=(steps,),
        in_specs=[pl.BlockSpec((nblk * CMP_BLOCK, LANES), lambda i: (i, col_block)),
                  pl.BlockSpec((nblk * CMP_BLOCK, LANES), lambda i: (i, col_block + 1))] + _cmp_weight_specs(imap),
        out_specs=[pl.BlockSpec((nblk, LANES), lambda i: (i, 0))] * 2,
        out_shape=[jax.ShapeDtypeStruct((steps * nblk, LANES), F32)] * 2,
        compiler_params=_cparams(("parallel",)),
    )(rows, rows, *cw)


def _compress_paged_kernel(pt_ref, cache_ref, pe_ref, w1k_ref, w1v_ref, w2k_ref, w2v_ref, kcg_ref,
                           kc_ref, vc_ref, bufk_ref, bufv_ref, sem, *, n_pages):
    n = pl.program_id(0)

    def page_copies(i):
        rows = pl.ds(i * PAGE_SIZE, PAGE_SIZE)
        page = pt_ref[n, i]
        return (pltpu.make_async_copy(cache_ref.at[page, :, pl.ds(0, LANES)], bufk_ref.at[rows, :], sem.at[0]),
                pltpu.make_async_copy(cache_ref.at[page, :, pl.ds(LANES, LANES)], bufv_ref.at[rows, :], sem.at[1]))

    def start(i, c):
        for cp in page_copies(i):
            cp.start()
        return c

    def wait(i, c):
        for cp in page_copies(i):
            cp.wait()
        return c

    lax.fori_loop(0, n_pages, start, 0)
    lax.fori_loop(0, n_pages, wait, 0)
    nblk = n_pages * (PAGE_SIZE // CMP_BLOCK)
    kc, vc = _compress_core(bufk_ref, bufv_ref, nblk, pe_ref, w1k_ref, w1v_ref, w2k_ref, w2v_ref, kcg_ref)
    kc_ref[0] = kc
    vc_ref[0] = vc


def _compress_paged(page_table, cache, cw):
    nb, n_pages = page_table.shape
    nblk = n_pages * (PAGE_SIZE // CMP_BLOCK)
    imap = lambda nd: (lambda n, pt: (0,) * nd)
    return pl.pallas_call(
        functools.partial(_compress_paged_kernel, n_pages=n_pages),
        grid_spec=pltpu.PrefetchScalarGridSpec(
            num_scalar_prefetch=1,
            grid=(nb,),
            in_specs=[pl.BlockSpec(memory_space=pl.ANY)] + _cmp_weight_specs(imap),
            out_specs=[pl.BlockSpec((1, nblk, LANES), lambda n, pt: (n, 0, 0))] * 2,
            scratch_shapes=[pltpu.VMEM((n_pages * PAGE_SIZE, LANES), F32), pltpu.VMEM((n_pages * PAGE_SIZE, LANES), F32),
                            pltpu.SemaphoreType.DMA((2,))],
        ),
        out_shape=[jax.ShapeDtypeStruct((nb, nblk, LANES), F32)] * 2,
        compiler_params=_cparams(("arbitrary",)),
    )(page_table, cache, *cw)


def _softmax_init():
    return (jnp.full((1, TQ), -jnp.inf, F32), jnp.zeros((1, TQ), F32), jnp.zeros((HEAD_DIM, TQ), F32))


def _softmax_step(state, s, valid, vt):
    m, l, acc = state
    s = jnp.where(valid, s, -jnp.inf)
    m_new = jnp.maximum(m, jnp.max(s, axis=0, keepdims=True))
    m_safe = jnp.where(m_new == -jnp.inf, 0.0, m_new)
    alpha = jnp.exp(m - m_safe)
    p = jnp.exp(s - m_safe)
    l = alpha * l + jnp.sum(p, axis=0, keepdims=True)
    acc = alpha * acc + jnp.dot(vt, p.astype(BF16), preferred_element_type=F32)
    return m_new, l, acc


def _softmax_out(state):
    _, l, acc = state
    return acc / jnp.maximum(l, 1e-30)


def _scores(kp, qp):
    return lax.dot_general(kp, qp, (((1,), (1,)), ((), ())), preferred_element_type=F32)


def _cmp_and_select(qp_heads, kcp, vct, t_row, n_blk, nb_pad, score_ref, sel_ref):
    b_col = lax.broadcasted_iota(jnp.int32, (nb_pad, TQ), 0)
    end = (b_col + 1) * CMP_BLOCK - 1
    valid = (t_row >= end) & (b_col < n_blk)
    outs = []
    imp = jnp.zeros((nb_pad, TQ), F32)
    for qp in qp_heads:
        s = jnp.where(valid, _scores(kcp, qp), -jnp.inf)
        m = jnp.max(s, axis=0, keepdims=True)
        e = jnp.exp(s - jnp.where(m == -jnp.inf, 0.0, m))
        p = e / jnp.maximum(jnp.sum(e, axis=0, keepdims=True), 1e-30)
        imp = imp + p
        outs.append(jnp.dot(vct, p.astype(BF16), preferred_element_type=F32))
    cur = t_row // CMP_BLOCK
    forced = (b_col == 0) | (b_col == cur) | (b_col == cur - 1)
    score = jnp.where(forced, FORCED_SCORE, jnp.where(b_col <= cur, imp, -FORCED_SCORE))
    score = jnp.where(b_col < n_blk, score, -jnp.inf)
    score_ref[...] = score

    def rank_body(bp, cnt):
        row = jnp.broadcast_to(score_ref[pl.ds(bp, 1), :], (nb_pad, TQ))
        ahead = (row > score) | ((row == score) & (b_col > bp))
        return cnt + jnp.where(ahead, 1.0, 0.0)

    cnt = lax.fori_loop(0, n_blk, rank_body, jnp.zeros((nb_pad, TQ), F32))
    sel_ref[...] = ((cnt < N_SEL) & (b_col < n_blk)).astype(F32)
    return outs


def _gate_sum(gt_ref, g, r, o_c, o_s, o_w):
    row = (g * NSA_REP + r) * 3
    return (gt_ref[0, pl.ds(row, 1), :] * o_c + gt_ref[0, pl.ds(row + 1, 1), :] * o_s
            + gt_ref[0, pl.ds(row + 2, 1), :] * o_w)


def _nsa_prompt_kernel(qp_ref, kcp_ref, vct_ref, ksp_ref, vst_ref, kwp_ref, vwt_ref, gt_ref, o_ref,
                       score_ref, sel_ref, *, n_blk):
    i = pl.program_id(1)
    t0 = i * TQ
    t_row = t0 + lax.broadcasted_iota(jnp.int32, (1, TQ), 1)
    k_col = lax.broadcasted_iota(jnp.int32, (KC, TQ), 0)
    blocks_per_chunk = KC // CMP_BLOCK
    for g in range(NSA_KV_HEADS):
        heads = [g * NSA_REP + r for r in range(NSA_REP)]
        qps = [qp_ref[0, :, h * LANES:(h + 1) * LANES] for h in heads]
        o_c = _cmp_and_select(qps, kcp_ref[0, g], vct_ref[0, g], t_row, n_blk, n_blk, score_ref, sel_ref)

        def sel_body(c, states):
            kp = ksp_ref[0, g, c]
            vt = vst_ref[0, g, c]
            pos = c * KC + k_col
            selm = jnp.concatenate(
                [jnp.broadcast_to(sel_ref[pl.ds(c * blocks_per_chunk + b, 1), :], (CMP_BLOCK, TQ))
                 for b in range(blocks_per_chunk)], axis=0)
            valid = (selm > 0.5) & (pos <= t_row)
            return tuple(_softmax_step(st, _scores(kp, qp), valid, vt) for st, qp in zip(states, qps))

        n_sel_chunks = (t0 + TQ + KC - 1) // KC
        st_s = lax.fori_loop(0, n_sel_chunks, sel_body, tuple(_softmax_init() for _ in heads))

        def win_body(c, states):
            kp = kwp_ref[0, g, c]
            vt = vwt_ref[0, g, c]
            dist = t_row - (c * KC + k_col)
            valid = (dist >= 0) & (dist < WINDOW)
            return tuple(_softmax_step(st, _scores(kp, qp), valid, vt) for st, qp in zip(states, qps))

        c_lo = jnp.maximum(i - WINDOW // KC, 0)
        st_w = lax.fori_loop(c_lo, i + 1, win_body, tuple(_softmax_init() for _ in heads))

        for r, h in enumerate(heads):
            o = _gate_sum(gt_ref, g, r, o_c[r], _softmax_out(st_s[r]), _softmax_out(st_w[r]))
            o_ref[0, h * HEAD_DIM:(h + 1) * HEAD_DIM, :] = o


def _nsa_prompt(qp, kcp, vct, ksp, vst, kwp, vwt, gates_t):
    n, t, _ = qp.shape
    n_blk = t // CMP_BLOCK
    n_chunks = t // KC
    G = NSA_KV_HEADS
    full = lambda shape: pl.BlockSpec((1,) + shape, lambda b, i: (b,) + (0,) * len(shape))
    return pl.pallas_call(
        functools.partial(_nsa_prompt_kernel, n_blk=n_blk),
        grid=(n, t // TQ),
        in_specs=[
            pl.BlockSpec((1, TQ, NSA_HEADS * LANES), lambda b, i: (b, i, 0)),
            full((G, n_blk, LANES)),
            full((G, HEAD_DIM, n_blk)),
            full((G, n_chunks, KC, LANES)),
            full((G, n_chunks, HEAD_DIM, KC)),
            full((G, n_chunks, KC, LANES)),
            full((G, n_chunks, HEAD_DIM, KC)),
            pl.BlockSpec((1, 3 * NSA_HEADS, TQ), lambda b, i: (b, 0, i)),
        ],
        out_specs=pl.BlockSpec((1, NSA_Q_W, TQ), lambda b, i: (b, 0, i)),
        out_shape=jax.ShapeDtypeStruct((n, NSA_Q_W, t), F32),
        scratch_shapes=[pltpu.VMEM((n_blk, TQ), F32), pltpu.VMEM((n_blk, TQ), F32)],
        compiler_params=_cparams(("parallel", "arbitrary")),
    )(qp, kcp, vct, ksp, vst, kwp, vwt, gates_t)


def _nsa_sample_kernel(pt_ref, qp_ref, kcp_ref, vct_ref, kwp_ref, vwt_ref, knew_ref, vnewt_ref, gt_ref, cache_ref,
                       o_ref, score_ref, sel_ref, buf_ref, slot_ref, sem, *, past, n_new, n_blk, nb_pad, n_pages):
    n = pl.program_id(0)
    G = NSA_KV_HEADS
    lane = lax.broadcasted_iota(jnp.int32, (1, TQ), 1)
    t_row = past + lane
    k_col = lax.broadcasted_iota(jnp.int32, (KC, TQ), 0)
    blocks_per_page = PAGE_SIZE // CMP_BLOCK
    nb_past = past // CMP_BLOCK

    qps = [[qp_ref[0, :, (g * NSA_REP + r) * LANES:(g * NSA_REP + r + 1) * LANES] for r in range(NSA_REP)]
           for g in range(G)]
    o_c = []
    for g in range(G):
        o_c.append(_cmp_and_select(qps[g], kcp_ref[0, g], vct_ref[0, g], t_row, n_blk, nb_pad, score_ref,
                                   sel_ref.at[g]))

    def page_copy(lp, slot):
        return pltpu.make_async_copy(cache_ref.at[pt_ref[n, lp], :, pl.ds(2 * LANES, 2 * LANES)],
                                     buf_ref.at[pl.ds(slot * PAGE_SIZE, PAGE_SIZE), :], sem)

    real = (lane < n_new).astype(F32)

    def gather_body(lp, cnt):
        need = jnp.float32(0.0)
        for g in range(G):
            for b in range(blocks_per_page):
                need = jnp.maximum(need, jnp.max(sel_ref[g, pl.ds(lp * blocks_per_page + b, 1), :] * real))
        wanted = need > 0.5

        @pl.when(wanted)
        def _():
            page_copy(lp, cnt).start()
            slot_ref[cnt] = lp

        return cnt + wanted.astype(jnp.int32)

    n_slots = lax.fori_loop(0, n_pages, gather_body, jnp.int32(0))

    def wait_body(s, c):
        page_copy(slot_ref[s], s).wait()
        return c

    lax.fori_loop(0, n_slots, wait_body, 0)

    eye = (lax.broadcasted_iota(jnp.int32, (HEAD_DIM, HEAD_DIM), 0)
           == lax.broadcasted_iota(jnp.int32, (HEAD_DIM, HEAD_DIM), 1)).astype(BF16)

    for g in range(G):
        heads = [g * NSA_REP + r for r in range(NSA_REP)]
        q64 = [qp[:, 0:HEAD_DIM] for qp in qps[g]]

        def pool_step(states, k, vt, pos, selm):
            dist = (t_row - pos).astype(F32)
            valid = (selm > 0.5) & (pos <= t_row)
            return tuple(_softmax_step(st, _scores(k, q) - ALIBI[h] * dist, valid, vt)
                         for st, q, h in zip(states, q64, heads))

        def sel_body(s, states):
            lp = slot_ref[s]
            base = pl.multiple_of(s * PAGE_SIZE, PAGE_SIZE)
            k = buf_ref[pl.ds(base, PAGE_SIZE), g * HEAD_DIM:(g + 1) * HEAD_DIM].astype(BF16)
            v = buf_ref[pl.ds(base, PAGE_SIZE), LANES + g * HEAD_DIM:LANES + (g + 1) * HEAD_DIM].astype(BF16)
            vt = _scores(eye, v).astype(BF16)
            selm = jnp.concatenate(
                [jnp.broadcast_to(sel_ref[g, pl.ds(lp * blocks_per_page + b, 1), :], (CMP_BLOCK, TQ))
                 for b in range(blocks_per_page)], axis=0)
            return pool_step(states, k, vt, lp * PAGE_SIZE + k_col, selm)

        st_s = lax.fori_loop(0, n_slots, sel_body, tuple(_softmax_init() for _ in heads))
        selm_new = jnp.broadcast_to(sel_ref[g, pl.ds(nb_past, 1), :], (KC, TQ))
        st_s = pool_step(st_s, knew_ref[0, g], vnewt_ref[0, g], past + k_col, selm_new)

        w_buf = min(WINDOW, past)

        def win_body(c, states):
            idx = c * KC + k_col
            dist = t_row - (past - w_buf + idx)
            valid = (dist >= 0) & (dist < WINDOW) & (idx < w_buf + n_new)
            kp = kwp_ref[0, g, c]
            vt = vwt_ref[0, g, c]
            return tuple(_softmax_step(st, _scores(kp, qp), valid, vt) for st, qp in zip(states, qps[g]))

        st_w = lax.fori_loop(0, kwp_ref.shape[2], win_body, tuple(_softmax_init() for _ in heads))

        for r, h in enumerate(heads):
            o = _gate_sum(gt_ref, g, r, o_c[g][r], _softmax_out(st_s[r]), _softmax_out(st_w[r]))
            o_ref[0, h * HEAD_DIM:(h + 1) * HEAD_DIM, :] = o


def _nsa_sample(page_table, qp, kcp, vct, kwp, vwt, knew, vnewt, gates_t, cache, *, past, n_new):
    n = qp.shape[0]
    G = NSA_KV_HEADS
    n_pages = page_table.shape[1]
    n_blk = past // CMP_BLOCK + 1
    nb_pad = kcp.shape[2]
    n_wc = kwp.shape[2]
    full = lambda shape: pl.BlockSpec((1,) + shape, lambda b, pt: (b,) + (0,) * len(shape))
    kern = functools.partial(_nsa_sample_kernel, past=past, n_new=n_new, n_blk=n_blk, nb_pad=nb_pad, n_pages=n_pages)
    return pl.pallas_call(
        kern,
        grid_spec=pltpu.PrefetchScalarGridSpec(
            num_scalar_prefetch=1,
            grid=(n,),
            in_specs=[
                full((TQ, NSA_HEADS * LANES)),
                full((G, nb_pad, LANES)),
                full((G, HEAD_DIM, nb_pad)),
                full((G, n_wc, KC, LANES)),
                full((G, n_wc, HEAD_DIM, KC)),
                full((G, KC, HEAD_DIM)),
                full((G, HEAD_DIM, KC)),
                full((3 * NSA_HEADS, TQ)),
                pl.BlockSpec(memory_space=pl.ANY),
            ],
            out_specs=full((NSA_Q_W, TQ)),
            scratch_shapes=[
                pltpu.VMEM((nb_pad, TQ), F32),
                pltpu.VMEM((G, nb_pad, TQ), F32),
                pltpu.VMEM((n_pages * PAGE_SIZE, 2 * LANES), F32),
                pltpu.SMEM((n_pages,), jnp.int32),
                pltpu.SemaphoreType.DMA(()),
            ],
        ),
        out_shape=jax.ShapeDtypeStruct((n, NSA_Q_W, TQ), F32),
        compiler_params=_cparams(("arbitrary",)),
    )(page_table, qp, kcp, vct, kwp, vwt, knew, vnewt, gates_t, cache)


def _alibi_features(pos):
    blk = (pos // CMP_BLOCK).astype(F32)
    off = (pos % CMP_BLOCK).astype(F32)
    pad = jnp.zeros(pos.shape + (HEAD_DIM - 2,), F32)
    return jnp.concatenate([blk[..., None], off[..., None], pad], axis=-1)


def _query_features(q):
    n, t, _ = q.shape
    qh = q.reshape(n, t, NSA_HEADS, HEAD_DIM) * (HEAD_DIM ** -0.5)
    slope = jnp.asarray(ALIBI, F32)
    feat = jnp.zeros((NSA_HEADS, HEAD_DIM), F32).at[:, 0].set(slope * CMP_BLOCK).at[:, 1].set(slope)
    feat = jnp.broadcast_to(feat, (n, t, NSA_HEADS, HEAD_DIM))
    return jnp.concatenate([qh, feat], axis=-1).reshape(n, t, NSA_HEADS * LANES).astype(BF16)


def _key_features(k, pos):
    n, l, _ = k.shape
    kh = k.reshape(n, l, NSA_KV_HEADS, HEAD_DIM).transpose(0, 2, 1, 3)
    feat = jnp.broadcast_to(_alibi_features(pos), (n, NSA_KV_HEADS, l, HEAD_DIM))
    return jnp.concatenate([kh, feat], axis=-1).astype(BF16)


def _values_t(v, chunk):
    n, l, _ = v.shape
    vh = v.reshape(n, l // chunk, chunk, NSA_KV_HEADS, HEAD_DIM)
    return vh.transpose(0, 3, 1, 4, 2).astype(BF16)


def _rwkv_pre_kernel(p_ref, prev_ref, mu_ref, w0_ref, wd_ref, a0_ref, wa_ref, wg_ref, kk_ref, ka_ref, rk_ref,
                     r_o, d_o, k_o, v_o, kk_o, b_o, g_o, bonus_o):
    p = p_ref[...]
    xs = p + (prev_ref[...] - p) * mu_ref[...]
    W = RWKV_W
    r, k, v = xs[:, 0:W], xs[:, W:2 * W], xs[:, 2 * W:3 * W]
    lora = xs[:, 3 * W:3 * W + LANES]
    xg = xs[:, 3 * W + LANES:3 * W + 2 * LANES]
    z = w0_ref[...] + jnp.dot(jnp.tanh(lora).astype(BF16), wd_ref[...], preferred_element_type=F32)
    nz = -z
    softplus = jnp.maximum(nz, 0.0) + jnp.log(1.0 + jnp.exp(-jnp.abs(nz)))
    decay = jnp.exp(-jnp.exp(-softplus - 0.5))
    a = jax.nn.sigmoid(a0_ref[...] + jnp.dot(lora.astype(BF16), wa_ref[...], preferred_element_type=F32))
    g = jnp.dot(jax.nn.sigmoid(xg).astype(BF16), wg_ref[...], preferred_element_type=F32)
    kk = k * kk_ref[...]
    kk = kk * lax.rsqrt(jnp.maximum(_seg_sum(kk * kk, RWKV_HEAD_DIM), 1e-24))
    k_h = k * (1.0 + (a - 1.0) * ka_ref[...])
    r_o[...] = r
    d_o[...] = decay
    k_o[...] = k_h
    v_o[...] = v
    kk_o[...] = kk
    b_o[...] = kk * a
    g_o[...] = g
    bonus_o[...] = _seg_sum(r * k_h * rk_ref[...], RWKV_HEAD_DIM) * v


def _rwkv_pre(p, prev, rw, *, tm):
    m = p.shape[0]
    row = lambda w: pl.BlockSpec((1, w), lambda i: (0, 0))
    mat = lambda a, b: pl.BlockSpec((a, b), lambda i: (0, 0))
    out = pl.BlockSpec((tm, RWKV_W), lambda i: (i, 0))
    return pl.pallas_call(
        _rwkv_pre_kernel,
        grid=(m // tm,),
        in_specs=[
            pl.BlockSpec((tm, RWKV_COLS), lambda i: (i, 0)),
            pl.BlockSpec((tm, RWKV_COLS), lambda i: (i, 0)),
            row(RWKV_COLS), row(RWKV_W), mat(LANES, RWKV_W), row(RWKV_W), mat(LANES, RWKV_W),
            mat(GATE_LORA, RWKV_W), row(RWKV_W), row(RWKV_W), row(RWKV_W),
        ],
        out_specs=[out] * 8,
        out_shape=[jax.ShapeDtypeStruct((m, RWKV_W), F32)] * 8,
        compiler_params=_cparams(("parallel",)),
    )(p, prev, *rw)


RW_I = RWKV_HEAD_DIM // 2


def _rwkv_scan_kernel(r_ref, d_ref, k_ref, kk_ref, b_ref, v_ref, s0_ref, o_ref, sout_ref, s_ref, *, tc):
    c = pl.program_id(1)

    @pl.when(c == 0)
    def _():
        s_ref[...] = s0_ref[0]

    def step(t, carry):
        u = jnp.zeros((RW_I, LANES), F32)
        for j in range(RWKV_HEAD_DIM):
            u = u + s_ref[j] * kk_ref[0, t, pl.ds(j, 1), :]
        u = -u
        vt = v_ref[0, t]
        o = jnp.zeros((RW_I, LANES), F32)
        for j in range(RWKV_HEAD_DIM):
            h = (s_ref[j] * d_ref[0, t, pl.ds(j, 1), :] + u * b_ref[0, t, pl.ds(j, 1), :]
                 + vt * k_ref[0, t, pl.ds(j, 1), :])
            s_ref[j] = h
            o = o + h * r_ref[0, t, pl.ds(j, 1), :]
        o_ref[0, t] = o
        return carry

    lax.fori_loop(0, tc, step, 0)
    sout_ref[0] = s_ref[...]


def _rwkv_scan(r, d, k, kk, b, v, s0, *, tc):
    pg, t = r.shape[:2]
    vec = pl.BlockSpec((1, tc, RWKV_HEAD_DIM, LANES), lambda g, c: (g, c, 0, 0))
    val = pl.BlockSpec((1, tc, RW_I, LANES), lambda g, c: (g, c, 0, 0))
    st = pl.BlockSpec((1, RWKV_HEAD_DIM, RW_I, LANES), lambda g, c: (g, 0, 0, 0))
    return pl.pallas_call(
        functools.partial(_rwkv_scan_kernel, tc=tc),
        grid=(pg, t // tc),
        in_specs=[vec, vec, vec, vec, vec, val, st],
        out_specs=[val, st],
        out_shape=[jax.ShapeDtypeStruct((pg, t, RW_I, LANES), F32),
                   jax.ShapeDtypeStruct((pg, RWKV_HEAD_DIM, RW_I, LANES), F32)],
        scratch_shapes=[pltpu.VMEM((RWKV_HEAD_DIM, RW_I, LANES), F32)],
        compiler_params=_cparams(("parallel", "arbitrary")),
    )(r, d, k, kk, b, v, s0)


def _rwkv_post_kernel(o_ref, bonus_ref, g_ref, lg_ref, lb_ref, out_ref):
    o = o_ref[...]
    inv = 1.0 / RWKV_HEAD_DIM
    mean = _seg_sum(o, RWKV_HEAD_DIM) * inv
    cen = o - mean
    var = _seg_sum(cen * cen, RWKV_HEAD_DIM) * inv
    y = cen * lax.rsqrt(var + LNX_EPS) * lg_ref[...] + lb_ref[...]
    out_ref[...] = (y + bonus_ref[...]) * g_ref[...]


def _rwkv_post(o, bonus, g, lnx_g, lnx_b, *, tm):
    m = o.shape[0]
    blk = pl.BlockSpec((tm, RWKV_W), lambda i: (i, 0))
    row = pl.BlockSpec((1, RWKV_W), lambda i: (0, 0))
    return pl.pallas_call(
        _rwkv_post_kernel,
        grid=(m // tm,),
        in_specs=[blk, blk, blk, row, row],
        out_specs=blk,
        out_shape=jax.ShapeDtypeStruct((m, RWKV_W), F32),
        compiler_params=_cparams(("parallel",)),
    )(o, bonus, g, lnx_g, lnx_b)


def _pairs_layout(x, n, t):
    pairs = n * RWKV_HEADS
    pg = pairs // 64
    y = x.reshape(n, t, RWKV_HEADS, RWKV_HEAD_DIM).transpose(1, 3, 0, 2).reshape(t, RWKV_HEAD_DIM, pg, 64)
    y = y.transpose(2, 0, 1, 3)
    return jnp.concatenate([y, y], axis=-1)


def _pairs_layout_v(x, n, t):
    pairs = n * RWKV_HEADS
    pg = pairs // 64
    y = x.reshape(n, t, RWKV_HEADS, 2, RW_I).transpose(1, 4, 3, 0, 2).reshape(t, RW_I, 2, pg, 64)
    return y.transpose(3, 0, 1, 2, 4).reshape(pg, t, RW_I, LANES)


def _pairs_unlayout_v(y, n, t):
    pg = y.shape[0]
    z = y.reshape(pg, t, RW_I, 2, 64).transpose(1, 2, 3, 0, 4).reshape(t, RW_I, 2, n, RWKV_HEADS)
    return z.transpose(3, 0, 4, 2, 1).reshape(n * t, RWKV_W)


def _state_layout(s0):
    n = s0.shape[0]
    pg = n * RWKV_HEADS // 64
    y = s0.reshape(pg, 64, 2, RW_I, RWKV_HEAD_DIM)
    return y.transpose(0, 4, 3, 2, 1).reshape(pg, RWKV_HEAD_DIM, RW_I, LANES)


def _state_unlayout(y, n):
    pg = y.shape[0]
    z = y.reshape(pg, RWKV_HEAD_DIM, RW_I, 2, 64).transpose(0, 4, 3, 2, 1)
    return z.reshape(n, RWKV_HEADS, RWKV_HEAD_DIM, RWKV_HEAD_DIM)


def _outproj_kernel(x_ref, a_ref, b_ref, wa_ref, wb_ref, o_ref):
    y = jnp.dot(a_ref[...].astype(BF16), wa_ref[...], preferred_element_type=F32)
    y = y + jnp.dot(b_ref[...].astype(BF16), wb_ref[...], preferred_element_type=F32)
    o_ref[...] = x_ref[...] + y


def _outproj(x, a, b, wa, wb, *, tm):
    m = x.shape[0]
    return pl.pallas_call(
        _outproj_kernel,
        grid=(m // tm,),
        in_specs=[
            pl.BlockSpec((tm, D_MODEL), lambda i: (i, 0)),
            pl.BlockSpec((tm, a.shape[1]), lambda i: (i, 0)),
            pl.BlockSpec((tm, b.shape[1]), lambda i: (i, 0)),
            pl.BlockSpec(wa.shape, lambda i: (0, 0)),
            pl.BlockSpec(wb.shape, lambda i: (0, 0)),
        ],
        out_specs=pl.BlockSpec((tm, D_MODEL), lambda i: (i, 0)),
        out_shape=jax.ShapeDtypeStruct((m, D_MODEL), F32),
        compiler_params=_cparams(("parallel",)),
    )(x, a, b, wa, wb)


def _memkv_kernel(x_ref, g_ref, w_ref, kg_ref, o_ref):
    xn = _rms(x_ref[...], g_ref[...]).astype(BF16)
    kv = jnp.dot(xn, w_ref[...], preferred_element_type=F32)
    xw = X_HEADS * X_HEAD_DIM
    o_ref[:, 0:xw] = _seg_rms(kv[:, 0:xw], kg_ref[...], X_HEAD_DIM)
    o_ref[:, xw:2 * xw] = kv[:, xw:2 * xw]


def _memkv(mem, g, w, kg, *, tm):
    m = mem.shape[0]
    xw = X_HEADS * X_HEAD_DIM
    return pl.pallas_call(
        _memkv_kernel,
        grid=(m // tm,),
        in_specs=[
            pl.BlockSpec((tm, D_MODEL), lambda i: (i, 0)),
            pl.BlockSpec((1, D_MODEL), lambda i: (0, 0)),
            pl.BlockSpec((D_MODEL, 2 * xw), lambda i: (0, 0)),
            pl.BlockSpec((1, xw), lambda i: (0, 0)),
        ],
        out_specs=pl.BlockSpec((tm, 2 * xw), lambda i: (i, 0)),
        out_shape=jax.ShapeDtypeStruct((m, 2 * xw), F32),
        compiler_params=_cparams(("parallel",)),
    )(mem, g, w, kg)


def _xattn_kernel(x_ref, mkv_ref, g_ref, wq_ref, qg_ref, wo_ref, o_ref):
    x = x_ref[0]
    xn = _rms(x, g_ref[...]).astype(BF16)
    q = jnp.dot(xn, wq_ref[...], preferred_element_type=F32)
    q = _seg_rms(q, qg_ref[...], X_HEAD_DIM) * (X_HEAD_DIM ** -0.5)
    xw = X_HEADS * X_HEAD_DIM
    outs = []
    for h in range(X_HEADS):
        lo = h * X_HEAD_DIM
        k = mkv_ref[0, :, lo:lo + X_HEAD_DIM].astype(BF16)
        v = mkv_ref[0, :, xw + lo:xw + lo + X_HEAD_DIM].astype(BF16)
        s = lax.dot_general(q[:, lo:lo + X_HEAD_DIM].astype(BF16), k, (((1,), (1,)), ((), ())),
                            preferred_element_type=F32)
        e = jnp.exp(s - jnp.max(s, axis=-1, keepdims=True))
        p = e / jnp.sum(e, axis=-1, keepdims=True)
        outs.append(jnp.dot(p.astype(BF16), v, preferred_element_type=F32))
    o = jnp.concatenate(outs, axis=-1).astype(BF16)
    o_ref[0] = x + jnp.dot(o, wo_ref[...], preferred_element_type=F32)


def _xattn(x, mkv, g, wq, qg, wo, *, tm):
    n, t, _ = x.shape
    xw = X_HEADS * X_HEAD_DIM
    n_mem = mkv.shape[1]
    return pl.pallas_call(
        _xattn_kernel,
        grid=(n, t // tm),
        in_specs=[
            pl.BlockSpec((1, tm, D_MODEL), lambda b, i: (b, i, 0)),
            pl.BlockSpec((1, n_mem, 2 * xw), lambda b, i: (b, 0, 0)),
            pl.BlockSpec((1, D_MODEL), lambda b, i: (0, 0)),
            pl.BlockSpec((D_MODEL, xw), lambda b, i: (0, 0)),
            pl.BlockSpec((1, xw), lambda b, i: (0, 0)),
            pl.BlockSpec((xw, D_MODEL), lambda b, i: (0, 0)),
        ],
        out_specs=pl.BlockSpec((1, tm, D_MODEL), lambda b, i: (b, i, 0)),
        out_shape=jax.ShapeDtypeStruct((n, t, D_MODEL), F32),
        compiler_params=_cparams(("parallel", "parallel")),
    )(x, mkv, g, wq, qg, wo)


def _row(v):
    return v.reshape(1, -1).astype(F32)


def _block_diag2(w):
    z = jnp.zeros_like(w)
    return jnp.concatenate([jnp.concatenate([w, z], axis=-1), jnp.concatenate([z, w], axis=-1)], axis=-2)


def _prep_weights(norm_ffn1_g, w_ffn1_gu, w_ffn1_down, norm_mix_g, w_in, w_out, q_norm_g, kc_norm_g, ks_norm_g,
                  kw_norm_g, gate_b, cmp_pe_k, cmp_pe_v, w_cmp_k1, w_cmp_k2, w_cmp_v1, w_cmp_v2, rwkv_mu, rwkv_w0,
                  w_decay2, rwkv_a0, w_iclr2, w_gate2, rwkv_k_k, rwkv_k_a, rwkv_r_k, lnx_g, lnx_b, norm_x_g,
                  norm_mem_g, w_xq, w_xkv, xq_norm_g, xk_norm_g, w_xo, norm_ffn2_g, w_ffn2_gu, w_ffn2_down,
                  norm_out_g):
    G = NSA_KV_HEADS
    W = {}
    W["ffn1"] = (_row(norm_ffn1_g), w_ffn1_gu.astype(BF16), w_ffn1_down.astype(BF16))
    W["ffn2"] = (_row(norm_ffn2_g), w_ffn2_gu.astype(BF16), w_ffn2_down.astype(BF16))
    W["norm_out_g"] = _row(norm_out_g)
    w_pad = jnp.concatenate([w_in[:, :NSA_COLS], jnp.zeros((D_MODEL, NSA_PAD - NSA_COLS), F32), w_in[:, NSA_COLS:]],
                            axis=1).astype(BF16)
    head_g = jnp.zeros((NSA_PAD,), F32)
    head_g = head_g.at[0:NSA_Q_W].set(jnp.tile(q_norm_g, NSA_HEADS))
    c = NSA_Q_W + 2 * NSA_KV_W
    head_g = head_g.at[c:c + NSA_KV_W].set(jnp.tile(ks_norm_g, G))
    c = NSA_Q_W + 4 * NSA_KV_W
    head_g = head_g.at[c:c + NSA_KV_W].set(jnp.tile(kw_norm_g, G))
    gb = jnp.zeros((LANES,), F32).at[0:3 * NSA_HEADS].set(gate_b)
    W["proj"] = (_row(norm_mix_g), w_pad, _row(head_g), _row(gb))
    pe = jnp.concatenate([cmp_pe_k, cmp_pe_k, cmp_pe_v, cmp_pe_v], axis=-1)[:, None, :]
    W["cmp"] = (pe, _block_diag2(w_cmp_k1).astype(BF16), _block_diag2(w_cmp_v1).astype(BF16),
                _block_diag2(w_cmp_k2).astype(BF16), _block_diag2(w_cmp_v2).astype(BF16),
                _row(jnp.tile(kc_norm_g, G)))
    zl = jnp.zeros((DECAY_LORA, RWKV_W), F32)
    W["rwkv_pre"] = (_row(rwkv_mu), _row(rwkv_w0), jnp.concatenate([w_decay2, zl], axis=0).astype(BF16),
                     _row(rwkv_a0), jnp.concatenate([zl, w_iclr2], axis=0).astype(BF16), w_gate2.astype(BF16),
                     _row(rwkv_k_k), _row(rwkv_k_a), _row(rwkv_r_k))
    W["rwkv_post"] = (_row(lnx_g), _row(lnx_b))
    W["w_out"] = (w_out[:NSA_Q_W].astype(BF16), w_out[NSA_Q_W:].astype(BF16))
    W["xattn"] = (_row(norm_x_g), w_xq.astype(BF16), _row(jnp.tile(xq_norm_g, X_HEADS)), w_xo.astype(BF16))
    W["memkv"] = (_row(norm_mem_g), w_xkv.astype(BF16), _row(jnp.tile(xk_norm_g, X_HEADS)))
    return W


def _tile_rows(m, pref):
    return pref if m % pref == 0 else m


def _rwkv_group(pr, shift0, s0, W, n, t):
    m = n * t
    pr3 = pr.reshape(n, t, RWKV_COLS)
    prev = jnp.concatenate([shift0[:, None, :], pr3[:, :-1]], axis=1).reshape(m, RWKV_COLS)
    r, d, k, v, kk, b, g, bonus = _rwkv_pre(pr, prev, W["rwkv_pre"], tm=_tile_rows(m, 256))
    lay = lambda a: _pairs_layout(a, n, t)
    o, s_new = _rwkv_scan(lay(r), lay(d), lay(k), lay(kk), lay(b), _pairs_layout_v(v, n, t), _state_layout(s0),
                          tc=min(t, 32))
    o = _pairs_unlayout_v(o, n, t)
    o = _rwkv_post(o, bonus, g, *W["rwkv_post"], tm=_tile_rows(m, 512))
    return o, _state_unlayout(s_new, n)


def _nsa_prompt_group(pn, W, n, t):
    G = NSA_KV_HEADS
    n_blk = t // CMP_BLOCK
    kc, vc = _compress_rows(pn, NSA_Q_W // LANES, W["cmp"], nblk=n_blk)
    pn3 = pn.reshape(n, t, NSA_PAD)
    col = lambda i: pn3[:, :, NSA_Q_W + i * NSA_KV_W:NSA_Q_W + (i + 1) * NSA_KV_W]
    pos = jnp.arange(t, dtype=jnp.int32)
    chunked = lambda a: a.reshape(n, G, t // KC, KC, LANES)
    ksp = chunked(_key_features(col(2), pos))
    kwp = chunked(_key_features(col(4), pos))
    end = (jnp.arange(n_blk, dtype=jnp.int32) + 1) * CMP_BLOCK - 1
    kcp = _key_features(kc.reshape(n, n_blk, NSA_KV_W), end)
    vct = vc.reshape(n, n_blk, G, HEAD_DIM).transpose(0, 2, 3, 1).astype(BF16)
    gates_t = pn3[:, :, NSA_Q_W + 6 * NSA_KV_W:NSA_COLS].transpose(0, 2, 1)
    o_t = _nsa_prompt(_query_features(pn3[:, :, :NSA_Q_W]), kcp, vct, ksp, _values_t(col(3), KC), kwp,
                      _values_t(col(5), KC), gates_t)
    return o_t.transpose(0, 2, 1).reshape(n * t, NSA_Q_W)


def _pad_axis(a, axis, size):
    pad = [(0, 0)] * a.ndim
    pad[axis] = (0, size - a.shape[axis])
    return jnp.pad(a, pad)


def _nsa_sample_group(pn, W, n, t_new, cache_kv, page_table, cache_win):
    G = NSA_KV_HEADS
    assert t_new <= CMP_BLOCK and t_new <= TQ
    past = page_table.shape[1] * PAGE_SIZE
    nb_past = past // CMP_BLOCK
    w_buf = cache_win.shape[1]
    cache2 = cache_kv.reshape(cache_kv.shape[0], PAGE_SIZE, 4 * G * HEAD_DIM)
    kc_past, vc_past = _compress_paged(page_table, cache2, W["cmp"])
    pn3 = pn.reshape(n, t_new, NSA_PAD)
    col = lambda i: pn3[:, :, NSA_Q_W + i * NSA_KV_W:NSA_Q_W + (i + 1) * NSA_KV_W]
    new_rows = _pad_axis(pn3[:, :, NSA_Q_W:NSA_Q_W + 2 * NSA_KV_W], 1, CMP_BLOCK).reshape(n * CMP_BLOCK, 2 * NSA_KV_W)
    kc_new, vc_new = _compress_rows(new_rows, 0, W["cmp"], nblk=n)
    nb_pad = -(-(nb_past + 1) // 16) * 16
    kc = _pad_axis(jnp.concatenate([kc_past, kc_new[:, None]], axis=1), 1, nb_pad)
    vc = _pad_axis(jnp.concatenate([vc_past, vc_new[:, None]], axis=1), 1, nb_pad)
    end_rel = (jnp.arange(nb_pad, dtype=jnp.int32) + 1) * CMP_BLOCK - 1 - past
    kcp = _key_features(kc, end_rel)
    vct = vc.reshape(n, nb_pad, G, HEAD_DIM).transpose(0, 2, 3, 1).astype(BF16)
    n_win = -(-(w_buf + t_new) // KC) * KC
    cw = cache_win.reshape(n, w_buf, 2, NSA_KV_W)
    kw_all = _pad_axis(jnp.concatenate([cw[:, :, 0], col(4)], axis=1), 1, n_win)
    vw_all = _pad_axis(jnp.concatenate([cw[:, :, 1], col(5)], axis=1), 1, n_win)
    kwp = _key_features(kw_all, jnp.arange(n_win, dtype=jnp.int32) - w_buf).reshape(n, G, n_win // KC, KC, LANES)
    vwt = _values_t(vw_all, KC)
    heads = lambda a: a.reshape(n, t_new, G, HEAD_DIM).transpose(0, 2, 1, 3)
    knew = _pad_axis(heads(col(2)), 2, KC).astype(BF16)
    vnewt = _pad_axis(heads(col(3)).transpose(0, 1, 3, 2), 3, KC).astype(BF16)
    qp = _query_features(_pad_axis(pn3[:, :, :NSA_Q_W], 1, TQ))
    gates_t = _pad_axis(pn3[:, :, NSA_Q_W + 6 * NSA_KV_W:NSA_COLS], 1, TQ).transpose(0, 2, 1)
    o_t = _nsa_sample(page_table, qp, kcp, vct, kwp, vwt, knew, vnewt, gates_t, cache2, past=past, n_new=t_new)
    return o_t[:, :, :t_new].transpose(0, 2, 1).reshape(n * t_new, NSA_Q_W)


def _layer(x, mkv, shift0, s0, W, nsa_fn):
    n, t, _ = x.shape
    m = n * t
    G = NSA_KV_HEADS
    x2 = x.reshape(m, D_MODEL)
    tm = _tile_rows(m, 512)
    x2 = _ffn(x2, *W["ffn1"], W["norm_out_g"], final_norm=False, tm=tm, tf=D_FF // 2)
    pn, pr = _proj(x2, *W["proj"], tm=_tile_rows(m, 256))
    o_nsa = nsa_fn(pn)
    o_rwkv, s_new = _rwkv_group(pr, shift0, s0, W, n, t)
    x2 = _outproj(x2, o_nsa, o_rwkv, *W["w_out"], tm=tm)
    x3 = _xattn(x2.reshape(n, t, D_MODEL), mkv, *W["xattn"], tm=_tile_rows(t, 512))
    y = _ffn(x3.reshape(m, D_MODEL), *W["ffn2"], W["norm_out_g"], final_norm=True, tm=tm, tf=D_FF // 2)
    pn3 = pn.reshape(n, t, NSA_PAD)
    kv_rows = pn3[:, :, NSA_Q_W:NSA_Q_W + 4 * NSA_KV_W].reshape(n, t, 4, G, HEAD_DIM)
    win_new = pn3[:, :, NSA_Q_W + 4 * NSA_KV_W:NSA_Q_W + 6 * NSA_KV_W].reshape(n, t, 2, G, HEAD_DIM)
    shift_new = pr.reshape(n, t, RWKV_COLS)[:, -1]
    return y.reshape(n, t, D_MODEL), kv_rows, win_new, s_new, shift_new


def kernel(x_prompt, x_sample, cache_nsa_kv, cache_nsa_win, state_rwkv_s, state_rwkv_shift, cache_mem_kv, page_table, mem_prompt, norm_ffn1_g, w_ffn1_gu, w_ffn1_down, norm_mix_g, w_in, w_out, q_norm_g, kc_norm_g, ks_norm_g, kw_norm_g, gate_b, cmp_pe_k, cmp_pe_v, w_cmp_k1, w_cmp_k2, w_cmp_v1, w_cmp_v2, rwkv_mu, rwkv_w0, w_decay2, rwkv_a0, w_iclr2, w_gate2, rwkv_k_k, rwkv_k_a, rwkv_r_k, lnx_g, lnx_b, norm_x_g, norm_mem_g, w_xq, w_xkv, xq_norm_g, xk_norm_g, w_xo, norm_ffn2_g, w_ffn2_gu, w_ffn2_down, norm_out_g):
    layer_weights = (norm_ffn1_g, w_ffn1_gu, w_ffn1_down, norm_mix_g, w_in, w_out, q_norm_g, kc_norm_g, ks_norm_g,
                     kw_norm_g, gate_b, cmp_pe_k, cmp_pe_v, w_cmp_k1, w_cmp_k2, w_cmp_v1, w_cmp_v2, rwkv_mu, rwkv_w0,
                     w_decay2, rwkv_a0, w_iclr2, w_gate2, rwkv_k_k, rwkv_k_a, rwkv_r_k, lnx_g, lnx_b, norm_x_g,
                     norm_mem_g, w_xq, w_xkv, xq_norm_g, xk_norm_g, w_xo, norm_ffn2_g, w_ffn2_gu, w_ffn2_down,
                     norm_out_g)
    assert w_in.shape[0] == 1, "single-layer trunk"
    W = _prep_weights(*(w[0] for w in layer_weights))
    n_p, t_p, _ = x_prompt.shape
    n_s, t_s, _ = x_sample.shape
    n_mem = mem_prompt.shape[1]
    xw = X_HEADS * X_HEAD_DIM

    mkv_p = _memkv(mem_prompt.reshape(n_p * n_mem, D_MODEL), *W["memkv"], tm=_tile_rows(n_p * n_mem, 512))
    mkv_p = mkv_p.reshape(n_p, n_mem, 2 * xw)
    y_p, kv_p, win_p, rs_p, sh_p = _layer(
        x_prompt, mkv_p, jnp.zeros((n_p, RWKV_COLS), F32),
        jnp.zeros((n_p, RWKV_HEADS, RWKV_HEAD_DIM, RWKV_HEAD_DIM), F32), W,
        lambda pn: _nsa_prompt_group(pn, W, n_p, t_p))
    win_p = win_p[:, t_p - min(WINDOW, t_p):]

    mkv_s = cache_mem_kv[0].reshape(n_s, n_mem, 2 * xw)
    y_s, kv_s, win_new, rs_s, sh_s = _layer(
        x_sample, mkv_s, state_rwkv_shift[0], state_rwkv_s[0], W,
        lambda pn: _nsa_sample_group(pn, W, n_s, t_s, cache_nsa_kv[0], page_table, cache_nsa_win[0]))
    win_s = jnp.concatenate([cache_nsa_win[0], win_new], axis=1)[:, t_s:]

    mkv_out = mkv_p.reshape(1, n_p, n_mem, 2, X_HEADS, X_HEAD_DIM)
    return (y_p, y_s, kv_p[None], kv_s[None], win_p[None], win_s[None], rs_p[None], rs_s[None], sh_p[None],
            sh_s[None], mkv_out)
```

```python
import functools
import math

import jax
import jax.numpy as jnp
from jax import lax
from jax.experimental import pallas as pl
from jax.experimental.pallas import tpu as pltpu

F32 = jnp.float32
BF16 = jnp.bfloat16

D_MODEL = 1024
NSA_HEADS = 8
NSA_KV_HEADS = 2
NSA_REP = NSA_HEADS // NSA_KV_HEADS
HEAD_DIM = 64
CMP_BLOCK = 64
N_SEL = 16
WINDOW = 512
CMP_HIDDEN = 128
PAGE_SIZE = 128
RWKV_HEADS = 8
RWKV_HEAD_DIM = 64
RWKV_W = RWKV_HEADS * RWKV_HEAD_DIM
DECAY_LORA = 64
ICLR_LORA = 64
GATE_LORA = 128
NSA_Q_W = NSA_HEADS * HEAD_DIM
NSA_KV_W = NSA_KV_HEADS * HEAD_DIM
NSA_COLS = NSA_Q_W + 6 * NSA_KV_W + 3 * NSA_HEADS
RWKV_COLS = 3 * RWKV_W + DECAY_LORA + ICLR_LORA + GATE_LORA
X_HEADS = 4
X_HEAD_DIM = 128
D_FF = 2816
RMS_EPS = 1e-6
LNX_EPS = 64e-5
FORCED_SCORE = 1e9

LANES = 128
NSA_PAD = 1408
P_COLS = NSA_PAD + RWKV_COLS
TQ = 128
KC = 128
VMEM_LIMIT = 56 * 1024 * 1024

ALIBI = tuple(2.0 ** (-8.0 * (h + 1.0) / NSA_HEADS) for h in range(NSA_HEADS))


def _cparams(sem):
    return pltpu.CompilerParams(dimension_semantics=sem, vmem_limit_bytes=VMEM_LIMIT)


def _rms(x, g):
    return x * lax.rsqrt(jnp.mean(x * x, axis=-1, keepdims=True) + RMS_EPS) * g


def _seg_ones(width, seg):
    r = lax.broadcasted_iota(jnp.int32, (width, width), 0) // seg
    c = lax.broadcasted_iota(jnp.int32, (width, width), 1) // seg
    return (r == c).astype(F32)


def _seg_sum(x, seg):
    ones = _seg_ones(LANES, seg)
    parts = [jnp.dot(x[:, c:c + LANES], ones, precision=lax.Precision.HIGHEST, preferred_element_type=F32)
             for c in range(0, x.shape[1], LANES)]
    return parts[0] if len(parts) == 1 else jnp.concatenate(parts, axis=1)


def _seg_rms(x, g, seg):
    if seg == LANES:
        parts = [_rms(x[:, c:c + LANES], g[:, c:c + LANES]) for c in range(0, x.shape[1], LANES)]
        return parts[0] if len(parts) == 1 else jnp.concatenate(parts, axis=1)
    return x * lax.rsqrt(_seg_sum(x * x, seg) * (1.0 / seg) + RMS_EPS) * g


def _ffn_kernel(x_ref, g_ref, wg_ref, wu_ref, wd_ref, go_ref, o_ref, xn_ref, acc_ref, *, final_norm):
    f = pl.program_id(1)

    @pl.when(f == 0)
    def _():
        xn_ref[...] = _rms(x_ref[...], g_ref[...]).astype(BF16)
        acc_ref[...] = jnp.zeros_like(acc_ref)

    xn = xn_ref[...]
    gate = jnp.dot(xn, wg_ref[...], preferred_element_type=F32)
    up = jnp.dot(xn, wu_ref[...], preferred_element_type=F32)
    h = (gate * jax.nn.sigmoid(gate) * up).astype(BF16)
    acc_ref[...] += jnp.dot(h, wd_ref[...], preferred_element_type=F32)

    @pl.when(f == pl.num_programs(1) - 1)
    def _():
        y = x_ref[...] + 0.5 * acc_ref[...]
        if final_norm:
            y = _rms(y, go_ref[...])
        o_ref[...] = y


def _ffn(x, g, w_gu, w_down, g_out, *, final_norm, tm, tf):
    m = x.shape[0]
    nf = D_FF // tf
    return pl.pallas_call(
        functools.partial(_ffn_kernel, final_norm=final_norm),
        grid=(m // tm, nf),
        in_specs=[
            pl.BlockSpec((tm, D_MODEL), lambda i, f: (i, 0)),
            pl.BlockSpec((1, D_MODEL), lambda i, f: (0, 0)),
            pl.BlockSpec((D_MODEL, tf), lambda i, f: (0, f)),
            pl.BlockSpec((D_MODEL, tf), lambda i, f: (0, nf + f)),
            pl.BlockSpec((tf, D_MODEL), lambda i, f: (f, 0)),
            pl.BlockSpec((1, D_MODEL), lambda i, f: (0, 0)),
        ],
        out_specs=pl.BlockSpec((tm, D_MODEL), lambda i, f: (i, 0)),
        out_shape=jax.ShapeDtypeStruct((m, D_MODEL), F32),
        scratch_shapes=[pltpu.VMEM((tm, D_MODEL), BF16), pltpu.VMEM((tm, D_MODEL), F32)],
        compiler_params=_cparams(("parallel", "arbitrary")),
        name="ffn",
    )(x, g, w_gu, w_gu, w_down, g_out)


def _proj_kernel(x_ref, g_ref, w_ref, hg_ref, gb_ref, on_ref, or_ref):
    xn = _rms(x_ref[...], g_ref[...]).astype(BF16)
    p = jnp.dot(xn, w_ref[...], preferred_element_type=F32)
    or_ref[...] = p[:, NSA_PAD:]
    on_ref[...] = p[:, 0:NSA_PAD]
    hg = hg_ref[...]
    on_ref[:, 0:NSA_Q_W] = _seg_rms(p[:, 0:NSA_Q_W], hg[:, 0:NSA_Q_W], HEAD_DIM)
    for c in (NSA_Q_W + 2 * NSA_KV_W, NSA_Q_W + 4 * NSA_KV_W):
        on_ref[:, c:c + NSA_KV_W] = _seg_rms(p[:, c:c + NSA_KV_W], hg[:, c:c + NSA_KV_W], HEAD_DIM)
    c = NSA_Q_W + 6 * NSA_KV_W
    on_ref[:, c:c + LANES] = jax.nn.sigmoid(p[:, c:c + LANES] + gb_ref[...])


def _proj(x, g, w_pad, head_g, gate_b, *, tm):
    m = x.shape[0]
    return pl.pallas_call(
        _proj_kernel,
        grid=(m // tm,),
        in_specs=[
            pl.BlockSpec((tm, D_MODEL), lambda i: (i, 0)),
            pl.BlockSpec((1, D_MODEL), lambda i: (0, 0)),
            pl.BlockSpec((D_MODEL, P_COLS), lambda i: (0, 0)),
            pl.BlockSpec((1, NSA_PAD), lambda i: (0, 0)),
            pl.BlockSpec((1, LANES), lambda i: (0, 0)),
        ],
        out_specs=[pl.BlockSpec((tm, NSA_PAD), lambda i: (i, 0)), pl.BlockSpec((tm, RWKV_COLS), lambda i: (i, 0))],
        out_shape=[jax.ShapeDtypeStruct((m, NSA_PAD), F32), jax.ShapeDtypeStruct((m, RWKV_COLS), F32)],
        compiler_params=_cparams(("parallel",)),
        name="proj",
    )(x, g, w_pad, head_g, gate_b)


def _compress_core(xk_ref, xv_ref, nblk, pe_ref, w1k_ref, w1v_ref, w2k_ref, w2v_ref, kcg_ref):
    def body(j, carry):
        acc_k, acc_v = carry
        pe = pe_ref[j]
        xk = xk_ref[pl.ds(j, nblk, stride=CMP_BLOCK), :] + pe[:, 0:LANES]
        xv = xv_ref[pl.ds(j, nblk, stride=CMP_BLOCK), :] + pe[:, LANES:2 * LANES]
        acc_k = acc_k + jnp.dot(xk.astype(BF16), w1k_ref[j], preferred_element_type=F32)
        acc_v = acc_v + jnp.dot(xv.astype(BF16), w1v_ref[j], preferred_element_type=F32)
        return acc_k, acc_v

    zero = jnp.zeros((nblk, 2 * CMP_HIDDEN), F32)
    acc_k, acc_v = lax.fori_loop(0, CMP_BLOCK, body, (zero, zero))
    hk = jax.nn.gelu(acc_k).astype(BF16)
    hv = jax.nn.gelu(acc_v).astype(BF16)
    kc = jnp.dot(hk, w2k_ref[...], preferred_element_type=F32)
    vc = jnp.dot(hv, w2v_ref[...], preferred_element_type=F32)
    return _seg_rms(kc, kcg_ref[...], HEAD_DIM), vc


def _compress_kernel(xk_ref, xv_ref, pe_ref, w1k_ref, w1v_ref, w2k_ref, w2v_ref, kcg_ref, kc_ref, vc_ref, *, nblk):
    kc, vc = _compress_core(xk_ref, xv_ref, nblk, pe_ref, w1k_ref, w1v_ref, w2k_ref, w2v_ref, kcg_ref)
    kc_ref[...] = kc
    vc_ref[...] = vc


def _cmp_weight_specs(imap):
    return [
        pl.BlockSpec((CMP_BLOCK, 1, 2 * LANES), imap(3)),
        pl.BlockSpec((CMP_BLOCK, LANES, 2 * CMP_HIDDEN), imap(3)),
        pl.BlockSpec((CMP_BLOCK, LANES, 2 * CMP_HIDDEN), imap(3)),
        pl.BlockSpec((2 * CMP_HIDDEN, LANES), imap(2)),
        pl.BlockSpec((2 * CMP_HIDDEN, LANES), imap(2)),
        pl.BlockSpec((1, LANES), imap(2)),
    ]


def _compress_rows(rows, col_block, cw, *, nblk):
    m = rows.shape[0]
    steps = m // (nblk * CMP_BLOCK)
    imap = lambda nd: (lambda i: (0,) * nd)
    return pl.pallas_call(
        functools.partial(_compress_kernel, nblk=nblk),
        grid=(steps,),
        in_specs=[pl.BlockSpec((nblk * CMP_BLOCK, LANES), lambda i: (i, col_block)),
                  pl.BlockSpec((nblk * CMP_BLOCK, LANES), lambda i: (i, col_block + 1))] + _cmp_weight_specs(imap),
        out_specs=[pl.BlockSpec((nblk, LANES), lambda i: (i, 0))] * 2,
        out_shape=[jax.ShapeDtypeStruct((steps * nblk, LANES), F32)] * 2,
        compiler_params=_cparams(("parallel",)),
        name="compress",
    )(rows, rows, *cw)


def _compress_paged_kernel(pt_ref, cache_ref, pe_ref, w1k_ref, w1v_ref, w2k_ref, w2v_ref, kcg_ref,
                           kc_ref, vc_ref, bufk_ref, bufv_ref, sem, *, n_pages):
    n = pl.program_id(0)

    def page_copies(i):
        rows = pl.ds(i * PAGE_SIZE, PAGE_SIZE)
        page = pt_ref[n, i]
        return (pltpu.make_async_copy(cache_ref.at[page, :, pl.ds(0, LANES)], bufk_ref.at[rows, :], sem.at[0]),
                pltpu.make_async_copy(cache_ref.at[page, :, pl.ds(LANES, LANES)], bufv_ref.at[rows, :], sem.at[1]))

    def start(i, c):
        for cp in page_copies(i):
            cp.start()
        return c

    def wait(i, c):
        for cp in page_copies(i):
            cp.wait()
        return c

    lax.fori_loop(0, n_pages, start, 0)
    lax.fori_loop(0, n_pages, wait, 0)
    nblk = n_pages * (PAGE_SIZE // CMP_BLOCK)
    kc, vc = _compress_core(bufk_ref, bufv_ref, nblk, pe_ref, w1k_ref, w1v_ref, w2k_ref, w2v_ref, kcg_ref)
    kc_ref[0] = kc
    vc_ref[0] = vc


def _compress_paged(page_table, cache, cw):
    nb, n_pages = page_table.shape
    nblk = n_pages * (PAGE_SIZE // CMP_BLOCK)
    imap = lambda nd: (lambda n, pt: (0,) * nd)
    return pl.pallas_call(
        functools.partial(_compress_paged_kernel, n_pages=n_pages),
        grid_spec=pltpu.PrefetchScalarGridSpec(
            num_scalar_prefetch=1,
            grid=(nb,),
            in_specs=[pl.BlockSpec(memory_space=pl.ANY)] + _cmp_weight_specs(imap),
            out_specs=[pl.BlockSpec((1, nblk, LANES), lambda n, pt: (n, 0, 0))] * 2,
            scratch_shapes=[pltpu.VMEM((n_pages * PAGE_SIZE, LANES), F32), pltpu.VMEM((n_pages * PAGE_SIZE, LANES), F32),
                            pltpu.SemaphoreType.DMA((2,))],
        ),
        out_shape=[jax.ShapeDtypeStruct((nb, nblk, LANES), F32)] * 2,
        compiler_params=_cparams(("arbitrary",)),
        name="compress_paged",
    )(page_table, cache, *cw)


def _gate_sum(gt_ref, g, r, o_c, o_s, o_w):
    row = (g * NSA_REP + r) * 3
    return (gt_ref[0, pl.ds(row, 1), :] * o_c + gt_ref[0, pl.ds(row + 1, 1), :] * o_s
            + gt_ref[0, pl.ds(row + 2, 1), :] * o_w)


QW = NSA_REP * TQ
SEL_KC = 512
RANK_UNROLL = 8


def _wide_init(width=QW):
    return (jnp.full((1, width), -jnp.inf, F32), jnp.zeros((1, width), F32), jnp.zeros((HEAD_DIM, width), F32))


def _wide_step(state, s, vt):
    m, l, acc = state
    m_new = jnp.maximum(m, jnp.max(s, axis=0, keepdims=True))
    m_safe = jnp.where(m_new == -jnp.inf, 0.0, m_new)
    alpha = jnp.exp(m - m_safe)
    p = jnp.exp(s - m_safe)
    l = alpha * l + jnp.sum(p, axis=0, keepdims=True)
    acc = alpha * acc + jnp.dot(vt, p.astype(BF16), preferred_element_type=F32)
    return m_new, l, acc


def _wide_out(state):
    _, l, acc = state
    return acc / jnp.maximum(l, 1e-30)


def _cmp_select_wide(qt, kcp, vct, t_row, n_blk, nb_pad, score_ref, selb_ref, head_sum, expand):
    w = qt.shape[1]
    b_col = lax.broadcasted_iota(jnp.int32, (nb_pad, w), 0)
    valid = (t_row >= (b_col + 1) * CMP_BLOCK - 1) & (b_col < n_blk)
    s = jnp.where(valid, jnp.dot(kcp, qt, preferred_element_type=F32), -jnp.inf)
    m = jnp.max(s, axis=0, keepdims=True)
    e = jnp.exp(s - jnp.where(m == -jnp.inf, 0.0, m))
    p = e / jnp.maximum(jnp.sum(e, axis=0, keepdims=True), 1e-30)
    o_c = jnp.dot(vct, p.astype(BF16), preferred_element_type=F32)
    imp = head_sum(p)
    ws = imp.shape[1]
    b1 = b_col[:, 0:ws]
    cur = t_row[:, 0:ws] // CMP_BLOCK
    forced = (b1 == 0) | (b1 == cur) | (b1 == cur - 1)
    score = jnp.where(forced, FORCED_SCORE, jnp.where(b1 <= cur, imp, -FORCED_SCORE))
    score = jnp.where(b1 < n_blk, score, -jnp.inf)
    score_ref[...] = score

    def rank_body(it, cnt):
        base = pl.multiple_of(it * RANK_UNROLL, RANK_UNROLL)
        for u in range(RANK_UNROLL):
            bp = base + u
            row = jnp.broadcast_to(score_ref[pl.ds(bp, 1), :], (nb_pad, ws))
            ahead = (row > score) | ((row == score) & (b1 > bp))
            cnt = cnt + jnp.where(ahead, 1.0, 0.0)
        return cnt

    cnt = lax.fori_loop(0, nb_pad // RANK_UNROLL, rank_body, jnp.zeros((nb_pad, ws), F32))
    selb_ref[...] = expand(jnp.where((cnt < N_SEL) & (b1 < n_blk), 0.0, -jnp.inf))
    return o_c


def _sum_lane_chunks(p):
    imp = p[:, 0:TQ]
    for r in range(1, NSA_REP):
        imp = imp + p[:, r * TQ:(r + 1) * TQ]
    return imp


def _nsa_prompt_kernel(qt_ref, kcp_ref, vct_ref, ksp_ref, vst_ref, kwp_ref, vwt_ref, gt_ref, o_ref,
                       score_ref, selb_ref, *, n_blk):
    i = pl.program_id(1)
    t0 = i * TQ
    t_row = t0 + (lax.broadcasted_iota(jnp.int32, (1, QW), 1) & (TQ - 1))
    blocks_per_step = SEL_KC // CMP_BLOCK
    chunks_per_step = SEL_KC // KC
    wk = WINDOW + TQ
    ws = pl.multiple_of(jnp.maximum(t0 - WINDOW, 0), TQ)
    dist = t_row - (ws + lax.broadcasted_iota(jnp.int32, (wk, QW), 0))
    wbias = jnp.where((dist >= 0) & (dist < WINDOW), 0.0, -jnp.inf)
    n_steps = (t0 + TQ + SEL_KC - 1) // SEL_KC

    for g in range(NSA_KV_HEADS):
        qt = qt_ref[0, g, 0]
        o_c = _cmp_select_wide(qt, kcp_ref[0, g], vct_ref[0, g], t_row, n_blk, n_blk, score_ref, selb_ref,
                               _sum_lane_chunks, lambda a: jnp.concatenate([a] * NSA_REP, axis=1))

        def step_inputs(c):
            k0 = pl.multiple_of(c * SEL_KC, SEL_KC)
            s = jnp.dot(ksp_ref[0, g, pl.ds(k0, SEL_KC), :], qt, preferred_element_type=F32)
            s = jnp.concatenate(
                [s[b * CMP_BLOCK:(b + 1) * CMP_BLOCK, :] + selb_ref[pl.ds(c * blocks_per_step + b, 1), :]
                 for b in range(blocks_per_step)], axis=0)
            vt = jnp.concatenate([vst_ref[0, g, c * chunks_per_step + j] for j in range(chunks_per_step)], axis=1)
            return s, vt

        def sel_body(c, state):
            return _wide_step(state, *step_inputs(c))

        state = lax.fori_loop(0, n_steps - 1, sel_body, _wide_init())
        c_last = n_steps - 1
        s, vt = step_inputs(c_last)
        pos = c_last * SEL_KC + lax.broadcasted_iota(jnp.int32, (SEL_KC, QW), 0)
        o_s = _wide_out(_wide_step(state, jnp.where(pos <= t_row, s, -jnp.inf), vt))

        s = jnp.dot(kwp_ref[0, g, pl.ds(ws, wk), :], qt, preferred_element_type=F32) + wbias
        vt = jnp.concatenate([vwt_ref[0, g, ws // KC + j] for j in range(wk // KC)], axis=1)
        o_w = _wide_out(_wide_step(_wide_init(), s, vt))

        for r in range(NSA_REP):
            h = g * NSA_REP + r
            lanes = slice(r * TQ, (r + 1) * TQ)
            o = _gate_sum(gt_ref, g, r, o_c[:, lanes], o_s[:, lanes], o_w[:, lanes])
            o_ref[0, h * HEAD_DIM:(h + 1) * HEAD_DIM, :] = o


def _nsa_prompt(qt, kcp, vct, ksp, vst, kwp, vwt, gates_t):
    n, G, n_tiles = qt.shape[:3]
    t = n_tiles * TQ
    n_blk = t // CMP_BLOCK
    n_chunks = t // KC
    full = lambda shape: pl.BlockSpec((1,) + shape, lambda b, i: (b,) + (0,) * len(shape))
    return pl.pallas_call(
        functools.partial(_nsa_prompt_kernel, n_blk=n_blk),
        grid=(n, n_tiles),
        in_specs=[
            pl.BlockSpec((1, G, 1, LANES, QW), lambda b, i: (b, 0, i, 0, 0)),
            full((G, n_blk, LANES)),
            full((G, HEAD_DIM, n_blk)),
            full((G, t, LANES)),
            full((G, n_chunks, HEAD_DIM, KC)),
            full((G, t, LANES)),
            full((G, n_chunks, HEAD_DIM, KC)),
            pl.BlockSpec((1, 3 * NSA_HEADS, TQ), lambda b, i: (b, 0, i)),
        ],
        out_specs=pl.BlockSpec((1, NSA_Q_W, TQ), lambda b, i: (b, 0, i)),
        out_shape=jax.ShapeDtypeStruct((n, NSA_Q_W, t), F32),
        scratch_shapes=[pltpu.VMEM((n_blk, TQ), F32), pltpu.VMEM((n_blk, QW), F32)],
        compiler_params=_cparams(("parallel", "arbitrary")),
        name="nsa_prompt",
    )(qt, kcp, vct, ksp, vst, kwp, vwt, gates_t)


def _query_tiles_t(qp):
    n, t, _ = qp.shape
    x = qp.reshape(n, t // TQ, TQ, NSA_KV_HEADS, NSA_REP, LANES)
    return x.transpose(0, 3, 1, 5, 4, 2).reshape(n, NSA_KV_HEADS, t // TQ, LANES, QW)


PAGES_PER_STEP = 4
PAGE_GROUP = 8


def _nsa_sample_kernel(pt_ref, qt_ref, kcp_ref, vct_ref, kwp_ref, vwt_ref, knew_ref, vnewt_ref, gt_ref, cache_ref,
                       o_ref, score_ref, selb_ref, buf_ref, slot_ref, sem, *, past, n_new, n_blk, nb_pad, n_pages):
    n = pl.program_id(0)
    G = NSA_KV_HEADS
    n_real = NSA_REP * n_new
    blocks_per_page = PAGE_SIZE // CMP_BLOCK
    nb_past = past // CMP_BLOCK
    w_buf = min(WINDOW, past)
    lane = lax.broadcasted_iota(jnp.int32, (1, LANES), 1)
    t_row = past + lane % n_new
    real = lane < n_real
    li = lax.broadcasted_iota(jnp.int32, (LANES, LANES), 0)
    lj = lax.broadcasted_iota(jnp.int32, (LANES, LANES), 1)
    same_token = ((li % n_new == lj % n_new) & (li < n_real)).astype(F32)
    head_sum = lambda p: jnp.dot(p, same_token, precision=lax.Precision.HIGHEST, preferred_element_type=F32)

    o_c = [_cmp_select_wide(qt_ref[0, g], kcp_ref[0, g], vct_ref[0, g], t_row, n_blk, nb_pad, score_ref,
                            selb_ref.at[g], head_sum, lambda a: a) for g in range(G)]

    def page_copy(lp, slot):
        return pltpu.make_async_copy(cache_ref.at[pt_ref[n, lp], :, pl.ds(2 * LANES, 2 * LANES)],
                                     buf_ref.at[pl.ds(slot * PAGE_SIZE, PAGE_SIZE), :], sem)

    def wanted_rows(first, count):
        rows = jnp.maximum(selb_ref[0, pl.ds(first, count), :], selb_ref[1, pl.ds(first, count), :])
        return jnp.max(jnp.where(real, rows, -jnp.inf)) > -1.0

    def page_body(lp, cnt):
        wanted = wanted_rows(lp * blocks_per_page, 1)
        for b in range(1, blocks_per_page):
            wanted = wanted | wanted_rows(lp * blocks_per_page + b, 1)

        @pl.when(wanted)
        def _():
            page_copy(lp, cnt).start()
            slot_ref[cnt] = lp

        return cnt + wanted.astype(jnp.int32)

    def group_body(pg, cnt):
        first = pl.multiple_of(pg * (PAGE_GROUP * blocks_per_page), PAGE_GROUP * blocks_per_page)
        return lax.cond(wanted_rows(first, PAGE_GROUP * blocks_per_page),
                        lambda c: lax.fori_loop(pg * PAGE_GROUP, (pg + 1) * PAGE_GROUP, page_body, c),
                        lambda c: c, cnt)

    n_slots = lax.fori_loop(0, n_pages // PAGE_GROUP, group_body, jnp.int32(0))
    n_steps = (n_slots + PAGES_PER_STEP - 1) // PAGES_PER_STEP

    def fill_body(j, c):
        slot = n_slots + j

        @pl.when(slot < n_steps * PAGES_PER_STEP)
        def _():
            page_copy(0, slot).start()
            slot_ref[slot] = -1

        return c

    lax.fori_loop(0, PAGES_PER_STEP - 1, fill_body, 0)

    def wait_body(s, c):
        page_copy(jnp.maximum(slot_ref[s], 0), s).wait()
        return c

    lax.fori_loop(0, n_steps * PAGES_PER_STEP, wait_body, 0)

    eye = (lax.broadcasted_iota(jnp.int32, (HEAD_DIM, HEAD_DIM), 0)
           == lax.broadcasted_iota(jnp.int32, (HEAD_DIM, HEAD_DIM), 1)).astype(BF16)
    k_idx = lax.broadcasted_iota(jnp.int32, (PAGE_SIZE, LANES), 0)
    w_idx = lax.broadcasted_iota(jnp.int32, (kwp_ref.shape[2], LANES), 0)
    dist_w = t_row - (past - w_buf + w_idx)
    wbias = jnp.where((dist_w >= 0) & (dist_w < WINDOW) & (w_idx < w_buf + n_new), 0.0, -jnp.inf)

    for g in range(G):
        qt = qt_ref[0, g]
        q64 = qt[0:HEAD_DIM, :]
        slope = jnp.zeros((1, LANES), F32)
        for r in range(NSA_REP):
            slope = jnp.where((lane >= r * n_new) & (lane < (r + 1) * n_new), ALIBI[g * NSA_REP + r], slope)

        def pool_scores(k, pos):
            return jnp.dot(k, q64, preferred_element_type=F32) - slope * (t_row - pos).astype(F32)

        def sel_body(st, state):
            base = pl.multiple_of(st * (PAGES_PER_STEP * PAGE_SIZE), PAGES_PER_STEP * PAGE_SIZE)
            rows = pl.ds(base, PAGES_PER_STEP * PAGE_SIZE)
            k = buf_ref[rows, g * HEAD_DIM:(g + 1) * HEAD_DIM].astype(BF16)
            v = buf_ref[rows, LANES + g * HEAD_DIM:LANES + (g + 1) * HEAD_DIM].astype(BF16)
            pos, bias = [], []
            for j in range(PAGES_PER_STEP):
                lp = slot_ref[st * PAGES_PER_STEP + j]
                live = jnp.where(lp >= 0, 0.0, -jnp.inf)
                lp = jnp.maximum(lp, 0)
                pos.append(lp * PAGE_SIZE + k_idx)
                bias += [jnp.broadcast_to(selb_ref[g, pl.ds(lp * blocks_per_page + b, 1), :] + live,
                                          (CMP_BLOCK, LANES)) for b in range(blocks_per_page)]
            s = pool_scores(k, jnp.concatenate(pos, axis=0)) + jnp.concatenate(bias, axis=0)
            vt = lax.dot_general(eye, v, (((1,), (1,)), ((), ())), preferred_element_type=F32).astype(BF16)
            return _wide_step(state, s, vt)

        state = lax.fori_loop(0, n_steps, sel_body, _wide_init(LANES))
        pos_new = past + lax.broadcasted_iota(jnp.int32, (knew_ref.shape[2], LANES), 0)
        s = pool_scores(knew_ref[0, g], pos_new) + selb_ref[g, pl.ds(nb_past, 1), :]
        o_s = _wide_out(_wide_step(state, jnp.where(pos_new <= t_row, s, -jnp.inf), vnewt_ref[0, g]))

        s = jnp.dot(kwp_ref[0, g], qt, preferred_element_type=F32) + wbias
        o_w = _wide_out(_wide_step(_wide_init(LANES), s, vwt_ref[0, g]))

        o_ref[0, g] = (gt_ref[0, g, pl.ds(0, 1), :] * o_c[g] + gt_ref[0, g, pl.ds(1, 1), :] * o_s
                       + gt_ref[0, g, pl.ds(2, 1), :] * o_w)


def _nsa_sample(page_table, qt, kcp, vct, kwp, vwt, knew, vnewt, gates_t, cache, *, past, n_new):
    n = qt.shape[0]
    G = NSA_KV_HEADS
    n_pages = page_table.shape[1]
    assert n_pages % PAGE_GROUP == 0
    n_blk = past // CMP_BLOCK + 1
    nb_pad = kcp.shape[2]
    n_win = kwp.shape[2]
    n_newp = knew.shape[2]
    full = lambda shape: pl.BlockSpec((1,) + shape, lambda b, pt: (b,) + (0,) * len(shape))
    kern = functools.partial(_nsa_sample_kernel, past=past, n_new=n_new, n_blk=n_blk, nb_pad=nb_pad, n_pages=n_pages)
    return pl.pallas_call(
        kern,
        grid_spec=pltpu.PrefetchScalarGridSpec(
            num_scalar_prefetch=1,
            grid=(n,),
            in_specs=[
                full((G, LANES, LANES)),
                full((G, nb_pad, LANES)),
                full((G, HEAD_DIM, nb_pad)),
                full((G, n_win, LANES)),
                full((G, HEAD_DIM, n_win)),
                full((G, n_newp, HEAD_DIM)),
                full((G, HEAD_DIM, n_newp)),
                full((G, 3, LANES)),
                pl.BlockSpec(memory_space=pl.ANY),
            ],
            out_specs=full((G, HEAD_DIM, LANES)),
            scratch_shapes=[
                pltpu.VMEM((nb_pad, LANES), F32),
                pltpu.VMEM((G, nb_pad, LANES), F32),
                pltpu.VMEM((n_pages * PAGE_SIZE, 2 * LANES), F32),
                pltpu.SMEM((n_pages + PAGES_PER_STEP,), jnp.int32),
                pltpu.SemaphoreType.DMA(()),
            ],
        ),
        out_shape=jax.ShapeDtypeStruct((n, G, HEAD_DIM, LANES), F32),
        compiler_params=_cparams(("arbitrary",)),
        name="nsa_sample",
    )(page_table, qt, kcp, vct, kwp, vwt, knew, vnewt, gates_t, cache)


def _alibi_features(pos):
    blk = (pos // CMP_BLOCK).astype(F32)
    off = (pos % CMP_BLOCK).astype(F32)
    pad = jnp.zeros(pos.shape + (HEAD_DIM - 2,), F32)
    return jnp.concatenate([blk[..., None], off[..., None], pad], axis=-1)


def _query_features(q):
    n, t, _ = q.shape
    qh = q.reshape(n, t, NSA_HEADS, HEAD_DIM) * (HEAD_DIM ** -0.5)
    slope = jnp.asarray(ALIBI, F32)
    feat = jnp.zeros((NSA_HEADS, HEAD_DIM), F32).at[:, 0].set(slope * CMP_BLOCK).at[:, 1].set(slope)
    feat = jnp.broadcast_to(feat, (n, t, NSA_HEADS, HEAD_DIM))
    return jnp.concatenate([qh, feat], axis=-1).reshape(n, t, NSA_HEADS * LANES).astype(BF16)


def _key_features(k, pos):
    n, l, _ = k.shape
    kh = k.reshape(n, l, NSA_KV_HEADS, HEAD_DIM).transpose(0, 2, 1, 3)
    feat = jnp.broadcast_to(_alibi_features(pos), (n, NSA_KV_HEADS, l, HEAD_DIM))
    return jnp.concatenate([kh, feat], axis=-1).astype(BF16)


def _values_t(v, chunk):
    n, l, _ = v.shape
    vh = v.reshape(n, l // chunk, chunk, NSA_KV_HEADS, HEAD_DIM)
    return vh.transpose(0, 3, 1, 4, 2).astype(BF16)


def _rwkv_pre_kernel(p_ref, prev_ref, mu_ref, w0_ref, wd_ref, a0_ref, wa_ref, wg_ref, kk_ref, ka_ref, rk_ref,
                     r_o, d_o, k_o, v_o, kk_o, b_o, g_o, bonus_o):
    p = p_ref[...]
    xs = p + (prev_ref[...] - p) * mu_ref[...]
    W = RWKV_W
    r, k, v = xs[:, 0:W], xs[:, W:2 * W], xs[:, 2 * W:3 * W]
    lora = xs[:, 3 * W:3 * W + LANES]
    xg = xs[:, 3 * W + LANES:3 * W + 2 * LANES]
    z = w0_ref[...] + jnp.dot(jnp.tanh(lora).astype(BF16), wd_ref[...], preferred_element_type=F32)
    nz = -z
    softplus = jnp.maximum(nz, 0.0) + jnp.log(1.0 + jnp.exp(-jnp.abs(nz)))
    decay = jnp.exp(-jnp.exp(-softplus - 0.5))
    a = jax.nn.sigmoid(a0_ref[...] + jnp.dot(lora.astype(BF16), wa_ref[...], preferred_element_type=F32))
    g = jnp.dot(jax.nn.sigmoid(xg).astype(BF16), wg_ref[...], preferred_element_type=F32)
    kk = k * kk_ref[...]
    kk = kk * lax.rsqrt(jnp.maximum(_seg_sum(kk * kk, RWKV_HEAD_DIM), 1e-24))
    k_h = k * (1.0 + (a - 1.0) * ka_ref[...])
    r_o[...] = r
    d_o[...] = decay
    k_o[...] = k_h
    v_o[...] = v
    kk_o[...] = kk
    b_o[...] = kk * a
    g_o[...] = g
    bonus_o[...] = _seg_sum(r * k_h * rk_ref[...], RWKV_HEAD_DIM) * v


def _rwkv_pre(p, prev, rw, *, tm):
    m = p.shape[0]
    row = lambda w: pl.BlockSpec((1, w), lambda i: (0, 0))
    mat = lambda a, b: pl.BlockSpec((a, b), lambda i: (0, 0))
    out = pl.BlockSpec((tm, RWKV_W), lambda i: (i, 0))
    return pl.pallas_call(
        _rwkv_pre_kernel,
        grid=(m // tm,),
        in_specs=[
            pl.BlockSpec((tm, RWKV_COLS), lambda i: (i, 0)),
            pl.BlockSpec((tm, RWKV_COLS), lambda i: (i, 0)),
            row(RWKV_COLS), row(RWKV_W), mat(LANES, RWKV_W), row(RWKV_W), mat(LANES, RWKV_W),
            mat(GATE_LORA, RWKV_W), row(RWKV_W), row(RWKV_W), row(RWKV_W),
        ],
        out_specs=[out] * 8,
        out_shape=[jax.ShapeDtypeStruct((m, RWKV_W), F32)] * 8,
        compiler_params=_cparams(("parallel",)),
        name="rwkv_pre",
    )(p, prev, *rw)


RW_I = RWKV_HEAD_DIM // 2


def _rwkv_scan_kernel(r_ref, d_ref, k_ref, kk_ref, b_ref, v_ref, s0_ref, o_ref, sout_ref, s_ref, *, tc):
    c = pl.program_id(1)

    @pl.when(c == 0)
    def _():
        s_ref[...] = s0_ref[0]

    def step(t, carry):
        u = jnp.zeros((RW_I, LANES), F32)
        for j in range(RWKV_HEAD_DIM):
            u = u + s_ref[j] * kk_ref[0, t, pl.ds(j, 1), :]
        u = -u
        vt = v_ref[0, t]
        o = jnp.zeros((RW_I, LANES), F32)
        for j in range(RWKV_HEAD_DIM):
            h = (s_ref[j] * d_ref[0, t, pl.ds(j, 1), :] + u * b_ref[0, t, pl.ds(j, 1), :]
                 + vt * k_ref[0, t, pl.ds(j, 1), :])
            s_ref[j] = h
            o = o + h * r_ref[0, t, pl.ds(j, 1), :]
        o_ref[0, t] = o
        return carry

    lax.fori_loop(0, tc, step, 0)
    sout_ref[0] = s_ref[...]


def _rwkv_scan(r, d, k, kk, b, v, s0, *, tc):
    pg, t = r.shape[:2]
    vec = pl.BlockSpec((1, tc, RWKV_HEAD_DIM, LANES), lambda g, c: (g, c, 0, 0))
    val = pl.BlockSpec((1, tc, RW_I, LANES), lambda g, c: (g, c, 0, 0))
    st = pl.BlockSpec((1, RWKV_HEAD_DIM, RW_I, LANES), lambda g, c: (g, 0, 0, 0))
    return pl.pallas_call(
        functools.partial(_rwkv_scan_kernel, tc=tc),
        grid=(pg, t // tc),
        in_specs=[vec, vec, vec, vec, vec, val, st],
        out_specs=[val, st],
        out_shape=[jax.ShapeDtypeStruct((pg, t, RW_I, LANES), F32),
                   jax.ShapeDtypeStruct((pg, RWKV_HEAD_DIM, RW_I, LANES), F32)],
        scratch_shapes=[pltpu.VMEM((RWKV_HEAD_DIM, RW_I, LANES), F32)],
        compiler_params=_cparams(("parallel", "arbitrary")),
        name="rwkv_scan",
    )(r, d, k, kk, b, v, s0)


def _rwkv_post_kernel(o_ref, bonus_ref, g_ref, lg_ref, lb_ref, out_ref):
    o = o_ref[...]
    inv = 1.0 / RWKV_HEAD_DIM
    mean = _seg_sum(o, RWKV_HEAD_DIM) * inv
    cen = o - mean
    var = _seg_sum(cen * cen, RWKV_HEAD_DIM) * inv
    y = cen * lax.rsqrt(var + LNX_EPS) * lg_ref[...] + lb_ref[...]
    out_ref[...] = (y + bonus_ref[...]) * g_ref[...]


def _rwkv_post(o, bonus, g, lnx_g, lnx_b, *, tm):
    m = o.shape[0]
    blk = pl.BlockSpec((tm, RWKV_W), lambda i: (i, 0))
    row = pl.BlockSpec((1, RWKV_W), lambda i: (0, 0))
    return pl.pallas_call(
        _rwkv_post_kernel,
        grid=(m // tm,),
        in_specs=[blk, blk, blk, row, row],
        out_specs=blk,
        out_shape=jax.ShapeDtypeStruct((m, RWKV_W), F32),
        compiler_params=_cparams(("parallel",)),
        name="rwkv_post",
    )(o, bonus, g, lnx_g, lnx_b)


def _pairs_layout(x, n, t):
    pairs = n * RWKV_HEADS
    pg = pairs // 64
    y = x.reshape(n, t, RWKV_HEADS, RWKV_HEAD_DIM).transpose(1, 3, 0, 2).reshape(t, RWKV_HEAD_DIM, pg, 64)
    y = y.transpose(2, 0, 1, 3)
    return jnp.concatenate([y, y], axis=-1)


def _pairs_layout_v(x, n, t):
    pairs = n * RWKV_HEADS
    pg = pairs // 64
    y = x.reshape(n, t, RWKV_HEADS, 2, RW_I).transpose(1, 4, 3, 0, 2).reshape(t, RW_I, 2, pg, 64)
    return y.transpose(3, 0, 1, 2, 4).reshape(pg, t, RW_I, LANES)


def _pairs_unlayout_v(y, n, t):
    pg = y.shape[0]
    z = y.reshape(pg, t, RW_I, 2, 64).transpose(1, 2, 3, 0, 4).reshape(t, RW_I, 2, n, RWKV_HEADS)
    return z.transpose(3, 0, 4, 2, 1).reshape(n * t, RWKV_W)


def _state_layout(s0):
    n = s0.shape[0]
    pg = n * RWKV_HEADS // 64
    y = s0.reshape(pg, 64, 2, RW_I, RWKV_HEAD_DIM)
    return y.transpose(0, 4, 3, 2, 1).reshape(pg, RWKV_HEAD_DIM, RW_I, LANES)


def _state_unlayout(y, n):
    pg = y.shape[0]
    z = y.reshape(pg, RWKV_HEAD_DIM, RW_I, 2, 64).transpose(0, 4, 3, 2, 1)
    return z.reshape(n, RWKV_HEADS, RWKV_HEAD_DIM, RWKV_HEAD_DIM)


def _outproj_kernel(x_ref, a_ref, b_ref, wa_ref, wb_ref, o_ref):
    y = jnp.dot(a_ref[...].astype(BF16), wa_ref[...], preferred_element_type=F32)
    y = y + jnp.dot(b_ref[...].astype(BF16), wb_ref[...], preferred_element_type=F32)
    o_ref[...] = x_ref[...] + y


def _outproj(x, a, b, wa, wb, *, tm):
    m = x.shape[0]
    return pl.pallas_call(
        _outproj_kernel,
        grid=(m // tm,),
        in_specs=[
            pl.BlockSpec((tm, D_MODEL), lambda i: (i, 0)),
            pl.BlockSpec((tm, a.shape[1]), lambda i: (i, 0)),
            pl.BlockSpec((tm, b.shape[1]), lambda i: (i, 0)),
            pl.BlockSpec(wa.shape, lambda i: (0, 0)),
            pl.BlockSpec(wb.shape, lambda i: (0, 0)),
        ],
        out_specs=pl.BlockSpec((tm, D_MODEL), lambda i: (i, 0)),
        out_shape=jax.ShapeDtypeStruct((m, D_MODEL), F32),
        compiler_params=_cparams(("parallel",)),
        name="outproj",
    )(x, a, b, wa, wb)


def _memkv_kernel(x_ref, g_ref, w_ref, kg_ref, o_ref):
    xn = _rms(x_ref[...], g_ref[...]).astype(BF16)
    kv = jnp.dot(xn, w_ref[...], preferred_element_type=F32)
    xw = X_HEADS * X_HEAD_DIM
    o_ref[:, 0:xw] = _seg_rms(kv[:, 0:xw], kg_ref[...], X_HEAD_DIM)
    o_ref[:, xw:2 * xw] = kv[:, xw:2 * xw]


def _memkv(mem, g, w, kg, *, tm):
    m = mem.shape[0]
    xw = X_HEADS * X_HEAD_DIM
    return pl.pallas_call(
        _memkv_kernel,
        grid=(m // tm,),
        in_specs=[
            pl.BlockSpec((tm, D_MODEL), lambda i: (i, 0)),
            pl.BlockSpec((1, D_MODEL), lambda i: (0, 0)),
            pl.BlockSpec((D_MODEL, 2 * xw), lambda i: (0, 0)),
            pl.BlockSpec((1, xw), lambda i: (0, 0)),
        ],
        out_specs=pl.BlockSpec((tm, 2 * xw), lambda i: (i, 0)),
        out_shape=jax.ShapeDtypeStruct((m, 2 * xw), F32),
        compiler_params=_cparams(("parallel",)),
        name="memkv",
    )(mem, g, w, kg)


def _xattn_kernel(x_ref, mkv_ref, g_ref, wq_ref, qg_ref, wo_ref, o_ref):
    x = x_ref[0]
    xn = _rms(x, g_ref[...]).astype(BF16)
    q = jnp.dot(xn, wq_ref[...], preferred_element_type=F32)
    q = _seg_rms(q, qg_ref[...], X_HEAD_DIM) * (X_HEAD_DIM ** -0.5)
    xw = X_HEADS * X_HEAD_DIM
    outs = []
    for h in range(X_HEADS):
        lo = h * X_HEAD_DIM
        k = mkv_ref[0, :, lo:lo + X_HEAD_DIM].astype(BF16)
        v = mkv_ref[0, :, xw + lo:xw + lo + X_HEAD_DIM].astype(BF16)
        s = lax.dot_general(q[:, lo:lo + X_HEAD_DIM].astype(BF16), k, (((1,), (1,)), ((), ())),
                            preferred_element_type=F32)
        e = jnp.exp(s - jnp.max(s, axis=-1, keepdims=True))
        p = e / jnp.sum(e, axis=-1, keepdims=True)
        outs.append(jnp.dot(p.astype(BF16), v, preferred_element_type=F32))
    o = jnp.concatenate(outs, axis=-1).astype(BF16)
    o_ref[0] = x + jnp.dot(o, wo_ref[...], preferred_element_type=F32)


def _xattn(x, mkv, g, wq, qg, wo, *, tm):
    n, t, _ = x.shape
    xw = X_HEADS * X_HEAD_DIM
    n_mem = mkv.shape[1]
    return pl.pallas_call(
        _xattn_kernel,
        grid=(n, t // tm),
        in_specs=[
            pl.BlockSpec((1, tm, D_MODEL), lambda b, i: (b, i, 0)),
            pl.BlockSpec((1, n_mem, 2 * xw), lambda b, i: (b, 0, 0)),
            pl.BlockSpec((1, D_MODEL), lambda b, i: (0, 0)),
            pl.BlockSpec((D_MODEL, xw), lambda b, i: (0, 0)),
            pl.BlockSpec((1, xw), lambda b, i: (0, 0)),
            pl.BlockSpec((xw, D_MODEL), lambda b, i: (0, 0)),
        ],
        out_specs=pl.BlockSpec((1, tm, D_MODEL), lambda b, i: (b, i, 0)),
        out_shape=jax.ShapeDtypeStruct((n, t, D_MODEL), F32),
        compiler_params=_cparams(("parallel", "parallel")),
        name="xattn",
    )(x, mkv, g, wq, qg, wo)


def _row(v):
    return v.reshape(1, -1).astype(F32)


def _block_diag2(w):
    z = jnp.zeros_like(w)
    return jnp.concatenate([jnp.concatenate([w, z], axis=-1), jnp.concatenate([z, w], axis=-1)], axis=-2)


def _prep_weights(norm_ffn1_g, w_ffn1_gu, w_ffn1_down, norm_mix_g, w_in, w_out, q_norm_g, kc_norm_g, ks_norm_g,
                  kw_norm_g, gate_b, cmp_pe_k, cmp_pe_v, w_cmp_k1, w_cmp_k2, w_cmp_v1, w_cmp_v2, rwkv_mu, rwkv_w0,
                  w_decay2, rwkv_a0, w_iclr2, w_gate2, rwkv_k_k, rwkv_k_a, rwkv_r_k, lnx_g, lnx_b, norm_x_g,
                  norm_mem_g, w_xq, w_xkv, xq_norm_g, xk_norm_g, w_xo, norm_ffn2_g, w_ffn2_gu, w_ffn2_down,
                  norm_out_g):
    G = NSA_KV_HEADS
    W = {}
    W["ffn1"] = (_row(norm_ffn1_g), w_ffn1_gu.astype(BF16), w_ffn1_down.astype(BF16))
    W["ffn2"] = (_row(norm_ffn2_g), w_ffn2_gu.astype(BF16), w_ffn2_down.astype(BF16))
    W["norm_out_g"] = _row(norm_out_g)
    w_pad = jnp.concatenate([w_in[:, :NSA_COLS], jnp.zeros((D_MODEL, NSA_PAD - NSA_COLS), F32), w_in[:, NSA_COLS:]],
                            axis=1).astype(BF16)
    head_g = jnp.zeros((NSA_PAD,), F32)
    head_g = head_g.at[0:NSA_Q_W].set(jnp.tile(q_norm_g, NSA_HEADS))
    c = NSA_Q_W + 2 * NSA_KV_W
    head_g = head_g.at[c:c + NSA_KV_W].set(jnp.tile(ks_norm_g, G))
    c = NSA_Q_W + 4 * NSA_KV_W
    head_g = head_g.at[c:c + NSA_KV_W].set(jnp.tile(kw_norm_g, G))
    gb = jnp.zeros((LANES,), F32).at[0:3 * NSA_HEADS].set(gate_b)
    W["proj"] = (_row(norm_mix_g), w_pad, _row(head_g), _row(gb))
    pe = jnp.concatenate([cmp_pe_k, cmp_pe_k, cmp_pe_v, cmp_pe_v], axis=-1)[:, None, :]
    W["cmp"] = (pe, _block_diag2(w_cmp_k1).astype(BF16), _block_diag2(w_cmp_v1).astype(BF16),
                _block_diag2(w_cmp_k2).astype(BF16), _block_diag2(w_cmp_v2).astype(BF16),
                _row(jnp.tile(kc_norm_g, G)))
    zl = jnp.zeros((DECAY_LORA, RWKV_W), F32)
    W["rwkv_pre"] = (_row(rwkv_mu), _row(rwkv_w0), jnp.concatenate([w_decay2, zl], axis=0).astype(BF16),
                     _row(rwkv_a0), jnp.concatenate([zl, w_iclr2], axis=0).astype(BF16), w_gate2.astype(BF16),
                     _row(rwkv_k_k), _row(rwkv_k_a), _row(rwkv_r_k))
    W["rwkv_post"] = (_row(lnx_g), _row(lnx_b))
    W["w_out"] = (w_out[:NSA_Q_W].astype(BF16), w_out[NSA_Q_W:].astype(BF16))
    W["xattn"] = (_row(norm_x_g), w_xq.astype(BF16), _row(jnp.tile(xq_norm_g, X_HEADS)), w_xo.astype(BF16))
    W["memkv"] = (_row(norm_mem_g), w_xkv.astype(BF16), _row(jnp.tile(xk_norm_g, X_HEADS)))
    return W


def _tile_rows(m, pref):
    return pref if m % pref == 0 else m


def _rwkv_group(pr, shift0, s0, W, n, t):
    m = n * t
    pr3 = pr.reshape(n, t, RWKV_COLS)
    prev = jnp.concatenate([shift0[:, None, :], pr3[:, :-1]], axis=1).reshape(m, RWKV_COLS)
    r, d, k, v, kk, b, g, bonus = _rwkv_pre(pr, prev, W["rwkv_pre"], tm=_tile_rows(m, 256))
    lay = lambda a: _pairs_layout(a, n, t)
    o, s_new = _rwkv_scan(lay(r), lay(d), lay(k), lay(kk), lay(b), _pairs_layout_v(v, n, t), _state_layout(s0),
                          tc=min(t, 32))
    o = _pairs_unlayout_v(o, n, t)
    o = _rwkv_post(o, bonus, g, *W["rwkv_post"], tm=_tile_rows(m, 512))
    return o, _state_unlayout(s_new, n)


def _nsa_prompt_group(pn, W, n, t):
    G = NSA_KV_HEADS
    n_blk = t // CMP_BLOCK
    kc, vc = _compress_rows(pn, NSA_Q_W // LANES, W["cmp"], nblk=n_blk)
    pn3 = pn.reshape(n, t, NSA_PAD)
    col = lambda i: pn3[:, :, NSA_Q_W + i * NSA_KV_W:NSA_Q_W + (i + 1) * NSA_KV_W]
    pos = jnp.arange(t, dtype=jnp.int32)
    ksp = _key_features(col(2), pos)
    kwp = _key_features(col(4), pos)
    end = (jnp.arange(n_blk, dtype=jnp.int32) + 1) * CMP_BLOCK - 1
    kcp = _key_features(kc.reshape(n, n_blk, NSA_KV_W), end)
    vct = vc.reshape(n, n_blk, G, HEAD_DIM).transpose(0, 2, 3, 1).astype(BF16)
    gates_t = pn3[:, :, NSA_Q_W + 6 * NSA_KV_W:NSA_COLS].transpose(0, 2, 1)
    o_t = _nsa_prompt(_query_tiles_t(_query_features(pn3[:, :, :NSA_Q_W])), kcp, vct, ksp, _values_t(col(3), KC),
                      kwp, _values_t(col(5), KC), gates_t)
    return o_t.transpose(0, 2, 1).reshape(n * t, NSA_Q_W)


def _pad_axis(a, axis, size):
    pad = [(0, 0)] * a.ndim
    pad[axis] = (0, size - a.shape[axis])
    return jnp.pad(a, pad)


def _nsa_sample_group(pn, W, n, t_new, cache_kv, page_table, cache_win):
    G = NSA_KV_HEADS
    assert t_new <= CMP_BLOCK and t_new <= TQ
    past = page_table.shape[1] * PAGE_SIZE
    nb_past = past // CMP_BLOCK
    w_buf = cache_win.shape[1]
    cache2 = cache_kv.reshape(cache_kv.shape[0], PAGE_SIZE, 4 * G * HEAD_DIM)
    kc_past, vc_past = _compress_paged(page_table, cache2, W["cmp"])
    pn3 = pn.reshape(n, t_new, NSA_PAD)
    col = lambda i: pn3[:, :, NSA_Q_W + i * NSA_KV_W:NSA_Q_W + (i + 1) * NSA_KV_W]
    new_rows = _pad_axis(pn3[:, :, NSA_Q_W:NSA_Q_W + 2 * NSA_KV_W], 1, CMP_BLOCK).reshape(n * CMP_BLOCK, 2 * NSA_KV_W)
    kc_new, vc_new = _compress_rows(new_rows, 0, W["cmp"], nblk=n)
    nb_pad = -(-(nb_past + 1) // 16) * 16
    kc = _pad_axis(jnp.concatenate([kc_past, kc_new[:, None]], axis=1), 1, nb_pad)
    vc = _pad_axis(jnp.concatenate([vc_past, vc_new[:, None]], axis=1), 1, nb_pad)
    end_rel = (jnp.arange(nb_pad, dtype=jnp.int32) + 1) * CMP_BLOCK - 1 - past
    kcp = _key_features(kc, end_rel)
    vct = vc.reshape(n, nb_pad, G, HEAD_DIM).transpose(0, 2, 3, 1).astype(BF16)
    n_win = -(-(w_buf + t_new) // KC) * KC
    cw = cache_win.reshape(n, w_buf, 2, NSA_KV_W)
    kw_all = _pad_axis(jnp.concatenate([cw[:, :, 0], col(4)], axis=1), 1, n_win)
    vw_all = _pad_axis(jnp.concatenate([cw[:, :, 1], col(5)], axis=1), 1, n_win)
    kwp = _key_features(kw_all, jnp.arange(n_win, dtype=jnp.int32) - w_buf)
    vwt = vw_all.reshape(n, n_win, G, HEAD_DIM).transpose(0, 2, 3, 1).astype(BF16)
    heads = lambda a: a.reshape(n, t_new, G, HEAD_DIM).transpose(0, 2, 1, 3)
    knew = _pad_axis(heads(col(2)), 2, KC).astype(BF16)
    vnewt = _pad_axis(heads(col(3)).transpose(0, 1, 3, 2), 3, KC).astype(BF16)
    n_real = NSA_REP * t_new
    qt = _query_features(pn3[:, :, :NSA_Q_W]).reshape(n, t_new, G, NSA_REP, LANES)
    qt = _pad_axis(qt.transpose(0, 2, 4, 3, 1).reshape(n, G, LANES, n_real), 3, LANES)
    gates = pn3[:, :, NSA_Q_W + 6 * NSA_KV_W:NSA_COLS].reshape(n, t_new, G, NSA_REP, 3)
    gates_t = _pad_axis(gates.transpose(0, 2, 4, 3, 1).reshape(n, G, 3, n_real), 3, LANES)
    o_t = _nsa_sample(page_table, qt, kcp, vct, kwp, vwt, knew, vnewt, gates_t, cache2, past=past, n_new=t_new)
    o = o_t[:, :, :, :n_real].reshape(n, G, HEAD_DIM, NSA_REP, t_new)
    return o.transpose(0, 4, 1, 3, 2).reshape(n * t_new, NSA_Q_W)


def _layer(x, mkv, shift0, s0, W, nsa_fn):
    n, t, _ = x.shape
    m = n * t
    G = NSA_KV_HEADS
    x2 = x.reshape(m, D_MODEL)
    tm = _tile_rows(m, 512)
    x2 = _ffn(x2, *W["ffn1"], W["norm_out_g"], final_norm=False, tm=tm, tf=D_FF // 2)
    pn, pr = _proj(x2, *W["proj"], tm=_tile_rows(m, 256))
    o_nsa = nsa_fn(pn)
    o_rwkv, s_new = _rwkv_group(pr, shift0, s0, W, n, t)
    x2 = _outproj(x2, o_nsa, o_rwkv, *W["w_out"], tm=tm)
    x3 = _xattn(x2.reshape(n, t, D_MODEL), mkv, *W["xattn"], tm=_tile_rows(t, 512))
    y = _ffn(x3.reshape(m, D_MODEL), *W["ffn2"], W["norm_out_g"], final_norm=True, tm=tm, tf=D_FF // 2)
    pn3 = pn.reshape(n, t, NSA_PAD)
    kv_rows = pn3[:, :, NSA_Q_W:NSA_Q_W + 4 * NSA_KV_W].reshape(n, t, 4, G, HEAD_DIM)
    win_new = pn3[:, :, NSA_Q_W + 4 * NSA_KV_W:NSA_Q_W + 6 * NSA_KV_W].reshape(n, t, 2, G, HEAD_DIM)
    shift_new = pr.reshape(n, t, RWKV_COLS)[:, -1]
    return y.reshape(n, t, D_MODEL), kv_rows, win_new, s_new, shift_new


def kernel(x_prompt, x_sample, cache_nsa_kv, cache_nsa_win, state_rwkv_s, state_rwkv_shift, cache_mem_kv, page_table, mem_prompt, norm_ffn1_g, w_ffn1_gu, w_ffn1_down, norm_mix_g, w_in, w_out, q_norm_g, kc_norm_g, ks_norm_g, kw_norm_g, gate_b, cmp_pe_k, cmp_pe_v, w_cmp_k1, w_cmp_k2, w_cmp_v1, w_cmp_v2, rwkv_mu, rwkv_w0, w_decay2, rwkv_a0, w_iclr2, w_gate2, rwkv_k_k, rwkv_k_a, rwkv_r_k, lnx_g, lnx_b, norm_x_g, norm_mem_g, w_xq, w_xkv, xq_norm_g, xk_norm_g, w_xo, norm_ffn2_g, w_ffn2_gu, w_ffn2_down, norm_out_g):
    layer_weights = (norm_ffn1_g, w_ffn1_gu, w_ffn1_down, norm_mix_g, w_in, w_out, q_norm_g, kc_norm_g, ks_norm_g,
                     kw_norm_g, gate_b, cmp_pe_k, cmp_pe_v, w_cmp_k1, w_cmp_k2, w_cmp_v1, w_cmp_v2, rwkv_mu, rwkv_w0,
                     w_decay2, rwkv_a0, w_iclr2, w_gate2, rwkv_k_k, rwkv_k_a, rwkv_r_k, lnx_g, lnx_b, norm_x_g,
                     norm_mem_g, w_xq, w_xkv, xq_norm_g, xk_norm_g, w_xo, norm_ffn2_g, w_ffn2_gu, w_ffn2_down,
                     norm_out_g)
    assert w_in.shape[0] == 1, "single-layer trunk"
    W = _prep_weights(*(w[0] for w in layer_weights))
    n_p, t_p, _ = x_prompt.shape
    n_s, t_s, _ = x_sample.shape
    n_mem = mem_prompt.shape[1]
    xw = X_HEADS * X_HEAD_DIM

    mkv_p = _memkv(mem_prompt.reshape(n_p * n_mem, D_MODEL), *W["memkv"], tm=_tile_rows(n_p * n_mem, 512))
    mkv_p = mkv_p.reshape(n_p, n_mem, 2 * xw)
    y_p, kv_p, win_p, rs_p, sh_p = _layer(
        x_prompt, mkv_p, jnp.zeros((n_p, RWKV_COLS), F32),
        jnp.zeros((n_p, RWKV_HEADS, RWKV_HEAD_DIM, RWKV_HEAD_DIM), F32), W,
        lambda pn: _nsa_prompt_group(pn, W, n_p, t_p))
    win_p = win_p[:, t_p - min(WINDOW, t_p):]

    mkv_s = cache_mem_kv[0].reshape(n_s, n_mem, 2 * xw)
    y_s, kv_s, win_new, rs_s, sh_s = _layer(
        x_sample, mkv_s, state_rwkv_shift[0], state_rwkv_s[0], W,
        lambda pn: _nsa_sample_group(pn, W, n_s, t_s, cache_nsa_kv[0], page_table, cache_nsa_win[0]))
    win_s = jnp.concatenate([cache_nsa_win[0], win_new], axis=1)[:, t_s:]

    mkv_out = mkv_p.reshape(1, n_p, n_mem, 2, X_HEADS, X_HEAD_DIM)
    return (y_p, y_s, kv_p[None], kv_s[None], win_p[None], win_s[None], rs_p[None], rs_s[None], sh_p[None],
            sh_s[None], mkv_out)
```

```python
import functools
import math

import jax
import jax.numpy as jnp
from jax import lax
from jax.experimental import pallas as pl
from jax.experimental.pallas import tpu as pltpu

F32 = jnp.float32
BF16 = jnp.bfloat16

D_MODEL = 1024
NSA_HEADS = 8
NSA_KV_HEADS = 2
NSA_REP = NSA_HEADS // NSA_KV_HEADS
HEAD_DIM = 64
CMP_BLOCK = 64
N_SEL = 16
WINDOW = 512
CMP_HIDDEN = 128
PAGE_SIZE = 128
RWKV_HEADS = 8
RWKV_HEAD_DIM = 64
RWKV_W = RWKV_HEADS * RWKV_HEAD_DIM
DECAY_LORA = 64
ICLR_LORA = 64
GATE_LORA = 128
NSA_Q_W = NSA_HEADS * HEAD_DIM
NSA_KV_W = NSA_KV_HEADS * HEAD_DIM
NSA_COLS = NSA_Q_W + 6 * NSA_KV_W + 3 * NSA_HEADS
RWKV_COLS = 3 * RWKV_W + DECAY_LORA + ICLR_LORA + GATE_LORA
X_HEADS = 4
X_HEAD_DIM = 128
D_FF = 2816
RMS_EPS = 1e-6
LNX_EPS = 64e-5
FORCED_SCORE = 1e9

LANES = 128
SUBLANES = 8
NSA_PAD = 1408
P_COLS = NSA_PAD + RWKV_COLS
TQ = 128
KC = 128
VMEM_LIMIT = 56 * 1024 * 1024

ALIBI = tuple(2.0 ** (-8.0 * (h + 1.0) / NSA_HEADS) for h in range(NSA_HEADS))


def _cparams(sem):
    return pltpu.CompilerParams(dimension_semantics=sem, vmem_limit_bytes=VMEM_LIMIT)


def _rms(x, g):
    return x * lax.rsqrt(jnp.mean(x * x, axis=-1, keepdims=True) + RMS_EPS) * g


def _seg_ones(width, seg):
    r = lax.broadcasted_iota(jnp.int32, (width, width), 0) // seg
    c = lax.broadcasted_iota(jnp.int32, (width, width), 1) // seg
    return (r == c).astype(F32)


def _seg_sum(x, seg):
    ones = _seg_ones(LANES, seg)
    parts = [jnp.dot(x[:, c:c + LANES], ones, precision=lax.Precision.HIGHEST, preferred_element_type=F32)
             for c in range(0, x.shape[1], LANES)]
    return parts[0] if len(parts) == 1 else jnp.concatenate(parts, axis=1)


def _seg_rms(x, g, seg):
    if seg == LANES:
        parts = [_rms(x[:, c:c + LANES], g[:, c:c + LANES]) for c in range(0, x.shape[1], LANES)]
        return parts[0] if len(parts) == 1 else jnp.concatenate(parts, axis=1)
    return x * lax.rsqrt(_seg_sum(x * x, seg) * (1.0 / seg) + RMS_EPS) * g


def _ffn_kernel(x_ref, g_ref, wg_ref, wu_ref, wd_ref, go_ref, o_ref, xn_ref, acc_ref, *, final_norm):
    f = pl.program_id(1)

    @pl.when(f == 0)
    def _():
        xn_ref[...] = _rms(x_ref[...], g_ref[...]).astype(BF16)
        acc_ref[...] = jnp.zeros_like(acc_ref)

    xn = xn_ref[...]
    gate = jnp.dot(xn, wg_ref[...], preferred_element_type=F32)
    up = jnp.dot(xn, wu_ref[...], preferred_element_type=F32)
    h = (gate * jax.nn.sigmoid(gate) * up).astype(BF16)
    acc_ref[...] += jnp.dot(h, wd_ref[...], preferred_element_type=F32)

    @pl.when(f == pl.num_programs(1) - 1)
    def _():
        y = x_ref[...] + 0.5 * acc_ref[...]
        if final_norm:
            y = _rms(y, go_ref[...])
        o_ref[...] = y


def _ffn(x, g, w_gu, w_down, g_out, *, final_norm, tm, tf):
    m = x.shape[0]
    nf = D_FF // tf
    return pl.pallas_call(
        functools.partial(_ffn_kernel, final_norm=final_norm),
        grid=(m // tm, nf),
        in_specs=[
            pl.BlockSpec((tm, D_MODEL), lambda i, f: (i, 0)),
            pl.BlockSpec((1, D_MODEL), lambda i, f: (0, 0)),
            pl.BlockSpec((D_MODEL, tf), lambda i, f: (0, f)),
            pl.BlockSpec((D_MODEL, tf), lambda i, f: (0, nf + f)),
            pl.BlockSpec((tf, D_MODEL), lambda i, f: (f, 0)),
            pl.BlockSpec((1, D_MODEL), lambda i, f: (0, 0)),
        ],
        out_specs=pl.BlockSpec((tm, D_MODEL), lambda i, f: (i, 0)),
        out_shape=jax.ShapeDtypeStruct((m, D_MODEL), F32),
        scratch_shapes=[pltpu.VMEM((tm, D_MODEL), BF16), pltpu.VMEM((tm, D_MODEL), F32)],
        compiler_params=_cparams(("parallel", "arbitrary")),
        name="ffn",
    )(x, g, w_gu, w_gu, w_down, g_out)


def _proj_kernel(x_ref, g_ref, w_ref, hg_ref, gb_ref, on_ref, or_ref):
    xn = _rms(x_ref[...], g_ref[...]).astype(BF16)
    p = jnp.dot(xn, w_ref[...], preferred_element_type=F32)
    or_ref[...] = p[:, NSA_PAD:]
    on_ref[...] = p[:, 0:NSA_PAD]
    hg = hg_ref[...]
    on_ref[:, 0:NSA_Q_W] = _seg_rms(p[:, 0:NSA_Q_W], hg[:, 0:NSA_Q_W], HEAD_DIM)
    for c in (NSA_Q_W + 2 * NSA_KV_W, NSA_Q_W + 4 * NSA_KV_W):
        on_ref[:, c:c + NSA_KV_W] = _seg_rms(p[:, c:c + NSA_KV_W], hg[:, c:c + NSA_KV_W], HEAD_DIM)
    c = NSA_Q_W + 6 * NSA_KV_W
    on_ref[:, c:c + LANES] = jax.nn.sigmoid(p[:, c:c + LANES] + gb_ref[...])


def _proj(x, g, w_pad, head_g, gate_b, *, tm):
    m = x.shape[0]
    return pl.pallas_call(
        _proj_kernel,
        grid=(m // tm,),
        in_specs=[
            pl.BlockSpec((tm, D_MODEL), lambda i: (i, 0)),
            pl.BlockSpec((1, D_MODEL), lambda i: (0, 0)),
            pl.BlockSpec((D_MODEL, P_COLS), lambda i: (0, 0)),
            pl.BlockSpec((1, NSA_PAD), lambda i: (0, 0)),
            pl.BlockSpec((1, LANES), lambda i: (0, 0)),
        ],
        out_specs=[pl.BlockSpec((tm, NSA_PAD), lambda i: (i, 0)), pl.BlockSpec((tm, RWKV_COLS), lambda i: (i, 0))],
        out_shape=[jax.ShapeDtypeStruct((m, NSA_PAD), F32), jax.ShapeDtypeStruct((m, RWKV_COLS), F32)],
        compiler_params=_cparams(("parallel",)),
        name="proj",
    )(x, g, w_pad, head_g, gate_b)


def _compress_core(xk_ref, xv_ref, nblk, pe_ref, w1k_ref, w1v_ref, w2k_ref, w2v_ref, kcg_ref):
    def body(j, carry):
        acc_k, acc_v = carry
        pe = pe_ref[j]
        xk = xk_ref[pl.ds(j, nblk, stride=CMP_BLOCK), :] + pe[:, 0:LANES]
        xv = xv_ref[pl.ds(j, nblk, stride=CMP_BLOCK), :] + pe[:, LANES:2 * LANES]
        acc_k = acc_k + jnp.dot(xk.astype(BF16), w1k_ref[j], preferred_element_type=F32)
        acc_v = acc_v + jnp.dot(xv.astype(BF16), w1v_ref[j], preferred_element_type=F32)
        return acc_k, acc_v

    zero = jnp.zeros((nblk, 2 * CMP_HIDDEN), F32)
    acc_k, acc_v = lax.fori_loop(0, CMP_BLOCK, body, (zero, zero))
    hk = jax.nn.gelu(acc_k).astype(BF16)
    hv = jax.nn.gelu(acc_v).astype(BF16)
    kc = jnp.dot(hk, w2k_ref[...], preferred_element_type=F32)
    vc = jnp.dot(hv, w2v_ref[...], preferred_element_type=F32)
    return _seg_rms(kc, kcg_ref[...], HEAD_DIM), vc


def _compress_kernel(xk_ref, xv_ref, pe_ref, w1k_ref, w1v_ref, w2k_ref, w2v_ref, kcg_ref, kc_ref, vc_ref, *, nblk):
    kc, vc = _compress_core(xk_ref, xv_ref, nblk, pe_ref, w1k_ref, w1v_ref, w2k_ref, w2v_ref, kcg_ref)
    kc_ref[...] = kc
    vc_ref[...] = vc


def _cmp_weight_specs(imap):
    return [
        pl.BlockSpec((CMP_BLOCK, 1, 2 * LANES), imap(3)),
        pl.BlockSpec((CMP_BLOCK, LANES, 2 * CMP_HIDDEN), imap(3)),
        pl.BlockSpec((CMP_BLOCK, LANES, 2 * CMP_HIDDEN), imap(3)),
        pl.BlockSpec((2 * CMP_HIDDEN, LANES), imap(2)),
        pl.BlockSpec((2 * CMP_HIDDEN, LANES), imap(2)),
        pl.BlockSpec((1, LANES), imap(2)),
    ]


def _compress_rows(rows, col_block, cw, *, nblk):
    m = rows.shape[0]
    steps = m // (nblk * CMP_BLOCK)
    imap = lambda nd: (lambda i: (0,) * nd)
    return pl.pallas_call(
        functools.partial(_compress_kernel, nblk=nblk),
        grid=(steps,),
        in_specs=[pl.BlockSpec((nblk * CMP_BLOCK, LANES), lambda i: (i, col_block)),
                  pl.BlockSpec((nblk * CMP_BLOCK, LANES), lambda i: (i, col_block + 1))] + _cmp_weight_specs(imap),
        out_specs=[pl.BlockSpec((nblk, LANES), lambda i: (i, 0))] * 2,
        out_shape=[jax.ShapeDtypeStruct((steps * nblk, LANES), F32)] * 2,
        compiler_params=_cparams(("parallel",)),
        name="compress",
    )(rows, rows, *cw)


CMP_ROWS = 2 * NSA_KV_HEADS * HEAD_DIM
CMP_FEATS = 8


def _compress_paged_kernel(pt_ref, cache_ref, pe_ref, w1_ref, w2_ref, kcg_ref, kc_ref, vc_ref, buf_ref, sem, *,
                           n_pages):
    n = pl.program_id(0)

    def page_copy(i):
        return pltpu.make_async_copy(cache_ref.at[pt_ref[n, i], pl.ds(0, CMP_ROWS), :], buf_ref.at[:, i, :], sem)

    def start(i, c):
        page_copy(i).start()
        return c

    def wait(i, c):
        page_copy(i).wait()
        return c

    lax.fori_loop(0, n_pages, start, 0)
    lax.fori_loop(0, n_pages, wait, 0)
    blocks_per_page = PAGE_SIZE // CMP_BLOCK
    for c, out_ref in enumerate((kc_ref, vc_ref)):
        for g in range(NSA_KV_HEADS):
            def body(it, acc):
                d0 = pl.multiple_of(it * CMP_FEATS, CMP_FEATS)
                x = jnp.concatenate(
                    [buf_ref[(c * NSA_KV_HEADS + g) * HEAD_DIM + d0 + u] + pe_ref[c * HEAD_DIM + d0 + u]
                     for u in range(CMP_FEATS)], axis=1)
                w = w1_ref[c, pl.ds(d0, CMP_FEATS)].reshape(CMP_FEATS * PAGE_SIZE, blocks_per_page * CMP_HIDDEN)
                return acc + jnp.dot(x.astype(BF16), w, preferred_element_type=F32)

            acc = lax.fori_loop(0, HEAD_DIM // CMP_FEATS, body,
                                jnp.zeros((n_pages, blocks_per_page * CMP_HIDDEN), F32))
            out = jnp.dot(jax.nn.gelu(acc).astype(BF16), w2_ref[c], preferred_element_type=F32)
            if c == 0:
                out = _seg_rms(out, kcg_ref[...], HEAD_DIM)
            out_ref[0, g] = out


def _compress_paged(page_table, cache_t, cw):
    nb, n_pages = page_table.shape
    width = (PAGE_SIZE // CMP_BLOCK) * HEAD_DIM
    const = lambda shape: pl.BlockSpec(shape, lambda n, pt: (0,) * len(shape))
    out = pl.BlockSpec((1, NSA_KV_HEADS, n_pages, width), lambda n, pt: (n, 0, 0, 0))
    return pl.pallas_call(
        functools.partial(_compress_paged_kernel, n_pages=n_pages),
        grid_spec=pltpu.PrefetchScalarGridSpec(
            num_scalar_prefetch=1,
            grid=(nb,),
            in_specs=[pl.BlockSpec(memory_space=pl.ANY)] + [const(w.shape) for w in cw],
            out_specs=[out, out],
            scratch_shapes=[pltpu.VMEM((CMP_ROWS, n_pages, PAGE_SIZE), F32), pltpu.SemaphoreType.DMA(())],
        ),
        out_shape=[jax.ShapeDtypeStruct((nb, NSA_KV_HEADS, n_pages, width), F32)] * 2,
        compiler_params=_cparams(("arbitrary",)),
        name="compress_paged",
    )(page_table, cache_t, *cw)


QW = NSA_REP * TQ
SEL_KC = 512
RANK_UNROLL = 8


def _wide_init(width=QW):
    return (jnp.full((1, width), -jnp.inf, F32), jnp.zeros((1, width), F32), jnp.zeros((HEAD_DIM, width), F32))


def _wide_step(state, s, vt=None, v_rows=None, group=0):
    m, l, acc = state
    m_new = jnp.maximum(m, jnp.max(s, axis=0, keepdims=True))
    m_safe = jnp.where(m_new == -jnp.inf, 0.0, m_new)
    alpha = jnp.exp(m - m_safe)
    p = jnp.exp(s - m_safe)
    l = alpha * l + jnp.sum(p, axis=0, keepdims=True)
    if vt is not None:
        pv = jnp.dot(vt, p.astype(BF16), preferred_element_type=F32)
    else:
        pv = lax.dot_general(v_rows, p.astype(BF16), (((0,), (0,)), ((), ())), preferred_element_type=F32)
        pv = pv[group * HEAD_DIM:(group + 1) * HEAD_DIM, :]
    return m_new, l, alpha * acc + pv


def _wide_out(state):
    _, l, acc = state
    return acc / jnp.maximum(l, 1e-30)


def _cmp_select_wide(qt, kcp, vct, t_row, n_blk, nb_pad, score_ref, selb_ref, head_sum, expand):
    w = qt.shape[1]
    b_col = lax.broadcasted_iota(jnp.int32, (nb_pad, w), 0)
    valid = (t_row >= (b_col + 1) * CMP_BLOCK - 1) & (b_col < n_blk)
    s = jnp.where(valid, jnp.dot(kcp, qt, preferred_element_type=F32), -jnp.inf)
    m = jnp.max(s, axis=0, keepdims=True)
    e = jnp.exp(s - jnp.where(m == -jnp.inf, 0.0, m))
    p = e / jnp.maximum(jnp.sum(e, axis=0, keepdims=True), 1e-30)
    o_c = jnp.dot(vct, p.astype(BF16), preferred_element_type=F32)
    imp = head_sum(p)
    ws = imp.shape[1]
    b1 = b_col[:, 0:ws]
    cur = t_row[:, 0:ws] // CMP_BLOCK
    forced = (b1 == 0) | (b1 == cur) | (b1 == cur - 1)
    score = jnp.where(forced, FORCED_SCORE, jnp.where(b1 <= cur, imp, -FORCED_SCORE))
    score = jnp.where(b1 < n_blk, score, -jnp.inf)
    score_ref[...] = score

    def rank_body(it, cnt):
        base = pl.multiple_of(it * RANK_UNROLL, RANK_UNROLL)
        for u in range(RANK_UNROLL):
            bp = base + u
            row = jnp.broadcast_to(score_ref[pl.ds(bp, 1), :], (nb_pad, ws))
            ahead = (row > score) | ((row == score) & (b1 > bp))
            cnt = cnt + jnp.where(ahead, 1.0, 0.0)
        return cnt

    cnt = lax.fori_loop(0, nb_pad // RANK_UNROLL, rank_body, jnp.zeros((nb_pad, ws), F32))
    selb_ref[...] = expand(jnp.where((cnt < N_SEL) & (b1 < n_blk), 0.0, -jnp.inf))
    return o_c


def _sum_lane_chunks(p):
    imp = p[:, 0:TQ]
    for r in range(1, NSA_REP):
        imp = imp + p[:, r * TQ:(r + 1) * TQ]
    return imp


def _nsa_prompt_kernel(q_ref, gate_ref, kcp_ref, vct_ref, ksp_ref, vs_ref, kwp_ref, vw_ref, o_ref,
                       score_ref, selb_ref, *, n_blk):
    i = pl.program_id(1)
    t0 = i * TQ
    t_row = t0 + (lax.broadcasted_iota(jnp.int32, (1, QW), 1) & (TQ - 1))
    blocks_per_step = SEL_KC // CMP_BLOCK
    wk = WINDOW + TQ
    ws = pl.multiple_of(jnp.maximum(t0 - WINDOW, 0), TQ)
    dist = t_row - (ws + lax.broadcasted_iota(jnp.int32, (wk, QW), 0))
    wbias = jnp.where((dist >= 0) & (dist < WINDOW), 0.0, -jnp.inf)
    n_steps = (t0 + TQ + SEL_KC - 1) // SEL_KC

    q_t = (q_ref[...] * (HEAD_DIM ** -0.5)).T
    gate_t = gate_ref[...].T
    f_row = lax.broadcasted_iota(jnp.int32, (HEAD_DIM, TQ), 0)
    outs = []
    for g in range(NSA_KV_HEADS):
        cols = []
        for r in range(NSA_REP):
            h = g * NSA_REP + r
            feat = jnp.where(f_row == 0, ALIBI[h] * CMP_BLOCK, jnp.where(f_row == 1, ALIBI[h], 0.0))
            cols.append(jnp.concatenate([q_t[h * HEAD_DIM:(h + 1) * HEAD_DIM, :], feat], axis=0))
        qt = jnp.concatenate(cols, axis=1).astype(BF16)
        o_c = _cmp_select_wide(qt, kcp_ref[0, g], vct_ref[0, g], t_row, n_blk, n_blk, score_ref, selb_ref,
                               _sum_lane_chunks, lambda a: jnp.concatenate([a] * NSA_REP, axis=1))

        def step_inputs(c):
            k0 = pl.multiple_of(c * SEL_KC, SEL_KC)
            s = jnp.dot(ksp_ref[0, g, pl.ds(k0, SEL_KC), :], qt, preferred_element_type=F32)
            s = jnp.concatenate(
                [s[b * CMP_BLOCK:(b + 1) * CMP_BLOCK, :] + selb_ref[pl.ds(c * blocks_per_step + b, 1), :]
                 for b in range(blocks_per_step)], axis=0)
            return s, vs_ref[pl.ds(k0, SEL_KC), :].astype(BF16)

        def sel_body(c, state):
            s, v = step_inputs(c)
            return _wide_step(state, s, v_rows=v, group=g)

        state = lax.fori_loop(0, n_steps - 1, sel_body, _wide_init())
        c_last = n_steps - 1
        s, v = step_inputs(c_last)
        pos = c_last * SEL_KC + lax.broadcasted_iota(jnp.int32, (SEL_KC, QW), 0)
        o_s = _wide_out(_wide_step(state, jnp.where(pos <= t_row, s, -jnp.inf), v_rows=v, group=g))

        s = jnp.dot(kwp_ref[0, g, pl.ds(ws, wk), :], qt, preferred_element_type=F32) + wbias
        o_w = _wide_out(_wide_step(_wide_init(), s, v_rows=vw_ref[pl.ds(ws, wk), :].astype(BF16), group=g))

        for r in range(NSA_REP):
            lanes = slice(r * TQ, (r + 1) * TQ)
            row = (g * NSA_REP + r) * 3
            outs.append(gate_t[row:row + 1, :] * o_c[:, lanes] + gate_t[row + 1:row + 2, :] * o_s[:, lanes]
                        + gate_t[row + 2:row + 3, :] * o_w[:, lanes])
    o_ref[...] = jnp.concatenate(outs, axis=0).T


def _nsa_prompt(pn, kcp, vct, ksp, kwp, *, n, t):
    G = NSA_KV_HEADS
    n_tiles = t // TQ
    n_blk = t // CMP_BLOCK
    col_block = lambda c: c // LANES
    full = lambda shape: pl.BlockSpec((1,) + shape, lambda b, i: (b,) + (0,) * len(shape))
    seq_cols = lambda c: pl.BlockSpec((t, LANES), lambda b, i: (b, col_block(c)))
    return pl.pallas_call(
        functools.partial(_nsa_prompt_kernel, n_blk=n_blk),
        grid=(n, n_tiles),
        in_specs=[
            pl.BlockSpec((TQ, NSA_Q_W), lambda b, i: (b * n_tiles + i, 0)),
            pl.BlockSpec((TQ, LANES), lambda b, i: (b * n_tiles + i, col_block(NSA_Q_W + 6 * NSA_KV_W))),
            full((G, n_blk, LANES)),
            full((G, HEAD_DIM, n_blk)),
            full((G, t, LANES)),
            seq_cols(NSA_Q_W + 3 * NSA_KV_W),
            full((G, t, LANES)),
            seq_cols(NSA_Q_W + 5 * NSA_KV_W),
        ],
        out_specs=pl.BlockSpec((TQ, NSA_Q_W), lambda b, i: (b * n_tiles + i, 0)),
        out_shape=jax.ShapeDtypeStruct((n * t, NSA_Q_W), F32),
        scratch_shapes=[pltpu.VMEM((n_blk, TQ), F32), pltpu.VMEM((n_blk, QW), F32)],
        compiler_params=_cparams(("parallel", "arbitrary")),
        name="nsa_prompt",
    )(pn, pn, kcp, vct, ksp, pn, kwp, pn)


PAGES_PER_STEP = 4
PAGE_GROUP = 8


SEL_ROWS = 2 * NSA_KV_HEADS * HEAD_DIM


def _nsa_sample_kernel(pt_ref, qt_ref, kcp_ref, vct_ref, win_ref, knew_ref, vnewt_ref, kwnew_ref, vwnewt_ref, gt_ref,
                       cache_ref, o_ref, score_ref, selb_ref, buf_ref, slot_ref, sem, *, past, n_new, n_blk, nb_pad,
                       n_pages):
    n = pl.program_id(0)
    G = NSA_KV_HEADS
    n_real = NSA_REP * n_new
    blocks_per_page = PAGE_SIZE // CMP_BLOCK
    nb_past = past // CMP_BLOCK
    w_buf = min(WINDOW, past)
    lane = lax.broadcasted_iota(jnp.int32, (1, LANES), 1)
    t_row = past + lane % n_new
    real = lane < n_real
    li = lax.broadcasted_iota(jnp.int32, (LANES, LANES), 0)
    lj = lax.broadcasted_iota(jnp.int32, (LANES, LANES), 1)
    same_token = ((li % n_new == lj % n_new) & (li < n_real)).astype(F32)
    head_sum = lambda p: jnp.dot(p, same_token, precision=lax.Precision.HIGHEST, preferred_element_type=F32)

    o_c = [_cmp_select_wide(qt_ref[0, g], kcp_ref[0, g], vct_ref[0, g], t_row, n_blk, nb_pad, score_ref,
                            selb_ref.at[g], head_sum, lambda a: a) for g in range(G)]

    def page_copy(lp, slot):
        return pltpu.make_async_copy(cache_ref.at[pt_ref[n, lp], pl.ds(CMP_ROWS, SEL_ROWS), :],
                                     buf_ref.at[pl.ds(slot * SEL_ROWS, SEL_ROWS), :], sem)

    def wanted_rows(first, count):
        rows = jnp.maximum(selb_ref[0, pl.ds(first, count), :], selb_ref[1, pl.ds(first, count), :])
        return jnp.max(jnp.where(real, rows, -jnp.inf)) > -1.0

    def page_body(lp, cnt):
        wanted = wanted_rows(lp * blocks_per_page, 1)
        for b in range(1, blocks_per_page):
            wanted = wanted | wanted_rows(lp * blocks_per_page + b, 1)

        @pl.when(wanted)
        def _():
            page_copy(lp, cnt).start()
            slot_ref[cnt] = lp

        return cnt + wanted.astype(jnp.int32)

    def group_body(pg, cnt):
        first = pl.multiple_of(pg * (PAGE_GROUP * blocks_per_page), PAGE_GROUP * blocks_per_page)
        return lax.cond(wanted_rows(first, PAGE_GROUP * blocks_per_page),
                        lambda c: lax.fori_loop(pg * PAGE_GROUP, (pg + 1) * PAGE_GROUP, page_body, c),
                        lambda c: c, cnt)

    n_slots = lax.fori_loop(0, n_pages // PAGE_GROUP, group_body, jnp.int32(0))
    n_steps = (n_slots + PAGES_PER_STEP - 1) // PAGES_PER_STEP

    def fill_body(j, c):
        slot = n_slots + j

        @pl.when(slot < n_steps * PAGES_PER_STEP)
        def _():
            page_copy(0, slot).start()
            slot_ref[slot] = -1

        return c

    lax.fori_loop(0, PAGES_PER_STEP - 1, fill_body, 0)

    def wait_body(s, c):
        page_copy(jnp.maximum(slot_ref[s], 0), s).wait()
        return c

    lax.fori_loop(0, n_steps * PAGES_PER_STEP, wait_body, 0)

    k_idx = lax.broadcasted_iota(jnp.int32, (PAGE_SIZE, LANES), 0)
    new_idx = lax.broadcasted_iota(jnp.int32, (knew_ref.shape[2], LANES), 0)
    pos_new = past + new_idx
    pos_win = past - w_buf + lax.broadcasted_iota(jnp.int32, (w_buf, LANES), 0)
    tn_dims = (((0,), (0,)), ((), ()))

    for g in range(G):
        q64 = qt_ref[0, g][0:HEAD_DIM, :]
        slope = jnp.zeros((1, LANES), F32)
        for r in range(NSA_REP):
            slope = jnp.where((lane >= r * n_new) & (lane < (r + 1) * n_new), ALIBI[g * NSA_REP + r], slope)

        def alibi(s, pos):
            return s - slope * (t_row - pos).astype(F32)

        def sel_body(st, state):
            kts, vts, pos, bias = [], [], [], []
            for j in range(PAGES_PER_STEP):
                slot = st * PAGES_PER_STEP + j
                base = pl.multiple_of(slot * SEL_ROWS, SEL_ROWS)
                kts.append(buf_ref[pl.ds(base + g * HEAD_DIM, HEAD_DIM), :])
                vts.append(buf_ref[pl.ds(base + (NSA_KV_HEADS + g) * HEAD_DIM, HEAD_DIM), :])
                lp = slot_ref[slot]
                live = jnp.where(lp >= 0, 0.0, -jnp.inf)
                lp = jnp.maximum(lp, 0)
                pos.append(lp * PAGE_SIZE + k_idx)
                bias += [jnp.broadcast_to(selb_ref[g, pl.ds(lp * blocks_per_page + b, 1), :] + live,
                                          (CMP_BLOCK, LANES)) for b in range(blocks_per_page)]
            kt = jnp.concatenate(kts, axis=1).astype(BF16)
            s = lax.dot_general(kt, q64, tn_dims, preferred_element_type=F32)
            s = alibi(s, jnp.concatenate(pos, axis=0)) + jnp.concatenate(bias, axis=0)
            return _wide_step(state, s, jnp.concatenate(vts, axis=1).astype(BF16))

        state = lax.fori_loop(0, n_steps, sel_body, _wide_init(LANES))
        s = alibi(jnp.dot(knew_ref[0, g], q64, preferred_element_type=F32), pos_new)
        s = jnp.where(pos_new <= t_row, s + selb_ref[g, pl.ds(nb_past, 1), :], -jnp.inf)
        o_s = _wide_out(_wide_step(state, s, vnewt_ref[0, g]))

        s = lax.dot_general(win_ref[0, 0, g].astype(BF16), q64, tn_dims, preferred_element_type=F32)
        s = jnp.where(t_row - pos_win < WINDOW, alibi(s, pos_win), -jnp.inf)
        state = _wide_step(_wide_init(LANES), s, win_ref[0, 1, g].astype(BF16))
        s = alibi(jnp.dot(kwnew_ref[0, g], q64, preferred_element_type=F32), pos_new)
        s = jnp.where((pos_new <= t_row) & (new_idx < n_new), s, -jnp.inf)
        o_w = _wide_out(_wide_step(state, s, vwnewt_ref[0, g]))

        o_ref[0, g] = (gt_ref[0, g, pl.ds(0, 1), :] * o_c[g] + gt_ref[0, g, pl.ds(1, 1), :] * o_s
                       + gt_ref[0, g, pl.ds(2, 1), :] * o_w)


def _nsa_sample(page_table, qt, kcp, vct, win_t, knew, vnewt, kwnew, vwnewt, gates_t, cache_t, *, past, n_new):
    n = qt.shape[0]
    G = NSA_KV_HEADS
    n_pages = page_table.shape[1]
    assert n_pages % PAGE_GROUP == 0
    n_blk = past // CMP_BLOCK + 1
    nb_pad = kcp.shape[2]
    w_buf = win_t.shape[4]
    assert w_buf == min(WINDOW, past)
    n_newp = knew.shape[2]
    full = lambda shape: pl.BlockSpec((1,) + shape, lambda b, pt: (b,) + (0,) * len(shape))
    kern = functools.partial(_nsa_sample_kernel, past=past, n_new=n_new, n_blk=n_blk, nb_pad=nb_pad, n_pages=n_pages)
    return pl.pallas_call(
        kern,
        grid_spec=pltpu.PrefetchScalarGridSpec(
            num_scalar_prefetch=1,
            grid=(n,),
            in_specs=[
                full((G, LANES, LANES)),
                full((G, nb_pad, LANES)),
                full((G, HEAD_DIM, nb_pad)),
                full((2, G, HEAD_DIM, w_buf)),
                full((G, n_newp, HEAD_DIM)),
                full((G, HEAD_DIM, n_newp)),
                full((G, n_newp, HEAD_DIM)),
                full((G, HEAD_DIM, n_newp)),
                full((G, 3, LANES)),
                pl.BlockSpec(memory_space=pl.ANY),
            ],
            out_specs=full((G, HEAD_DIM, LANES)),
            scratch_shapes=[
                pltpu.VMEM((nb_pad, LANES), F32),
                pltpu.VMEM((G, nb_pad, LANES), F32),
                pltpu.VMEM((n_pages * SEL_ROWS, PAGE_SIZE), F32),
                pltpu.SMEM((n_pages + PAGES_PER_STEP,), jnp.int32),
                pltpu.SemaphoreType.DMA(()),
            ],
        ),
        out_shape=jax.ShapeDtypeStruct((n, G, HEAD_DIM, LANES), F32),
        compiler_params=_cparams(("arbitrary",)),
        name="nsa_sample",
    )(page_table, qt, kcp, vct, win_t, knew, vnewt, kwnew, vwnewt, gates_t, cache_t)


def _alibi_features(pos):
    blk = (pos // CMP_BLOCK).astype(F32)
    off = (pos % CMP_BLOCK).astype(F32)
    pad = jnp.zeros(pos.shape + (HEAD_DIM - 2,), F32)
    return jnp.concatenate([blk[..., None], off[..., None], pad], axis=-1)


def _query_features(q):
    n, t, _ = q.shape
    qh = q.reshape(n, t, NSA_HEADS, HEAD_DIM) * (HEAD_DIM ** -0.5)
    slope = jnp.asarray(ALIBI, F32)
    feat = jnp.zeros((NSA_HEADS, HEAD_DIM), F32).at[:, 0].set(slope * CMP_BLOCK).at[:, 1].set(slope)
    feat = jnp.broadcast_to(feat, (n, t, NSA_HEADS, HEAD_DIM))
    return jnp.concatenate([qh, feat], axis=-1).reshape(n, t, NSA_HEADS * LANES).astype(BF16)


def _key_features(k, pos):
    n, l, _ = k.shape
    kh = k.reshape(n, l, NSA_KV_HEADS, HEAD_DIM).transpose(0, 2, 1, 3)
    feat = jnp.broadcast_to(_alibi_features(pos), (n, NSA_KV_HEADS, l, HEAD_DIM))
    return jnp.concatenate([kh, feat], axis=-1).astype(BF16)


def _rwkv_pre_kernel(p_ref, before_ref, shift_ref, mu_ref, w0_ref, wd_ref, a0_ref, wa_ref, wg_ref, kk_ref, ka_ref,
                     rk_ref, r_o, d_o, k_o, v_o, kk_o, b_o, g_o, bonus_o):
    p = p_ref[...]
    first = jnp.where(pl.program_id(1) == 0, shift_ref[0], before_ref[7:8, :])
    row = lax.broadcasted_iota(jnp.int32, p.shape, 0)
    prev = jnp.where(row == 0, first, pltpu.roll(p, 1, axis=0))
    xs = p + (prev - p) * mu_ref[...]
    W = RWKV_W
    r, k, v = xs[:, 0:W], xs[:, W:2 * W], xs[:, 2 * W:3 * W]
    lora = xs[:, 3 * W:3 * W + LANES]
    xg = xs[:, 3 * W + LANES:3 * W + 2 * LANES]
    z = w0_ref[...] + jnp.dot(jnp.tanh(lora).astype(BF16), wd_ref[...], preferred_element_type=F32)
    nz = -z
    softplus = jnp.maximum(nz, 0.0) + jnp.log(1.0 + jnp.exp(-jnp.abs(nz)))
    decay = jnp.exp(-jnp.exp(-softplus - 0.5))
    a = jax.nn.sigmoid(a0_ref[...] + jnp.dot(lora.astype(BF16), wa_ref[...], preferred_element_type=F32))
    g = jnp.dot(jax.nn.sigmoid(xg).astype(BF16), wg_ref[...], preferred_element_type=F32)
    kk = k * kk_ref[...]
    kk = kk * lax.rsqrt(jnp.maximum(_seg_sum(kk * kk, RWKV_HEAD_DIM), 1e-24))
    k_h = k * (1.0 + (a - 1.0) * ka_ref[...])
    r_o[...] = r
    d_o[...] = decay
    k_o[...] = k_h
    v_o[...] = v
    kk_o[...] = kk
    b_o[...] = kk * a
    g_o[...] = g
    bonus_o[...] = _seg_sum(r * k_h * rk_ref[...], RWKV_HEAD_DIM) * v


def _rwkv_pre(p, shift0, rw, *, n, t, tm):
    m = n * t
    assert tm % 8 == 0 and t % tm == 0
    tiles = t // tm
    row = lambda w: pl.BlockSpec((1, w), lambda b, i: (0, 0))
    mat = lambda a, c: pl.BlockSpec((a, c), lambda b, i: (0, 0))
    out = pl.BlockSpec((tm, RWKV_W), lambda b, i: (b * tiles + i, 0))
    return pl.pallas_call(
        _rwkv_pre_kernel,
        grid=(n, tiles),
        in_specs=[
            pl.BlockSpec((tm, RWKV_COLS), lambda b, i: (b * tiles + i, 0)),
            pl.BlockSpec((8, RWKV_COLS), lambda b, i: (jnp.maximum((b * tiles + i) * (tm // 8) - 1, 0), 0)),
            pl.BlockSpec((1, 1, RWKV_COLS), lambda b, i: (b, 0, 0)),
            row(RWKV_COLS), row(RWKV_W), mat(LANES, RWKV_W), row(RWKV_W), mat(LANES, RWKV_W),
            mat(GATE_LORA, RWKV_W), row(RWKV_W), row(RWKV_W), row(RWKV_W),
        ],
        out_specs=[out] * 8,
        out_shape=[jax.ShapeDtypeStruct((m, RWKV_W), F32)] * 8,
        compiler_params=_cparams(("parallel", "arbitrary")),
        name="rwkv_pre",
    )(p, p, shift0.reshape(n, 1, RWKV_COLS), *rw)


RW_J = RWKV_HEAD_DIM // 2
RW_PAIRS = LANES // 2


def _rwkv_scan_kernel(r_ref, d_ref, k_ref, kk_ref, b_ref, v_ref, s0_ref, o_ref, sout_ref, s_ref, *, tc):
    c = pl.program_id(1)

    @pl.when(c == 0)
    def _():
        s_ref[...] = s0_ref[0]

    def both_halves(x):
        return x + pltpu.roll(x, RW_PAIRS, axis=1)

    tiles = [pl.ds(k * SUBLANES, SUBLANES) for k in range(RWKV_HEAD_DIM // SUBLANES)]

    def step(t, carry):
        def key_row(ref, j):
            return jnp.broadcast_to(ref[0, t, pl.ds(j, 1), :], (SUBLANES, LANES))

        u = [jnp.zeros((SUBLANES, LANES), F32) for _ in tiles]
        for j in range(RW_J):
            kk = key_row(kk_ref, j)
            u = [acc + s_ref[j, rows, :] * kk for acc, rows in zip(u, tiles)]
        u = [-both_halves(x) for x in u]
        vt = [v_ref[0, t, rows, :] for rows in tiles]
        o = [jnp.zeros((SUBLANES, LANES), F32) for _ in tiles]
        for j in range(RW_J):
            d, b, k, r = key_row(d_ref, j), key_row(b_ref, j), key_row(k_ref, j), key_row(r_ref, j)
            for i, rows in enumerate(tiles):
                h = s_ref[j, rows, :] * d + u[i] * b + vt[i] * k
                s_ref[j, rows, :] = h
                o[i] = o[i] + h * r
        for acc, rows in zip(o, tiles):
            o_ref[0, t, rows, :] = both_halves(acc)
        return carry

    lax.fori_loop(0, tc, step, 0)
    sout_ref[0] = s_ref[...]


def _rwkv_scan(r, d, k, kk, b, v, s0, *, tc):
    pg, t = r.shape[:2]
    vec = pl.BlockSpec((1, tc, RW_J, LANES), lambda g, c: (g, c, 0, 0))
    val = pl.BlockSpec((1, tc, RWKV_HEAD_DIM, LANES), lambda g, c: (g, c, 0, 0))
    st = pl.BlockSpec((1, RW_J, RWKV_HEAD_DIM, LANES), lambda g, c: (g, 0, 0, 0))
    return pl.pallas_call(
        functools.partial(_rwkv_scan_kernel, tc=tc),
        grid=(pg, t // tc),
        in_specs=[vec, vec, vec, vec, vec, val, st],
        out_specs=[val, st],
        out_shape=[jax.ShapeDtypeStruct((pg, t, RWKV_HEAD_DIM, LANES), F32),
                   jax.ShapeDtypeStruct((pg, RW_J, RWKV_HEAD_DIM, LANES), F32)],
        scratch_shapes=[pltpu.VMEM((RW_J, RWKV_HEAD_DIM, LANES), F32)],
        compiler_params=_cparams(("parallel", "arbitrary")),
        name="rwkv_scan",
    )(r, d, k, kk, b, v, s0)


def _rwkv_post_kernel(o_ref, bonus_ref, g_ref, lg_ref, lb_ref, out_ref):
    o = o_ref[...]
    inv = 1.0 / RWKV_HEAD_DIM
    mean = _seg_sum(o, RWKV_HEAD_DIM) * inv
    cen = o - mean
    var = _seg_sum(cen * cen, RWKV_HEAD_DIM) * inv
    y = cen * lax.rsqrt(var + LNX_EPS) * lg_ref[...] + lb_ref[...]
    out_ref[...] = (y + bonus_ref[...]) * g_ref[...]


def _rwkv_post(o, bonus, g, lnx_g, lnx_b, *, tm):
    m = o.shape[0]
    blk = pl.BlockSpec((tm, RWKV_W), lambda i: (i, 0))
    row = pl.BlockSpec((1, RWKV_W), lambda i: (0, 0))
    return pl.pallas_call(
        _rwkv_post_kernel,
        grid=(m // tm,),
        in_specs=[blk, blk, blk, row, row],
        out_specs=blk,
        out_shape=jax.ShapeDtypeStruct((m, RWKV_W), F32),
        compiler_params=_cparams(("parallel",)),
        name="rwkv_post",
    )(o, bonus, g, lnx_g, lnx_b)


def _pairs_layout(x, n, t):
    pg = n * RWKV_HEADS // RW_PAIRS
    y = x.reshape(n, t, RWKV_HEADS, 2, RW_J).transpose(1, 4, 3, 0, 2).reshape(t, RW_J, 2, pg, RW_PAIRS)
    return y.transpose(3, 0, 1, 2, 4).reshape(pg, t, RW_J, LANES)


def _pairs_layout_v(x, n, t):
    pg = n * RWKV_HEADS // RW_PAIRS
    y = x.reshape(n, t, RWKV_HEADS, RWKV_HEAD_DIM).transpose(1, 3, 0, 2).reshape(t, RWKV_HEAD_DIM, pg, RW_PAIRS)
    y = y.transpose(2, 0, 1, 3)
    return jnp.concatenate([y, y], axis=-1)


def _pairs_unlayout_v(y, n, t):
    pg = y.shape[0]
    z = y[..., :RW_PAIRS].transpose(1, 2, 0, 3).reshape(t, RWKV_HEAD_DIM, n, RWKV_HEADS)
    return z.transpose(2, 0, 3, 1).reshape(n * t, RWKV_W)


def _state_layout(s0):
    n = s0.shape[0]
    pg = n * RWKV_HEADS // RW_PAIRS
    y = s0.reshape(pg, RW_PAIRS, RWKV_HEAD_DIM, 2, RW_J)
    return y.transpose(0, 4, 2, 3, 1).reshape(pg, RW_J, RWKV_HEAD_DIM, LANES)


def _state_unlayout(y, n):
    pg = y.shape[0]
    z = y.reshape(pg, RW_J, RWKV_HEAD_DIM, 2, RW_PAIRS).transpose(0, 4, 2, 3, 1)
    return z.reshape(n, RWKV_HEADS, RWKV_HEAD_DIM, RWKV_HEAD_DIM)


def _outproj_kernel(x_ref, a_ref, b_ref, wa_ref, wb_ref, o_ref):
    y = jnp.dot(a_ref[...].astype(BF16), wa_ref[...], preferred_element_type=F32)
    y = y + jnp.dot(b_ref[...].astype(BF16), wb_ref[...], preferred_element_type=F32)
    o_ref[...] = x_ref[...] + y


def _outproj(x, a, b, wa, wb, *, tm):
    m = x.shape[0]
    return pl.pallas_call(
        _outproj_kernel,
        grid=(m // tm,),
        in_specs=[
            pl.BlockSpec((tm, D_MODEL), lambda i: (i, 0)),
            pl.BlockSpec((tm, a.shape[1]), lambda i: (i, 0)),
            pl.BlockSpec((tm, b.shape[1]), lambda i: (i, 0)),
            pl.BlockSpec(wa.shape, lambda i: (0, 0)),
            pl.BlockSpec(wb.shape, lambda i: (0, 0)),
        ],
        out_specs=pl.BlockSpec((tm, D_MODEL), lambda i: (i, 0)),
        out_shape=jax.ShapeDtypeStruct((m, D_MODEL), F32),
        compiler_params=_cparams(("parallel",)),
        name="outproj",
    )(x, a, b, wa, wb)


def _memkv_kernel(x_ref, g_ref, w_ref, kg_ref, o_ref):
    xn = _rms(x_ref[...], g_ref[...]).astype(BF16)
    kv = jnp.dot(xn, w_ref[...], preferred_element_type=F32)
    xw = X_HEADS * X_HEAD_DIM
    o_ref[:, 0:xw] = _seg_rms(kv[:, 0:xw], kg_ref[...], X_HEAD_DIM)
    o_ref[:, xw:2 * xw] = kv[:, xw:2 * xw]


def _memkv(mem, g, w, kg, *, tm):
    m = mem.shape[0]
    xw = X_HEADS * X_HEAD_DIM
    return pl.pallas_call(
        _memkv_kernel,
        grid=(m // tm,),
        in_specs=[
            pl.BlockSpec((tm, D_MODEL), lambda i: (i, 0)),
            pl.BlockSpec((1, D_MODEL), lambda i: (0, 0)),
            pl.BlockSpec((D_MODEL, 2 * xw), lambda i: (0, 0)),
            pl.BlockSpec((1, xw), lambda i: (0, 0)),
        ],
        out_specs=pl.BlockSpec((tm, 2 * xw), lambda i: (i, 0)),
        out_shape=jax.ShapeDtypeStruct((m, 2 * xw), F32),
        compiler_params=_cparams(("parallel",)),
        name="memkv",
    )(mem, g, w, kg)


def _xattn_kernel(x_ref, mkv_ref, g_ref, wq_ref, qg_ref, wo_ref, o_ref):
    x = x_ref[0]
    xn = _rms(x, g_ref[...]).astype(BF16)
    q = jnp.dot(xn, wq_ref[...], preferred_element_type=F32)
    q = _seg_rms(q, qg_ref[...], X_HEAD_DIM) * (X_HEAD_DIM ** -0.5)
    xw = X_HEADS * X_HEAD_DIM
    outs = []
    for h in range(X_HEADS):
        lo = h * X_HEAD_DIM
        k = mkv_ref[0, :, lo:lo + X_HEAD_DIM].astype(BF16)
        v = mkv_ref[0, :, xw + lo:xw + lo + X_HEAD_DIM].astype(BF16)
        s = lax.dot_general(q[:, lo:lo + X_HEAD_DIM].astype(BF16), k, (((1,), (1,)), ((), ())),
                            preferred_element_type=F32)
        e = jnp.exp(s - jnp.max(s, axis=-1, keepdims=True))
        p = e / jnp.sum(e, axis=-1, keepdims=True)
        outs.append(jnp.dot(p.astype(BF16), v, preferred_element_type=F32))
    o = jnp.concatenate(outs, axis=-1).astype(BF16)
    o_ref[0] = x + jnp.dot(o, wo_ref[...], preferred_element_type=F32)


def _xattn(x, mkv, g, wq, qg, wo, *, tm):
    n, t, _ = x.shape
    xw = X_HEADS * X_HEAD_DIM
    n_mem = mkv.shape[1]
    return pl.pallas_call(
        _xattn_kernel,
        grid=(n, t // tm),
        in_specs=[
            pl.BlockSpec((1, tm, D_MODEL), lambda b, i: (b, i, 0)),
            pl.BlockSpec((1, n_mem, 2 * xw), lambda b, i: (b, 0, 0)),
            pl.BlockSpec((1, D_MODEL), lambda b, i: (0, 0)),
            pl.BlockSpec((D_MODEL, xw), lambda b, i: (0, 0)),
            pl.BlockSpec((1, xw), lambda b, i: (0, 0)),
            pl.BlockSpec((xw, D_MODEL), lambda b, i: (0, 0)),
        ],
        out_specs=pl.BlockSpec((1, tm, D_MODEL), lambda b, i: (b, i, 0)),
        out_shape=jax.ShapeDtypeStruct((n, t, D_MODEL), F32),
        compiler_params=_cparams(("parallel", "parallel")),
        name="xattn",
    )(x, mkv, g, wq, qg, wo)


def _row(v):
    return v.reshape(1, -1).astype(F32)


def _block_diag2(w):
    z = jnp.zeros_like(w)
    return jnp.concatenate([jnp.concatenate([w, z], axis=-1), jnp.concatenate([z, w], axis=-1)], axis=-2)


def _prep_weights(norm_ffn1_g, w_ffn1_gu, w_ffn1_down, norm_mix_g, w_in, w_out, q_norm_g, kc_norm_g, ks_norm_g,
                  kw_norm_g, gate_b, cmp_pe_k, cmp_pe_v, w_cmp_k1, w_cmp_k2, w_cmp_v1, w_cmp_v2, rwkv_mu, rwkv_w0,
                  w_decay2, rwkv_a0, w_iclr2, w_gate2, rwkv_k_k, rwkv_k_a, rwkv_r_k, lnx_g, lnx_b, norm_x_g,
                  norm_mem_g, w_xq, w_xkv, xq_norm_g, xk_norm_g, w_xo, norm_ffn2_g, w_ffn2_gu, w_ffn2_down,
                  norm_out_g):
    G = NSA_KV_HEADS
    W = {}
    W["ffn1"] = (_row(norm_ffn1_g), w_ffn1_gu.astype(BF16), w_ffn1_down.astype(BF16))
    W["ffn2"] = (_row(norm_ffn2_g), w_ffn2_gu.astype(BF16), w_ffn2_down.astype(BF16))
    W["norm_out_g"] = _row(norm_out_g)
    w_pad = jnp.concatenate([w_in[:, :NSA_COLS], jnp.zeros((D_MODEL, NSA_PAD - NSA_COLS), F32), w_in[:, NSA_COLS:]],
                            axis=1).astype(BF16)
    head_g = jnp.zeros((NSA_PAD,), F32)
    head_g = head_g.at[0:NSA_Q_W].set(jnp.tile(q_norm_g, NSA_HEADS))
    c = NSA_Q_W + 2 * NSA_KV_W
    head_g = head_g.at[c:c + NSA_KV_W].set(jnp.tile(ks_norm_g, G))
    c = NSA_Q_W + 4 * NSA_KV_W
    head_g = head_g.at[c:c + NSA_KV_W].set(jnp.tile(kw_norm_g, G))
    gb = jnp.zeros((LANES,), F32).at[0:3 * NSA_HEADS].set(gate_b)
    W["proj"] = (_row(norm_mix_g), w_pad, _row(head_g), _row(gb))
    pe = jnp.concatenate([cmp_pe_k, cmp_pe_k, cmp_pe_v, cmp_pe_v], axis=-1)[:, None, :]
    W["cmp"] = (pe, _block_diag2(w_cmp_k1).astype(BF16), _block_diag2(w_cmp_v1).astype(BF16),
                _block_diag2(w_cmp_k2).astype(BF16), _block_diag2(w_cmp_v2).astype(BF16),
                _row(jnp.tile(kc_norm_g, G)))
    pe_t = jnp.concatenate([jnp.tile(cmp_pe_k.T, (1, 2)), jnp.tile(cmp_pe_v.T, (1, 2))], axis=0)[:, None, :]
    w1_t = jnp.stack([_block_diag2(w_cmp_k1.transpose(1, 0, 2)), _block_diag2(w_cmp_v1.transpose(1, 0, 2))])
    w2_t = jnp.stack([_block_diag2(w_cmp_k2), _block_diag2(w_cmp_v2)])
    W["cmp_paged"] = (pe_t, w1_t.astype(BF16), w2_t.astype(BF16), _row(jnp.tile(kc_norm_g, G)))
    zl = jnp.zeros((DECAY_LORA, RWKV_W), F32)
    W["rwkv_pre"] = (_row(rwkv_mu), _row(rwkv_w0), jnp.concatenate([w_decay2, zl], axis=0).astype(BF16),
                     _row(rwkv_a0), jnp.concatenate([zl, w_iclr2], axis=0).astype(BF16), w_gate2.astype(BF16),
                     _row(rwkv_k_k), _row(rwkv_k_a), _row(rwkv_r_k))
    W["rwkv_post"] = (_row(lnx_g), _row(lnx_b))
    W["w_out"] = (w_out[:NSA_Q_W].astype(BF16), w_out[NSA_Q_W:].astype(BF16))
    W["xattn"] = (_row(norm_x_g), w_xq.astype(BF16), _row(jnp.tile(xq_norm_g, X_HEADS)), w_xo.astype(BF16))
    W["memkv"] = (_row(norm_mem_g), w_xkv.astype(BF16), _row(jnp.tile(xk_norm_g, X_HEADS)))
    return W


def _tile_rows(m, pref):
    return pref if m % pref == 0 else m


def _rwkv_group(pr, shift0, s0, W, n, t):
    m = n * t
    r, d, k, v, kk, b, g, bonus = _rwkv_pre(pr, shift0, W["rwkv_pre"], n=n, t=t, tm=_tile_rows(t, 256))
    lay = lambda a: _pairs_layout(a, n, t)
    o, s_new = _rwkv_scan(lay(r), lay(d), lay(k), lay(kk), lay(b), _pairs_layout_v(v, n, t), _state_layout(s0),
                          tc=min(t, 32))
    o = _pairs_unlayout_v(o, n, t)
    o = _rwkv_post(o, bonus, g, *W["rwkv_post"], tm=_tile_rows(m, 512))
    return o, _state_unlayout(s_new, n)


def _nsa_prompt_group(pn, W, n, t):
    G = NSA_KV_HEADS
    n_blk = t // CMP_BLOCK
    kc, vc = _compress_rows(pn, NSA_Q_W // LANES, W["cmp"], nblk=n_blk)
    pn3 = pn.reshape(n, t, NSA_PAD)
    col = lambda i: pn3[:, :, NSA_Q_W + i * NSA_KV_W:NSA_Q_W + (i + 1) * NSA_KV_W]
    pos = jnp.arange(t, dtype=jnp.int32)
    ksp = _key_features(col(2), pos)
    kwp = _key_features(col(4), pos)
    end = (jnp.arange(n_blk, dtype=jnp.int32) + 1) * CMP_BLOCK - 1
    kcp = _key_features(kc.reshape(n, n_blk, NSA_KV_W), end)
    vct = vc.reshape(n, n_blk, G, HEAD_DIM).transpose(0, 2, 3, 1).astype(BF16)
    return _nsa_prompt(pn, kcp, vct, ksp, kwp, n=n, t=t)


def _pad_axis(a, axis, size):
    pad = [(0, 0)] * a.ndim
    pad[axis] = (0, size - a.shape[axis])
    return jnp.pad(a, pad)


def _nsa_sample_group(pn, W, n, t_new, cache_kv, page_table, cache_win):
    G = NSA_KV_HEADS
    assert t_new <= CMP_BLOCK and t_new <= TQ
    past = page_table.shape[1] * PAGE_SIZE
    nb_past = past // CMP_BLOCK
    cache_t = cache_kv.transpose(0, 2, 3, 4, 1).reshape(cache_kv.shape[0], 4 * G * HEAD_DIM, PAGE_SIZE)
    win_t = cache_win.transpose(0, 2, 3, 4, 1)
    kc_past, vc_past = _compress_paged(page_table, cache_t, W["cmp_paged"])
    per_block = lambda a: a.reshape(n, G, nb_past, HEAD_DIM).transpose(0, 2, 1, 3).reshape(n, nb_past, NSA_KV_W)
    kc_past, vc_past = per_block(kc_past), per_block(vc_past)
    pn3 = pn.reshape(n, t_new, NSA_PAD)
    col = lambda i: pn3[:, :, NSA_Q_W + i * NSA_KV_W:NSA_Q_W + (i + 1) * NSA_KV_W]
    new_rows = _pad_axis(pn3[:, :, NSA_Q_W:NSA_Q_W + 2 * NSA_KV_W], 1, CMP_BLOCK).reshape(n * CMP_BLOCK, 2 * NSA_KV_W)
    kc_new, vc_new = _compress_rows(new_rows, 0, W["cmp"], nblk=n)
    nb_pad = -(-(nb_past + 1) // 16) * 16
    kc = _pad_axis(jnp.concatenate([kc_past, kc_new[:, None]], axis=1), 1, nb_pad)
    vc = _pad_axis(jnp.concatenate([vc_past, vc_new[:, None]], axis=1), 1, nb_pad)
    end_rel = (jnp.arange(nb_pad, dtype=jnp.int32) + 1) * CMP_BLOCK - 1 - past
    kcp = _key_features(kc, end_rel)
    vct = vc.reshape(n, nb_pad, G, HEAD_DIM).transpose(0, 2, 3, 1).astype(BF16)
    heads = lambda a: a.reshape(n, t_new, G, HEAD_DIM).transpose(0, 2, 1, 3)
    new_k = lambda i: _pad_axis(heads(col(i)), 2, KC).astype(BF16)
    new_vt = lambda i: _pad_axis(heads(col(i)).transpose(0, 1, 3, 2), 3, KC).astype(BF16)
    n_real = NSA_REP * t_new
    qt = _query_features(pn3[:, :, :NSA_Q_W]).reshape(n, t_new, G, NSA_REP, LANES)
    qt = _pad_axis(qt.transpose(0, 2, 4, 3, 1).reshape(n, G, LANES, n_real), 3, LANES)
    gates = pn3[:, :, NSA_Q_W + 6 * NSA_KV_W:NSA_COLS].reshape(n, t_new, G, NSA_REP, 3)
    gates_t = _pad_axis(gates.transpose(0, 2, 4, 3, 1).reshape(n, G, 3, n_real), 3, LANES)
    o_t = _nsa_sample(page_table, qt, kcp, vct, win_t, new_k(2), new_vt(3), new_k(4), new_vt(5), gates_t, cache_t,
                      past=past, n_new=t_new)
    o = o_t[:, :, :, :n_real].reshape(n, G, HEAD_DIM, NSA_REP, t_new)
    return o.transpose(0, 4, 1, 3, 2).reshape(n * t_new, NSA_Q_W)


def _layer(x, mkv, shift0, s0, W, nsa_fn):
    n, t, _ = x.shape
    m = n * t
    G = NSA_KV_HEADS
    x2 = x.reshape(m, D_MODEL)
    tm = _tile_rows(m, 512)
    x2 = _ffn(x2, *W["ffn1"], W["norm_out_g"], final_norm=False, tm=tm, tf=D_FF // 2)
    pn, pr = _proj(x2, *W["proj"], tm=_tile_rows(m, 256))
    o_nsa = nsa_fn(pn)
    o_rwkv, s_new = _rwkv_group(pr, shift0, s0, W, n, t)
    x2 = _outproj(x2, o_nsa, o_rwkv, *W["w_out"], tm=tm)
    x3 = _xattn(x2.reshape(n, t, D_MODEL), mkv, *W["xattn"], tm=_tile_rows(t, 512))
    y = _ffn(x3.reshape(m, D_MODEL), *W["ffn2"], W["norm_out_g"], final_norm=True, tm=tm, tf=D_FF // 2)
    pn3 = pn.reshape(n, t, NSA_PAD)
    kv_rows = pn3[:, :, NSA_Q_W:NSA_Q_W + 4 * NSA_KV_W].reshape(n, t, 4, G, HEAD_DIM)
    win_new = pn3[:, :, NSA_Q_W + 4 * NSA_KV_W:NSA_Q_W + 6 * NSA_KV_W].reshape(n, t, 2, G, HEAD_DIM)
    shift_new = pr.reshape(n, t, RWKV_COLS)[:, -1]
    return y.reshape(n, t, D_MODEL), kv_rows, win_new, s_new, shift_new


def kernel(x_prompt, x_sample, cache_nsa_kv, cache_nsa_win, state_rwkv_s, state_rwkv_shift, cache_mem_kv, page_table, mem_prompt, norm_ffn1_g, w_ffn1_gu, w_ffn1_down, norm_mix_g, w_in, w_out, q_norm_g, kc_norm_g, ks_norm_g, kw_norm_g, gate_b, cmp_pe_k, cmp_pe_v, w_cmp_k1, w_cmp_k2, w_cmp_v1, w_cmp_v2, rwkv_mu, rwkv_w0, w_decay2, rwkv_a0, w_iclr2, w_gate2, rwkv_k_k, rwkv_k_a, rwkv_r_k, lnx_g, lnx_b, norm_x_g, norm_mem_g, w_xq, w_xkv, xq_norm_g, xk_norm_g, w_xo, norm_ffn2_g, w_ffn2_gu, w_ffn2_down, norm_out_g):
    layer_weights = (norm_ffn1_g, w_ffn1_gu, w_ffn1_down, norm_mix_g, w_in, w_out, q_norm_g, kc_norm_g, ks_norm_g,
                     kw_norm_g, gate_b, cmp_pe_k, cmp_pe_v, w_cmp_k1, w_cmp_k2, w_cmp_v1, w_cmp_v2, rwkv_mu, rwkv_w0,
                     w_decay2, rwkv_a0, w_iclr2, w_gate2, rwkv_k_k, rwkv_k_a, rwkv_r_k, lnx_g, lnx_b, norm_x_g,
                     norm_mem_g, w_xq, w_xkv, xq_norm_g, xk_norm_g, w_xo, norm_ffn2_g, w_ffn2_gu, w_ffn2_down,
                     norm_out_g)
    assert w_in.shape[0] == 1, "single-layer trunk"
    W = _prep_weights(*(w[0] for w in layer_weights))
    n_p, t_p, _ = x_prompt.shape
    n_s, t_s, _ = x_sample.shape
    n_mem = mem_prompt.shape[1]
    xw = X_HEADS * X_HEAD_DIM

    mkv_p = _memkv(mem_prompt.reshape(n_p * n_mem, D_MODEL), *W["memkv"], tm=_tile_rows(n_p * n_mem, 512))
    mkv_p = mkv_p.reshape(n_p, n_mem, 2 * xw)
    y_p, kv_p, win_p, rs_p, sh_p = _layer(
        x_prompt, mkv_p, jnp.zeros((n_p, RWKV_COLS), F32),
        jnp.zeros((n_p, RWKV_HEADS, RWKV_HEAD_DIM, RWKV_HEAD_DIM), F32), W,
        lambda pn: _nsa_prompt_group(pn, W, n_p, t_p))
    win_p = win_p[:, t_p - min(WINDOW, t_p):]

    mkv_s = cache_mem_kv[0].reshape(n_s, n_mem, 2 * xw)
    y_s, kv_s, win_new, rs_s, sh_s = _layer(
        x_sample, mkv_s, state_rwkv_shift[0], state_rwkv_s[0], W,
        lambda pn: _nsa_sample_group(pn, W, n_s, t_s, cache_nsa_kv[0], page_table, cache_nsa_win[0]))
    win_s = jnp.concatenate([cache_nsa_win[0], win_new], axis=1)[:, t_s:]

    mkv_out = mkv_p.reshape(1, n_p, n_mem, 2, X_HEADS, X_HEAD_DIM)
    return (y_p, y_s, kv_p[None], kv_s[None], win_p[None], win_s[None], rs_p[None], rs_s[None], sh_p[None],
            sh_s[None], mkv_out)
```

```python
import functools
import math

import jax
import jax.numpy as jnp
from jax import lax
from jax.experimental import pallas as pl
from jax.experimental.pallas import tpu as pltpu

F32 = jnp.float32
BF16 = jnp.bfloat16

D_MODEL = 1024
NSA_HEADS = 8
NSA_KV_HEADS = 2
NSA_REP = NSA_HEADS // NSA_KV_HEADS
HEAD_DIM = 64
CMP_BLOCK = 64
N_SEL = 16
WINDOW = 512
CMP_HIDDEN = 128
PAGE_SIZE = 128
RWKV_HEADS = 8
RWKV_HEAD_DIM = 64
RWKV_W = RWKV_HEADS * RWKV_HEAD_DIM
DECAY_LORA = 64
ICLR_LORA = 64
GATE_LORA = 128
NSA_Q_W = NSA_HEADS * HEAD_DIM
NSA_KV_W = NSA_KV_HEADS * HEAD_DIM
NSA_COLS = NSA_Q_W + 6 * NSA_KV_W + 3 * NSA_HEADS
RWKV_COLS = 3 * RWKV_W + DECAY_LORA + ICLR_LORA + GATE_LORA
X_HEADS = 4
X_HEAD_DIM = 128
D_FF = 2816
RMS_EPS = 1e-6
LNX_EPS = 64e-5
FORCED_SCORE = 1e9

LANES = 128
SUBLANES = 8
NSA_PAD = 1408
P_COLS = NSA_PAD + RWKV_COLS
TQ = 128
KC = 128
VMEM_LIMIT = 56 * 1024 * 1024

ALIBI = tuple(2.0 ** (-8.0 * (h + 1.0) / NSA_HEADS) for h in range(NSA_HEADS))


def _cparams(sem):
    return pltpu.CompilerParams(dimension_semantics=sem, vmem_limit_bytes=VMEM_LIMIT)


def _rms(x, g):
    return x * lax.rsqrt(jnp.mean(x * x, axis=-1, keepdims=True) + RMS_EPS) * g


def _seg_ones(width, seg):
    r = lax.broadcasted_iota(jnp.int32, (width, width), 0) // seg
    c = lax.broadcasted_iota(jnp.int32, (width, width), 1) // seg
    return (r == c).astype(F32)


def _seg_sum(x, seg):
    ones = _seg_ones(LANES, seg)
    parts = [jnp.dot(x[:, c:c + LANES], ones, precision=lax.Precision.HIGHEST, preferred_element_type=F32)
             for c in range(0, x.shape[1], LANES)]
    return parts[0] if len(parts) == 1 else jnp.concatenate(parts, axis=1)


def _seg_rms(x, g, seg):
    if seg == LANES:
        parts = [_rms(x[:, c:c + LANES], g[:, c:c + LANES]) for c in range(0, x.shape[1], LANES)]
        return parts[0] if len(parts) == 1 else jnp.concatenate(parts, axis=1)
    return x * lax.rsqrt(_seg_sum(x * x, seg) * (1.0 / seg) + RMS_EPS) * g


def _ffn_kernel(x_ref, g_ref, wg_ref, wu_ref, wd_ref, go_ref, o_ref, xn_ref, acc_ref, *, final_norm):
    f = pl.program_id(1)

    @pl.when(f == 0)
    def _():
        xn_ref[...] = _rms(x_ref[...], g_ref[...]).astype(BF16)
        acc_ref[...] = jnp.zeros_like(acc_ref)

    xn = xn_ref[...]
    gate = jnp.dot(xn, wg_ref[...], preferred_element_type=F32)
    up = jnp.dot(xn, wu_ref[...], preferred_element_type=F32)
    h = (gate * jax.nn.sigmoid(gate) * up).astype(BF16)
    acc_ref[...] += jnp.dot(h, wd_ref[...], preferred_element_type=F32)

    @pl.when(f == pl.num_programs(1) - 1)
    def _():
        y = x_ref[...] + 0.5 * acc_ref[...]
        if final_norm:
            y = _rms(y, go_ref[...])
        o_ref[...] = y


def _ffn(x, g, w_gu, w_down, g_out, *, final_norm, tm, tf):
    m = x.shape[0]
    nf = D_FF // tf
    return pl.pallas_call(
        functools.partial(_ffn_kernel, final_norm=final_norm),
        grid=(m // tm, nf),
        in_specs=[
            pl.BlockSpec((tm, D_MODEL), lambda i, f: (i, 0)),
            pl.BlockSpec((1, D_MODEL), lambda i, f: (0, 0)),
            pl.BlockSpec((D_MODEL, tf), lambda i, f: (0, f)),
            pl.BlockSpec((D_MODEL, tf), lambda i, f: (0, nf + f)),
            pl.BlockSpec((tf, D_MODEL), lambda i, f: (f, 0)),
            pl.BlockSpec((1, D_MODEL), lambda i, f: (0, 0)),
        ],
        out_specs=pl.BlockSpec((tm, D_MODEL), lambda i, f: (i, 0)),
        out_shape=jax.ShapeDtypeStruct((m, D_MODEL), F32),
        scratch_shapes=[pltpu.VMEM((tm, D_MODEL), BF16), pltpu.VMEM((tm, D_MODEL), F32)],
        compiler_params=_cparams(("parallel", "arbitrary")),
        name="ffn",
    )(x, g, w_gu, w_gu, w_down, g_out)


def _proj_kernel(x_ref, g_ref, w_ref, hg_ref, gb_ref, on_ref, or_ref):
    xn = _rms(x_ref[...], g_ref[...]).astype(BF16)
    p = jnp.dot(xn, w_ref[...], preferred_element_type=F32)
    or_ref[...] = p[:, NSA_PAD:]
    on_ref[...] = p[:, 0:NSA_PAD]
    hg = hg_ref[...]
    on_ref[:, 0:NSA_Q_W] = _seg_rms(p[:, 0:NSA_Q_W], hg[:, 0:NSA_Q_W], HEAD_DIM)
    for c in (NSA_Q_W + 2 * NSA_KV_W, NSA_Q_W + 4 * NSA_KV_W):
        on_ref[:, c:c + NSA_KV_W] = _seg_rms(p[:, c:c + NSA_KV_W], hg[:, c:c + NSA_KV_W], HEAD_DIM)
    c = NSA_Q_W + 6 * NSA_KV_W
    on_ref[:, c:c + LANES] = jax.nn.sigmoid(p[:, c:c + LANES] + gb_ref[...])


def _proj(x, g, w_pad, head_g, gate_b, *, tm):
    m = x.shape[0]
    return pl.pallas_call(
        _proj_kernel,
        grid=(m // tm,),
        in_specs=[
            pl.BlockSpec((tm, D_MODEL), lambda i: (i, 0)),
            pl.BlockSpec((1, D_MODEL), lambda i: (0, 0)),
            pl.BlockSpec((D_MODEL, P_COLS), lambda i: (0, 0)),
            pl.BlockSpec((1, NSA_PAD), lambda i: (0, 0)),
            pl.BlockSpec((1, LANES), lambda i: (0, 0)),
        ],
        out_specs=[pl.BlockSpec((tm, NSA_PAD), lambda i: (i, 0)), pl.BlockSpec((tm, RWKV_COLS), lambda i: (i, 0))],
        out_shape=[jax.ShapeDtypeStruct((m, NSA_PAD), F32), jax.ShapeDtypeStruct((m, RWKV_COLS), F32)],
        compiler_params=_cparams(("parallel",)),
        name="proj",
    )(x, g, w_pad, head_g, gate_b)


def _compress_core(xk_ref, xv_ref, nblk, pe_ref, w1k_ref, w1v_ref, w2k_ref, w2v_ref, kcg_ref):
    def body(j, carry):
        acc_k, acc_v = carry
        pe = pe_ref[j]
        xk = xk_ref[pl.ds(j, nblk, stride=CMP_BLOCK), :] + pe[:, 0:LANES]
        xv = xv_ref[pl.ds(j, nblk, stride=CMP_BLOCK), :] + pe[:, LANES:2 * LANES]
        acc_k = acc_k + jnp.dot(xk.astype(BF16), w1k_ref[j], preferred_element_type=F32)
        acc_v = acc_v + jnp.dot(xv.astype(BF16), w1v_ref[j], preferred_element_type=F32)
        return acc_k, acc_v

    zero = jnp.zeros((nblk, 2 * CMP_HIDDEN), F32)
    acc_k, acc_v = lax.fori_loop(0, CMP_BLOCK, body, (zero, zero))
    hk = jax.nn.gelu(acc_k).astype(BF16)
    hv = jax.nn.gelu(acc_v).astype(BF16)
    kc = jnp.dot(hk, w2k_ref[...], preferred_element_type=F32)
    vc = jnp.dot(hv, w2v_ref[...], preferred_element_type=F32)
    return _seg_rms(kc, kcg_ref[...], HEAD_DIM), vc


def _compress_kernel(xk_ref, xv_ref, pe_ref, w1k_ref, w1v_ref, w2k_ref, w2v_ref, kcg_ref, kc_ref, vc_ref, *, nblk):
    kc, vc = _compress_core(xk_ref, xv_ref, nblk, pe_ref, w1k_ref, w1v_ref, w2k_ref, w2v_ref, kcg_ref)
    kc_ref[...] = kc
    vc_ref[...] = vc


def _cmp_weight_specs(imap):
    return [
        pl.BlockSpec((CMP_BLOCK, 1, 2 * LANES), imap(3)),
        pl.BlockSpec((CMP_BLOCK, LANES, 2 * CMP_HIDDEN), imap(3)),
        pl.BlockSpec((CMP_BLOCK, LANES, 2 * CMP_HIDDEN), imap(3)),
        pl.BlockSpec((2 * CMP_HIDDEN, LANES), imap(2)),
        pl.BlockSpec((2 * CMP_HIDDEN, LANES), imap(2)),
        pl.BlockSpec((1, LANES), imap(2)),
    ]


def _compress_rows(rows, col_block, cw, *, nblk):
    m = rows.shape[0]
    steps = m // (nblk * CMP_BLOCK)
    imap = lambda nd: (lambda i: (0,) * nd)
    return pl.pallas_call(
        functools.partial(_compress_kernel, nblk=nblk),
        grid=(steps,),
        in_specs=[pl.BlockSpec((nblk * CMP_BLOCK, LANES), lambda i: (i, col_block)),
                  pl.BlockSpec((nblk * CMP_BLOCK, LANES), lambda i: (i, col_block + 1))] + _cmp_weight_specs(imap),
        out_specs=[pl.BlockSpec((nblk, LANES), lambda i: (i, 0))] * 2,
        out_shape=[jax.ShapeDtypeStruct((steps * nblk, LANES), F32)] * 2,
        compiler_params=_cparams(("parallel",)),
        name="compress",
    )(rows, rows, *cw)


CMP_ROWS = 2 * NSA_KV_HEADS * HEAD_DIM
CMP_FEATS = 8


def _compress_paged_kernel(pt_ref, cache_ref, pe_ref, w1_ref, w2_ref, kcg_ref, kc_ref, vc_ref, buf_ref, sem, *,
                           n_pages):
    n = pl.program_id(0)

    def page_copy(i):
        return pltpu.make_async_copy(cache_ref.at[pt_ref[n, i], pl.ds(0, CMP_ROWS), :], buf_ref.at[:, i, :], sem)

    def start(i, c):
        page_copy(i).start()
        return c

    def wait(i, c):
        page_copy(i).wait()
        return c

    lax.fori_loop(0, n_pages, start, 0)
    lax.fori_loop(0, n_pages, wait, 0)
    blocks_per_page = PAGE_SIZE // CMP_BLOCK
    for c, out_ref in enumerate((kc_ref, vc_ref)):
        for g in range(NSA_KV_HEADS):
            def body(it, acc):
                d0 = pl.multiple_of(it * CMP_FEATS, CMP_FEATS)
                x = jnp.concatenate(
                    [buf_ref[(c * NSA_KV_HEADS + g) * HEAD_DIM + d0 + u] + pe_ref[c * HEAD_DIM + d0 + u]
                     for u in range(CMP_FEATS)], axis=1)
                w = w1_ref[c, pl.ds(d0, CMP_FEATS)].reshape(CMP_FEATS * PAGE_SIZE, blocks_per_page * CMP_HIDDEN)
                return acc + jnp.dot(x.astype(BF16), w, preferred_element_type=F32)

            acc = lax.fori_loop(0, HEAD_DIM // CMP_FEATS, body,
                                jnp.zeros((n_pages, blocks_per_page * CMP_HIDDEN), F32))
            out = jnp.dot(jax.nn.gelu(acc).astype(BF16), w2_ref[c], preferred_element_type=F32)
            if c == 0:
                out = _seg_rms(out, kcg_ref[...], HEAD_DIM)
            out_ref[0, g] = out


def _compress_paged(page_table, cache_t, cw):
    nb, n_pages = page_table.shape
    width = (PAGE_SIZE // CMP_BLOCK) * HEAD_DIM
    const = lambda shape: pl.BlockSpec(shape, lambda n, pt: (0,) * len(shape))
    out = pl.BlockSpec((1, NSA_KV_HEADS, n_pages, width), lambda n, pt: (n, 0, 0, 0))
    return pl.pallas_call(
        functools.partial(_compress_paged_kernel, n_pages=n_pages),
        grid_spec=pltpu.PrefetchScalarGridSpec(
            num_scalar_prefetch=1,
            grid=(nb,),
            in_specs=[pl.BlockSpec(memory_space=pl.ANY)] + [const(w.shape) for w in cw],
            out_specs=[out, out],
            scratch_shapes=[pltpu.VMEM((CMP_ROWS, n_pages, PAGE_SIZE), F32), pltpu.SemaphoreType.DMA(())],
        ),
        out_shape=[jax.ShapeDtypeStruct((nb, NSA_KV_HEADS, n_pages, width), F32)] * 2,
        compiler_params=_cparams(("arbitrary",)),
        name="compress_paged",
    )(page_table, cache_t, *cw)


QW = NSA_REP * TQ
SEL_KC = 512
RANK_UNROLL = 8


def _wide_init(width=QW):
    return (jnp.full((1, width), -jnp.inf, F32), jnp.zeros((1, width), F32), jnp.zeros((HEAD_DIM, width), F32))


def _wide_step(state, s, vt=None, v_rows=None, group=0):
    m, l, acc = state
    m_new = jnp.maximum(m, jnp.max(s, axis=0, keepdims=True))
    m_safe = jnp.where(m_new == -jnp.inf, 0.0, m_new)
    alpha = jnp.exp(m - m_safe)
    p = jnp.exp(s - m_safe)
    l = alpha * l + jnp.sum(p, axis=0, keepdims=True)
    if vt is not None:
        pv = jnp.dot(vt, p.astype(BF16), preferred_element_type=F32)
    else:
        pv = lax.dot_general(v_rows, p.astype(BF16), (((0,), (0,)), ((), ())), preferred_element_type=F32)
        pv = pv[group * HEAD_DIM:(group + 1) * HEAD_DIM, :]
    return m_new, l, alpha * acc + pv


def _wide_out(state):
    _, l, acc = state
    return acc / jnp.maximum(l, 1e-30)


def _cmp_branch(qt, kcp, vct, t_row, n_blk):
    b_col = lax.broadcasted_iota(jnp.int32, (kcp.shape[0], qt.shape[1]), 0)
    valid = (t_row >= (b_col + 1) * CMP_BLOCK - 1) & (b_col < n_blk)
    s = jnp.where(valid, jnp.dot(kcp, qt, preferred_element_type=F32), -jnp.inf)
    m = jnp.max(s, axis=0, keepdims=True)
    e = jnp.exp(s - jnp.where(m == -jnp.inf, 0.0, m))
    p = e / jnp.maximum(jnp.sum(e, axis=0, keepdims=True), 1e-30)
    return jnp.dot(vct, p.astype(BF16), preferred_element_type=F32), p


def _select_blocks(imp, t_row, n_blk, score_ref):
    nb_pad, w = imp.shape
    b_col = lax.broadcasted_iota(jnp.int32, (nb_pad, w), 0)
    cur = t_row // CMP_BLOCK
    forced = (b_col == 0) | (b_col == cur) | (b_col == cur - 1)
    score = jnp.where(forced, FORCED_SCORE, jnp.where(b_col <= cur, imp, -FORCED_SCORE))
    score = jnp.where(b_col < n_blk, score, -jnp.inf)
    score_ref[...] = score

    def rank_body(it, cnt):
        base = pl.multiple_of(it * RANK_UNROLL, RANK_UNROLL)
        for u in range(RANK_UNROLL):
            bp = base + u
            row = jnp.broadcast_to(score_ref[pl.ds(bp, 1), :], (nb_pad, w))
            ahead = (row > score) | ((row == score) & (b_col > bp))
            cnt = cnt + jnp.where(ahead, 1.0, 0.0)
        return cnt

    cnt = lax.fori_loop(0, nb_pad // RANK_UNROLL, rank_body, jnp.zeros((nb_pad, w), F32))
    return jnp.where((cnt < N_SEL) & (b_col < n_blk), 0.0, -jnp.inf)


def _sum_lane_chunks(p):
    imp = p[:, 0:TQ]
    for r in range(1, NSA_REP):
        imp = imp + p[:, r * TQ:(r + 1) * TQ]
    return imp


def _nsa_prompt_kernel(q_ref, gate_ref, kcp_ref, vct_ref, ksp_ref, vs_ref, kwp_ref, vw_ref, o_ref,
                       score_ref, selb_ref, *, n_blk):
    i = pl.program_id(1)
    t0 = i * TQ
    t_row = t0 + (lax.broadcasted_iota(jnp.int32, (1, QW), 1) & (TQ - 1))
    blocks_per_step = SEL_KC // CMP_BLOCK
    wk = WINDOW + TQ
    ws = pl.multiple_of(jnp.maximum(t0 - WINDOW, 0), TQ)
    dist = t_row - (ws + lax.broadcasted_iota(jnp.int32, (wk, QW), 0))
    wbias = jnp.where((dist >= 0) & (dist < WINDOW), 0.0, -jnp.inf)
    n_steps = (t0 + TQ + SEL_KC - 1) // SEL_KC

    q_t = (q_ref[...] * (HEAD_DIM ** -0.5)).T
    gate_t = gate_ref[...].T
    f_row = lax.broadcasted_iota(jnp.int32, (HEAD_DIM, TQ), 0)
    groups = range(NSA_KV_HEADS)
    qts, o_c = [], []
    for g in groups:
        cols = []
        for r in range(NSA_REP):
            h = g * NSA_REP + r
            feat = jnp.where(f_row == 0, ALIBI[h] * CMP_BLOCK, jnp.where(f_row == 1, ALIBI[h], 0.0))
            cols.append(jnp.concatenate([q_t[h * HEAD_DIM:(h + 1) * HEAD_DIM, :], feat], axis=0))
        qts.append(jnp.concatenate(cols, axis=1).astype(BF16))
        o, p = _cmp_branch(qts[g], kcp_ref[0, g], vct_ref[0, g], t_row, n_blk)
        o_c.append(o)
        selb = _select_blocks(_sum_lane_chunks(p), t_row[:, 0:TQ], n_blk, score_ref)
        selb_ref[g] = jnp.concatenate([selb] * NSA_REP, axis=1)

    def step_scores(c, g):
        k0 = pl.multiple_of(c * SEL_KC, SEL_KC)
        s = jnp.dot(ksp_ref[0, g, pl.ds(k0, SEL_KC), :], qts[g], preferred_element_type=F32)
        return jnp.concatenate(
            [s[b * CMP_BLOCK:(b + 1) * CMP_BLOCK, :] + selb_ref[g, pl.ds(c * blocks_per_step + b, 1), :]
             for b in range(blocks_per_step)], axis=0)

    def step_values(c):
        return vs_ref[pl.ds(pl.multiple_of(c * SEL_KC, SEL_KC), SEL_KC), :].astype(BF16)

    def sel_body(c, states):
        v = step_values(c)
        return tuple(_wide_step(states[g], step_scores(c, g), v_rows=v, group=g) for g in groups)

    states = lax.fori_loop(0, n_steps - 1, sel_body, tuple(_wide_init() for _ in groups))
    c_last = n_steps - 1
    causal = c_last * SEL_KC + lax.broadcasted_iota(jnp.int32, (SEL_KC, QW), 0) <= t_row
    v = step_values(c_last)
    o_s = [_wide_out(_wide_step(states[g], jnp.where(causal, step_scores(c_last, g), -jnp.inf), v_rows=v, group=g))
           for g in groups]

    v = vw_ref[pl.ds(ws, wk), :].astype(BF16)
    o_w = [_wide_out(_wide_step(
        _wide_init(), jnp.dot(kwp_ref[0, g, pl.ds(ws, wk), :], qts[g], preferred_element_type=F32) + wbias,
        v_rows=v, group=g)) for g in groups]

    outs = []
    for g in groups:
        for r in range(NSA_REP):
            lanes = slice(r * TQ, (r + 1) * TQ)
            row = (g * NSA_REP + r) * 3
            outs.append(gate_t[row:row + 1, :] * o_c[g][:, lanes] + gate_t[row + 1:row + 2, :] * o_s[g][:, lanes]
                        + gate_t[row + 2:row + 3, :] * o_w[g][:, lanes])
    o_ref[...] = jnp.concatenate(outs, axis=0).T


def _nsa_prompt(pn, kcp, vct, ksp, kwp, *, n, t):
    G = NSA_KV_HEADS
    n_tiles = t // TQ
    n_blk = t // CMP_BLOCK
    col_block = lambda c: c // LANES
    full = lambda shape: pl.BlockSpec((1,) + shape, lambda b, i: (b,) + (0,) * len(shape))
    seq_cols = lambda c: pl.BlockSpec((t, LANES), lambda b, i: (b, col_block(c)))
    return pl.pallas_call(
        functools.partial(_nsa_prompt_kernel, n_blk=n_blk),
        grid=(n, n_tiles),
        in_specs=[
            pl.BlockSpec((TQ, NSA_Q_W), lambda b, i: (b * n_tiles + i, 0)),
            pl.BlockSpec((TQ, LANES), lambda b, i: (b * n_tiles + i, col_block(NSA_Q_W + 6 * NSA_KV_W))),
            full((G, n_blk, LANES)),
            full((G, HEAD_DIM, n_blk)),
            full((G, t, LANES)),
            seq_cols(NSA_Q_W + 3 * NSA_KV_W),
            full((G, t, LANES)),
            seq_cols(NSA_Q_W + 5 * NSA_KV_W),
        ],
        out_specs=pl.BlockSpec((TQ, NSA_Q_W), lambda b, i: (b * n_tiles + i, 0)),
        out_shape=jax.ShapeDtypeStruct((n * t, NSA_Q_W), F32),
        scratch_shapes=[pltpu.VMEM((n_blk, TQ), F32), pltpu.VMEM((G, n_blk, QW), F32)],
        compiler_params=_cparams(("parallel", "arbitrary")),
        name="nsa_prompt",
    )(pn, pn, kcp, vct, ksp, pn, kwp, pn)


PAGES_PER_STEP = 4
PAGE_GROUP = 8


SEL_ROWS = 2 * NSA_KV_HEADS * HEAD_DIM


def _nsa_sample_cmp_kernel(qt_ref, kcp_ref, vct_ref, o_ref, imp_ref, *, past, n_new, n_blk):
    n_real = NSA_REP * n_new
    t_row = past + lax.broadcasted_iota(jnp.int32, (1, LANES), 1) % n_new
    li = lax.broadcasted_iota(jnp.int32, (LANES, LANES), 0)
    lj = lax.broadcasted_iota(jnp.int32, (LANES, LANES), 1)
    same_token = ((li % n_new == lj) & (li < n_real)).astype(F32)
    for g in range(NSA_KV_HEADS):
        o, p = _cmp_branch(qt_ref[0, g], kcp_ref[0, g], vct_ref[0, g], t_row, n_blk)
        o_ref[0, g] = o
        imp_ref[0, g] = jnp.dot(p, same_token, precision=lax.Precision.HIGHEST, preferred_element_type=F32)


def _nsa_sample_cmp(qt, kcp, vct, *, past, n_new):
    n, G, nb_pad = kcp.shape[:3]
    full = lambda shape: pl.BlockSpec((1,) + shape, lambda b: (b,) + (0,) * len(shape))
    return pl.pallas_call(
        functools.partial(_nsa_sample_cmp_kernel, past=past, n_new=n_new, n_blk=past // CMP_BLOCK + 1),
        grid=(n,),
        in_specs=[full((G, LANES, LANES)), full((G, nb_pad, LANES)), full((G, HEAD_DIM, nb_pad))],
        out_specs=[full((G, HEAD_DIM, LANES)), full((G, nb_pad, LANES))],
        out_shape=[jax.ShapeDtypeStruct((n, G, HEAD_DIM, LANES), F32),
                   jax.ShapeDtypeStruct((n, G, nb_pad, LANES), F32)],
        compiler_params=_cparams(("parallel",)),
        name="nsa_sample_cmp",
    )(qt, kcp, vct)


def _nsa_rank_kernel(imp_ref, selb_ref, score_ref, *, past, n_new, n_blk):
    t_row = past + lax.broadcasted_iota(jnp.int32, (1, imp_ref.shape[2]), 1) % n_new
    selb_ref[0] = _select_blocks(imp_ref[0], t_row, n_blk, score_ref)


def _nsa_rank(imp, *, past, n_new):
    G, nb_pad, w = imp.shape
    blk = pl.BlockSpec((1, nb_pad, w), lambda g: (g, 0, 0))
    return pl.pallas_call(
        functools.partial(_nsa_rank_kernel, past=past, n_new=n_new, n_blk=past // CMP_BLOCK + 1),
        grid=(G,),
        in_specs=[blk],
        out_specs=blk,
        out_shape=jax.ShapeDtypeStruct((G, nb_pad, w), F32),
        scratch_shapes=[pltpu.VMEM((nb_pad, w), F32)],
        compiler_params=_cparams(("parallel",)),
        name="nsa_rank",
    )(imp)


def _nsa_sample_kernel(pt_ref, qt_ref, oc_ref, selb_ref, win_ref, knew_ref, vnewt_ref, kwnew_ref, vwnewt_ref, gt_ref,
                       cache_ref, o_ref, buf_ref, slot_ref, sem, *, past, n_new, n_pages):
    n = pl.program_id(0)
    G = NSA_KV_HEADS
    n_real = NSA_REP * n_new
    blocks_per_page = PAGE_SIZE // CMP_BLOCK
    nb_past = past // CMP_BLOCK
    w_buf = min(WINDOW, past)
    lane = lax.broadcasted_iota(jnp.int32, (1, LANES), 1)
    t_row = past + lane % n_new
    real = lane < n_real
    selb_ref = selb_ref.at[0]

    def page_copy(lp, slot):
        return pltpu.make_async_copy(cache_ref.at[pt_ref[n, lp], pl.ds(CMP_ROWS, SEL_ROWS), :],
                                     buf_ref.at[pl.ds(slot * SEL_ROWS, SEL_ROWS), :], sem)

    def wanted_rows(first, count):
        rows = jnp.maximum(selb_ref[0, pl.ds(first, count), :], selb_ref[1, pl.ds(first, count), :])
        return jnp.max(jnp.where(real, rows, -jnp.inf)) > -1.0

    def page_body(lp, cnt):
        wanted = wanted_rows(lp * blocks_per_page, 1)
        for b in range(1, blocks_per_page):
            wanted = wanted | wanted_rows(lp * blocks_per_page + b, 1)

        @pl.when(wanted)
        def _():
            page_copy(lp, cnt).start()
            slot_ref[cnt] = lp

        return cnt + wanted.astype(jnp.int32)

    def group_body(pg, cnt):
        first = pl.multiple_of(pg * (PAGE_GROUP * blocks_per_page), PAGE_GROUP * blocks_per_page)
        return lax.cond(wanted_rows(first, PAGE_GROUP * blocks_per_page),
                        lambda c: lax.fori_loop(pg * PAGE_GROUP, (pg + 1) * PAGE_GROUP, page_body, c),
                        lambda c: c, cnt)

    n_slots = lax.fori_loop(0, n_pages // PAGE_GROUP, group_body, jnp.int32(0))
    n_steps = (n_slots + PAGES_PER_STEP - 1) // PAGES_PER_STEP

    def fill_body(j, c):
        slot = n_slots + j

        @pl.when(slot < n_steps * PAGES_PER_STEP)
        def _():
            page_copy(0, slot).start()
            slot_ref[slot] = -1

        return c

    lax.fori_loop(0, PAGES_PER_STEP - 1, fill_body, 0)

    def wait_body(s, c):
        page_copy(jnp.maximum(slot_ref[s], 0), s).wait()
        return c

    lax.fori_loop(0, n_steps * PAGES_PER_STEP, wait_body, 0)

    k_idx = lax.broadcasted_iota(jnp.int32, (PAGE_SIZE, LANES), 0)
    new_idx = lax.broadcasted_iota(jnp.int32, (knew_ref.shape[2], LANES), 0)
    pos_new = past + new_idx
    pos_win = past - w_buf + lax.broadcasted_iota(jnp.int32, (w_buf, LANES), 0)
    tn_dims = (((0,), (0,)), ((), ()))

    for g in range(G):
        q64 = qt_ref[0, g][0:HEAD_DIM, :]
        slope = jnp.zeros((1, LANES), F32)
        for r in range(NSA_REP):
            slope = jnp.where((lane >= r * n_new) & (lane < (r + 1) * n_new), ALIBI[g * NSA_REP + r], slope)

        def alibi(s, pos):
            return s - slope * (t_row - pos).astype(F32)

        def sel_body(st, state):
            kts, vts, pos, bias = [], [], [], []
            for j in range(PAGES_PER_STEP):
                slot = st * PAGES_PER_STEP + j
                base = pl.multiple_of(slot * SEL_ROWS, SEL_ROWS)
                kts.append(buf_ref[pl.ds(base + g * HEAD_DIM, HEAD_DIM), :])
                vts.append(buf_ref[pl.ds(base + (NSA_KV_HEADS + g) * HEAD_DIM, HEAD_DIM), :])
                lp = slot_ref[slot]
                live = jnp.where(lp >= 0, 0.0, -jnp.inf)
                lp = jnp.maximum(lp, 0)
                pos.append(lp * PAGE_SIZE + k_idx)
                bias += [jnp.broadcast_to(selb_ref[g, pl.ds(lp * blocks_per_page + b, 1), :] + live,
                                          (CMP_BLOCK, LANES)) for b in range(blocks_per_page)]
            kt = jnp.concatenate(kts, axis=1).astype(BF16)
            s = lax.dot_general(kt, q64, tn_dims, preferred_element_type=F32)
            s = alibi(s, jnp.concatenate(pos, axis=0)) + jnp.concatenate(bias, axis=0)
            return _wide_step(state, s, jnp.concatenate(vts, axis=1).astype(BF16))

        state = lax.fori_loop(0, n_steps, sel_body, _wide_init(LANES))
        s = alibi(jnp.dot(knew_ref[0, g], q64, preferred_element_type=F32), pos_new)
        s = jnp.where(pos_new <= t_row, s + selb_ref[g, pl.ds(nb_past, 1), :], -jnp.inf)
        o_s = _wide_out(_wide_step(state, s, vnewt_ref[0, g]))

        s = lax.dot_general(win_ref[0, 0, g].astype(BF16), q64, tn_dims, preferred_element_type=F32)
        s = jnp.where(t_row - pos_win < WINDOW, alibi(s, pos_win), -jnp.inf)
        state = _wide_step(_wide_init(LANES), s, win_ref[0, 1, g].astype(BF16))
        s = alibi(jnp.dot(kwnew_ref[0, g], q64, preferred_element_type=F32), pos_new)
        s = jnp.where((pos_new <= t_row) & (new_idx < n_new), s, -jnp.inf)
        o_w = _wide_out(_wide_step(state, s, vwnewt_ref[0, g]))

        o_ref[0, g] = (gt_ref[0, g, pl.ds(0, 1), :] * oc_ref[0, g] + gt_ref[0, g, pl.ds(1, 1), :] * o_s
                       + gt_ref[0, g, pl.ds(2, 1), :] * o_w)


def _nsa_sample(page_table, qt, o_c, selb, win_t, knew, vnewt, kwnew, vwnewt, gates_t, cache_t, *, past, n_new):
    n = qt.shape[0]
    G = NSA_KV_HEADS
    n_pages = page_table.shape[1]
    assert n_pages % PAGE_GROUP == 0
    nb_pad = selb.shape[2]
    w_buf = win_t.shape[4]
    assert w_buf == min(WINDOW, past)
    n_newp = knew.shape[2]
    full = lambda shape: pl.BlockSpec((1,) + shape, lambda b, pt: (b,) + (0,) * len(shape))
    kern = functools.partial(_nsa_sample_kernel, past=past, n_new=n_new, n_pages=n_pages)
    return pl.pallas_call(
        kern,
        grid_spec=pltpu.PrefetchScalarGridSpec(
            num_scalar_prefetch=1,
            grid=(n,),
            in_specs=[
                full((G, LANES, LANES)),
                full((G, HEAD_DIM, LANES)),
                full((G, nb_pad, LANES)),
                full((2, G, HEAD_DIM, w_buf)),
                full((G, n_newp, HEAD_DIM)),
                full((G, HEAD_DIM, n_newp)),
                full((G, n_newp, HEAD_DIM)),
                full((G, HEAD_DIM, n_newp)),
                full((G, 3, LANES)),
                pl.BlockSpec(memory_space=pl.ANY),
            ],
            out_specs=full((G, HEAD_DIM, LANES)),
            scratch_shapes=[
                pltpu.VMEM((n_pages * SEL_ROWS, PAGE_SIZE), F32),
                pltpu.SMEM((n_pages + PAGES_PER_STEP,), jnp.int32),
                pltpu.SemaphoreType.DMA(()),
            ],
        ),
        out_shape=jax.ShapeDtypeStruct((n, G, HEAD_DIM, LANES), F32),
        compiler_params=_cparams(("arbitrary",)),
        name="nsa_sample",
    )(page_table, qt, o_c, selb, win_t, knew, vnewt, kwnew, vwnewt, gates_t, cache_t)


def _alibi_features(pos):
    blk = (pos // CMP_BLOCK).astype(F32)
    off = (pos % CMP_BLOCK).astype(F32)
    pad = jnp.zeros(pos.shape + (HEAD_DIM - 2,), F32)
    return jnp.concatenate([blk[..., None], off[..., None], pad], axis=-1)


def _query_features(q):
    n, t, _ = q.shape
    qh = q.reshape(n, t, NSA_HEADS, HEAD_DIM) * (HEAD_DIM ** -0.5)
    slope = jnp.asarray(ALIBI, F32)
    feat = jnp.zeros((NSA_HEADS, HEAD_DIM), F32).at[:, 0].set(slope * CMP_BLOCK).at[:, 1].set(slope)
    feat = jnp.broadcast_to(feat, (n, t, NSA_HEADS, HEAD_DIM))
    return jnp.concatenate([qh, feat], axis=-1).reshape(n, t, NSA_HEADS * LANES).astype(BF16)


def _key_features(k, pos):
    n, l, _ = k.shape
    kh = k.reshape(n, l, NSA_KV_HEADS, HEAD_DIM).transpose(0, 2, 1, 3)
    feat = jnp.broadcast_to(_alibi_features(pos), (n, NSA_KV_HEADS, l, HEAD_DIM))
    return jnp.concatenate([kh, feat], axis=-1).astype(BF16)


def _rwkv_pre_kernel(p_ref, before_ref, shift_ref, mu_ref, w0_ref, wd_ref, a0_ref, wa_ref, wg_ref, kk_ref, ka_ref,
                     rk_ref, r_o, d_o, k_o, v_o, kk_o, b_o, g_o, bonus_o):
    p = p_ref[...]
    first = jnp.where(pl.program_id(1) == 0, shift_ref[0], before_ref[7:8, :])
    row = lax.broadcasted_iota(jnp.int32, p.shape, 0)
    prev = jnp.where(row == 0, first, pltpu.roll(p, 1, axis=0))
    xs = p + (prev - p) * mu_ref[...]
    W = RWKV_W
    r, k, v = xs[:, 0:W], xs[:, W:2 * W], xs[:, 2 * W:3 * W]
    lora = xs[:, 3 * W:3 * W + LANES]
    xg = xs[:, 3 * W + LANES:3 * W + 2 * LANES]
    z = w0_ref[...] + jnp.dot(jnp.tanh(lora).astype(BF16), wd_ref[...], preferred_element_type=F32)
    nz = -z
    softplus = jnp.maximum(nz, 0.0) + jnp.log(1.0 + jnp.exp(-jnp.abs(nz)))
    decay = jnp.exp(-jnp.exp(-softplus - 0.5))
    a = jax.nn.sigmoid(a0_ref[...] + jnp.dot(lora.astype(BF16), wa_ref[...], preferred_element_type=F32))
    g = jnp.dot(jax.nn.sigmoid(xg).astype(BF16), wg_ref[...], preferred_element_type=F32)
    kk = k * kk_ref[...]
    kk = kk * lax.rsqrt(jnp.maximum(_seg_sum(kk * kk, RWKV_HEAD_DIM), 1e-24))
    k_h = k * (1.0 + (a - 1.0) * ka_ref[...])
    r_o[...] = r
    d_o[...] = decay
    k_o[...] = k_h
    v_o[...] = v
    kk_o[...] = kk
    b_o[...] = kk * a
    g_o[...] = g
    bonus_o[...] = _seg_sum(r * k_h * rk_ref[...], RWKV_HEAD_DIM) * v


def _rwkv_pre(p, shift0, rw, *, n, t, tm):
    m = n * t
    assert tm % 8 == 0 and t % tm == 0
    tiles = t // tm
    row = lambda w: pl.BlockSpec((1, w), lambda b, i: (0, 0))
    mat = lambda a, c: pl.BlockSpec((a, c), lambda b, i: (0, 0))
    out = pl.BlockSpec((tm, RWKV_W), lambda b, i: (b * tiles + i, 0))
    return pl.pallas_call(
        _rwkv_pre_kernel,
        grid=(n, tiles),
        in_specs=[
            pl.BlockSpec((tm, RWKV_COLS), lambda b, i: (b * tiles + i, 0)),
            pl.BlockSpec((8, RWKV_COLS), lambda b, i: (jnp.maximum((b * tiles + i) * (tm // 8) - 1, 0), 0)),
            pl.BlockSpec((1, 1, RWKV_COLS), lambda b, i: (b, 0, 0)),
            row(RWKV_COLS), row(RWKV_W), mat(LANES, RWKV_W), row(RWKV_W), mat(LANES, RWKV_W),
            mat(GATE_LORA, RWKV_W), row(RWKV_W), row(RWKV_W), row(RWKV_W),
        ],
        out_specs=[out] * 8,
        out_shape=[jax.ShapeDtypeStruct((m, RWKV_W), F32)] * 8,
        compiler_params=_cparams(("parallel", "arbitrary")),
        name="rwkv_pre",
    )(p, p, shift0.reshape(n, 1, RWKV_COLS), *rw)


RW_J = RWKV_HEAD_DIM // 2
RW_PAIRS = LANES // 2


def _rwkv_scan_kernel(r_ref, d_ref, k_ref, kk_ref, b_ref, v_ref, s0_ref, o_ref, sout_ref, s_ref, *, tc):
    c = pl.program_id(1)

    @pl.when(c == 0)
    def _():
        s_ref[...] = s0_ref[0]

    def both_halves(x):
        return x + pltpu.roll(x, RW_PAIRS, axis=1)

    tiles = [pl.ds(k * SUBLANES, SUBLANES) for k in range(RWKV_HEAD_DIM // SUBLANES)]

    def key_row(ref, t, j):
        return jnp.broadcast_to(ref[0, t, pl.ds(j, 1), :], (SUBLANES, LANES))

    def step(t, sa_parts):
        t_next = jnp.minimum(t + 1, tc - 1)
        sa_next = []
        for i, rows in enumerate(tiles):
            u = -both_halves(sa_parts[i])
            vt = v_ref[0, t, rows, :]
            o = jnp.zeros((SUBLANES, LANES), F32)
            sa = jnp.zeros((SUBLANES, LANES), F32)
            for j in range(RW_J):
                h = (s_ref[j, rows, :] * key_row(d_ref, t, j) + u * key_row(b_ref, t, j)
                     + vt * key_row(k_ref, t, j))
                s_ref[j, rows, :] = h
                o = o + h * key_row(r_ref, t, j)
                sa = sa + h * key_row(kk_ref, t_next, j)
            o_ref[0, t, rows, :] = both_halves(o)
            sa_next.append(sa)
        return tuple(sa_next)

    sa0 = []
    for rows in tiles:
        sa = jnp.zeros((SUBLANES, LANES), F32)
        for j in range(RW_J):
            sa = sa + s_ref[j, rows, :] * key_row(kk_ref, 0, j)
        sa0.append(sa)
    lax.fori_loop(0, tc, step, tuple(sa0))
    sout_ref[0] = s_ref[...]


def _rwkv_scan(r, d, k, kk, b, v, s0, *, tc):
    pg, t = r.shape[:2]
    vec = pl.BlockSpec((1, tc, RW_J, LANES), lambda g, c: (g, c, 0, 0))
    val = pl.BlockSpec((1, tc, RWKV_HEAD_DIM, LANES), lambda g, c: (g, c, 0, 0))
    st = pl.BlockSpec((1, RW_J, RWKV_HEAD_DIM, LANES), lambda g, c: (g, 0, 0, 0))
    return pl.pallas_call(
        functools.partial(_rwkv_scan_kernel, tc=tc),
        grid=(pg, t // tc),
        in_specs=[vec, vec, vec, vec, vec, val, st],
        out_specs=[val, st],
        out_shape=[jax.ShapeDtypeStruct((pg, t, RWKV_HEAD_DIM, LANES), F32),
                   jax.ShapeDtypeStruct((pg, RW_J, RWKV_HEAD_DIM, LANES), F32)],
        scratch_shapes=[pltpu.VMEM((RW_J, RWKV_HEAD_DIM, LANES), F32)],
        compiler_params=_cparams(("parallel", "arbitrary")),
        name="rwkv_scan",
    )(r, d, k, kk, b, v, s0)


def _rwkv_post_kernel(o_ref, bonus_ref, g_ref, lg_ref, lb_ref, out_ref):
    o = o_ref[...]
    inv = 1.0 / RWKV_HEAD_DIM
    mean = _seg_sum(o, RWKV_HEAD_DIM) * inv
    cen = o - mean
    var = _seg_sum(cen * cen, RWKV_HEAD_DIM) * inv
    y = cen * lax.rsqrt(var + LNX_EPS) * lg_ref[...] + lb_ref[...]
    out_ref[...] = (y + bonus_ref[...]) * g_ref[...]


def _rwkv_post(o, bonus, g, lnx_g, lnx_b, *, tm):
    m = o.shape[0]
    blk = pl.BlockSpec((tm, RWKV_W), lambda i: (i, 0))
    row = pl.BlockSpec((1, RWKV_W), lambda i: (0, 0))
    return pl.pallas_call(
        _rwkv_post_kernel,
        grid=(m // tm,),
        in_specs=[blk, blk, blk, row, row],
        out_specs=blk,
        out_shape=jax.ShapeDtypeStruct((m, RWKV_W), F32),
        compiler_params=_cparams(("parallel",)),
        name="rwkv_post",
    )(o, bonus, g, lnx_g, lnx_b)


def _pairs_layout(x, n, t):
    pg = n * RWKV_HEADS // RW_PAIRS
    y = x.reshape(n, t, RWKV_HEADS, 2, RW_J).transpose(1, 4, 3, 0, 2).reshape(t, RW_J, 2, pg, RW_PAIRS)
    return y.transpose(3, 0, 1, 2, 4).reshape(pg, t, RW_J, LANES)


def _pairs_layout_v(x, n, t):
    pg = n * RWKV_HEADS // RW_PAIRS
    y = x.reshape(n, t, RWKV_HEADS, RWKV_HEAD_DIM).transpose(1, 3, 0, 2).reshape(t, RWKV_HEAD_DIM, pg, RW_PAIRS)
    y = y.transpose(2, 0, 1, 3)
    return jnp.concatenate([y, y], axis=-1)


def _pairs_unlayout_v(y, n, t):
    pg = y.shape[0]
    z = y[..., :RW_PAIRS].transpose(1, 2, 0, 3).reshape(t, RWKV_HEAD_DIM, n, RWKV_HEADS)
    return z.transpose(2, 0, 3, 1).reshape(n * t, RWKV_W)


def _state_layout(s0):
    n = s0.shape[0]
    pg = n * RWKV_HEADS // RW_PAIRS
    y = s0.reshape(pg, RW_PAIRS, RWKV_HEAD_DIM, 2, RW_J)
    return y.transpose(0, 4, 2, 3, 1).reshape(pg, RW_J, RWKV_HEAD_DIM, LANES)


def _state_unlayout(y, n):
    pg = y.shape[0]
    z = y.reshape(pg, RW_J, RWKV_HEAD_DIM, 2, RW_PAIRS).transpose(0, 4, 2, 3, 1)
    return z.reshape(n, RWKV_HEADS, RWKV_HEAD_DIM, RWKV_HEAD_DIM)


def _outproj_kernel(x_ref, a_ref, b_ref, wa_ref, wb_ref, o_ref):
    y = jnp.dot(a_ref[...].astype(BF16), wa_ref[...], preferred_element_type=F32)
    y = y + jnp.dot(b_ref[...].astype(BF16), wb_ref[...], preferred_element_type=F32)
    o_ref[...] = x_ref[...] + y


def _outproj(x, a, b, wa, wb, *, tm):
    m = x.shape[0]
    return pl.pallas_call(
        _outproj_kernel,
        grid=(m // tm,),
        in_specs=[
            pl.BlockSpec((tm, D_MODEL), lambda i: (i, 0)),
            pl.BlockSpec((tm, a.shape[1]), lambda i: (i, 0)),
            pl.BlockSpec((tm, b.shape[1]), lambda i: (i, 0)),
            pl.BlockSpec(wa.shape, lambda i: (0, 0)),
            pl.BlockSpec(wb.shape, lambda i: (0, 0)),
        ],
        out_specs=pl.BlockSpec((tm, D_MODEL), lambda i: (i, 0)),
        out_shape=jax.ShapeDtypeStruct((m, D_MODEL), F32),
        compiler_params=_cparams(("parallel",)),
        name="outproj",
    )(x, a, b, wa, wb)


def _memkv_kernel(x_ref, g_ref, w_ref, kg_ref, o_ref):
    xn = _rms(x_ref[...], g_ref[...]).astype(BF16)
    kv = jnp.dot(xn, w_ref[...], preferred_element_type=F32)
    xw = X_HEADS * X_HEAD_DIM
    o_ref[:, 0:xw] = _seg_rms(kv[:, 0:xw], kg_ref[...], X_HEAD_DIM)
    o_ref[:, xw:2 * xw] = kv[:, xw:2 * xw]


def _memkv(mem, g, w, kg, *, tm):
    m = mem.shape[0]
    xw = X_HEADS * X_HEAD_DIM
    return pl.pallas_call(
        _memkv_kernel,
        grid=(m // tm,),
        in_specs=[
            pl.BlockSpec((tm, D_MODEL), lambda i: (i, 0)),
            pl.BlockSpec((1, D_MODEL), lambda i: (0, 0)),
            pl.BlockSpec((D_MODEL, 2 * xw), lambda i: (0, 0)),
            pl.BlockSpec((1, xw), lambda i: (0, 0)),
        ],
        out_specs=pl.BlockSpec((tm, 2 * xw), lambda i: (i, 0)),
        out_shape=jax.ShapeDtypeStruct((m, 2 * xw), F32),
        compiler_params=_cparams(("parallel",)),
        name="memkv",
    )(mem, g, w, kg)


def _xattn_kernel(x_ref, mkv_ref, g_ref, wq_ref, qg_ref, wo_ref, o_ref):
    x = x_ref[0]
    xn = _rms(x, g_ref[...]).astype(BF16)
    q = jnp.dot(xn, wq_ref[...], preferred_element_type=F32)
    q = _seg_rms(q, qg_ref[...], X_HEAD_DIM) * (X_HEAD_DIM ** -0.5)
    xw = X_HEADS * X_HEAD_DIM
    outs = []
    for h in range(X_HEADS):
        lo = h * X_HEAD_DIM
        k = mkv_ref[0, :, lo:lo + X_HEAD_DIM].astype(BF16)
        v = mkv_ref[0, :, xw + lo:xw + lo + X_HEAD_DIM].astype(BF16)
        s = lax.dot_general(q[:, lo:lo + X_HEAD_DIM].astype(BF16), k, (((1,), (1,)), ((), ())),
                            preferred_element_type=F32)
        e = jnp.exp(s - jnp.max(s, axis=-1, keepdims=True))
        p = e / jnp.sum(e, axis=-1, keepdims=True)
        outs.append(jnp.dot(p.astype(BF16), v, preferred_element_type=F32))
    o = jnp.concatenate(outs, axis=-1).astype(BF16)
    o_ref[0] = x + jnp.dot(o, wo_ref[...], preferred_element_type=F32)


def _xattn(x, mkv, g, wq, qg, wo, *, tm):
    n, t, _ = x.shape
    xw = X_HEADS * X_HEAD_DIM
    n_mem = mkv.shape[1]
    return pl.pallas_call(
        _xattn_kernel,
        grid=(n, t // tm),
        in_specs=[
            pl.BlockSpec((1, tm, D_MODEL), lambda b, i: (b, i, 0)),
            pl.BlockSpec((1, n_mem, 2 * xw), lambda b, i: (b, 0, 0)),
            pl.BlockSpec((1, D_MODEL), lambda b, i: (0, 0)),
            pl.BlockSpec((D_MODEL, xw), lambda b, i: (0, 0)),
            pl.BlockSpec((1, xw), lambda b, i: (0, 0)),
            pl.BlockSpec((xw, D_MODEL), lambda b, i: (0, 0)),
        ],
        out_specs=pl.BlockSpec((1, tm, D_MODEL), lambda b, i: (b, i, 0)),
        out_shape=jax.ShapeDtypeStruct((n, t, D_MODEL), F32),
        compiler_params=_cparams(("parallel", "parallel")),
        name="xattn",
    )(x, mkv, g, wq, qg, wo)


def _row(v):
    return v.reshape(1, -1).astype(F32)


def _block_diag2(w):
    z = jnp.zeros_like(w)
    return jnp.concatenate([jnp.concatenate([w, z], axis=-1), jnp.concatenate([z, w], axis=-1)], axis=-2)


def _prep_weights(norm_ffn1_g, w_ffn1_gu, w_ffn1_down, norm_mix_g, w_in, w_out, q_norm_g, kc_norm_g, ks_norm_g,
                  kw_norm_g, gate_b, cmp_pe_k, cmp_pe_v, w_cmp_k1, w_cmp_k2, w_cmp_v1, w_cmp_v2, rwkv_mu, rwkv_w0,
                  w_decay2, rwkv_a0, w_iclr2, w_gate2, rwkv_k_k, rwkv_k_a, rwkv_r_k, lnx_g, lnx_b, norm_x_g,
                  norm_mem_g, w_xq, w_xkv, xq_norm_g, xk_norm_g, w_xo, norm_ffn2_g, w_ffn2_gu, w_ffn2_down,
                  norm_out_g):
    G = NSA_KV_HEADS
    W = {}
    W["ffn1"] = (_row(norm_ffn1_g), w_ffn1_gu.astype(BF16), w_ffn1_down.astype(BF16))
    W["ffn2"] = (_row(norm_ffn2_g), w_ffn2_gu.astype(BF16), w_ffn2_down.astype(BF16))
    W["norm_out_g"] = _row(norm_out_g)
    w_pad = jnp.concatenate([w_in[:, :NSA_COLS], jnp.zeros((D_MODEL, NSA_PAD - NSA_COLS), F32), w_in[:, NSA_COLS:]],
                            axis=1).astype(BF16)
    head_g = jnp.zeros((NSA_PAD,), F32)
    head_g = head_g.at[0:NSA_Q_W].set(jnp.tile(q_norm_g, NSA_HEADS))
    c = NSA_Q_W + 2 * NSA_KV_W
    head_g = head_g.at[c:c + NSA_KV_W].set(jnp.tile(ks_norm_g, G))
    c = NSA_Q_W + 4 * NSA_KV_W
    head_g = head_g.at[c:c + NSA_KV_W].set(jnp.tile(kw_norm_g, G))
    gb = jnp.zeros((LANES,), F32).at[0:3 * NSA_HEADS].set(gate_b)
    W["proj"] = (_row(norm_mix_g), w_pad, _row(head_g), _row(gb))
    pe = jnp.concatenate([cmp_pe_k, cmp_pe_k, cmp_pe_v, cmp_pe_v], axis=-1)[:, None, :]
    W["cmp"] = (pe, _block_diag2(w_cmp_k1).astype(BF16), _block_diag2(w_cmp_v1).astype(BF16),
                _block_diag2(w_cmp_k2).astype(BF16), _block_diag2(w_cmp_v2).astype(BF16),
                _row(jnp.tile(kc_norm_g, G)))
    pe_t = jnp.concatenate([jnp.tile(cmp_pe_k.T, (1, 2)), jnp.tile(cmp_pe_v.T, (1, 2))], axis=0)[:, None, :]
    w1_t = jnp.stack([_block_diag2(w_cmp_k1.transpose(1, 0, 2)), _block_diag2(w_cmp_v1.transpose(1, 0, 2))])
    w2_t = jnp.stack([_block_diag2(w_cmp_k2), _block_diag2(w_cmp_v2)])
    W["cmp_paged"] = (pe_t, w1_t.astype(BF16), w2_t.astype(BF16), _row(jnp.tile(kc_norm_g, G)))
    zl = jnp.zeros((DECAY_LORA, RWKV_W), F32)
    W["rwkv_pre"] = (_row(rwkv_mu), _row(rwkv_w0), jnp.concatenate([w_decay2, zl], axis=0).astype(BF16),
                     _row(rwkv_a0), jnp.concatenate([zl, w_iclr2], axis=0).astype(BF16), w_gate2.astype(BF16),
                     _row(rwkv_k_k), _row(rwkv_k_a), _row(rwkv_r_k))
    W["rwkv_post"] = (_row(lnx_g), _row(lnx_b))
    W["w_out"] = (w_out[:NSA_Q_W].astype(BF16), w_out[NSA_Q_W:].astype(BF16))
    W["xattn"] = (_row(norm_x_g), w_xq.astype(BF16), _row(jnp.tile(xq_norm_g, X_HEADS)), w_xo.astype(BF16))
    W["memkv"] = (_row(norm_mem_g), w_xkv.astype(BF16), _row(jnp.tile(xk_norm_g, X_HEADS)))
    return W


def _tile_rows(m, pref):
    return pref if m % pref == 0 else m


def _rwkv_group(pr, shift0, s0, W, n, t):
    m = n * t
    r, d, k, v, kk, b, g, bonus = _rwkv_pre(pr, shift0, W["rwkv_pre"], n=n, t=t, tm=_tile_rows(t, 256))
    lay = lambda a: _pairs_layout(a, n, t)
    o, s_new = _rwkv_scan(lay(r), lay(d), lay(k), lay(kk), lay(b), _pairs_layout_v(v, n, t), _state_layout(s0),
                          tc=min(t, 32))
    o = _pairs_unlayout_v(o, n, t)
    o = _rwkv_post(o, bonus, g, *W["rwkv_post"], tm=_tile_rows(m, 512))
    return o, _state_unlayout(s_new, n)


def _nsa_prompt_group(pn, W, n, t):
    G = NSA_KV_HEADS
    n_blk = t // CMP_BLOCK
    kc, vc = _compress_rows(pn, NSA_Q_W // LANES, W["cmp"], nblk=n_blk)
    pn3 = pn.reshape(n, t, NSA_PAD)
    col = lambda i: pn3[:, :, NSA_Q_W + i * NSA_KV_W:NSA_Q_W + (i + 1) * NSA_KV_W]
    pos = jnp.arange(t, dtype=jnp.int32)
    ksp = _key_features(col(2), pos)
    kwp = _key_features(col(4), pos)
    end = (jnp.arange(n_blk, dtype=jnp.int32) + 1) * CMP_BLOCK - 1
    kcp = _key_features(kc.reshape(n, n_blk, NSA_KV_W), end)
    vct = vc.reshape(n, n_blk, G, HEAD_DIM).transpose(0, 2, 3, 1).astype(BF16)
    return _nsa_prompt(pn, kcp, vct, ksp, kwp, n=n, t=t)


def _pad_axis(a, axis, size):
    pad = [(0, 0)] * a.ndim
    pad[axis] = (0, size - a.shape[axis])
    return jnp.pad(a, pad)


def _nsa_sample_group(pn, W, n, t_new, cache_kv, page_table, cache_win):
    G = NSA_KV_HEADS
    assert t_new <= CMP_BLOCK and t_new <= TQ
    past = page_table.shape[1] * PAGE_SIZE
    nb_past = past // CMP_BLOCK
    cache_t = cache_kv.transpose(0, 2, 3, 4, 1).reshape(cache_kv.shape[0], 4 * G * HEAD_DIM, PAGE_SIZE)
    win_t = cache_win.transpose(0, 2, 3, 4, 1)
    kc_past, vc_past = _compress_paged(page_table, cache_t, W["cmp_paged"])
    per_block = lambda a: a.reshape(n, G, nb_past, HEAD_DIM).transpose(0, 2, 1, 3).reshape(n, nb_past, NSA_KV_W)
    kc_past, vc_past = per_block(kc_past), per_block(vc_past)
    pn3 = pn.reshape(n, t_new, NSA_PAD)
    col = lambda i: pn3[:, :, NSA_Q_W + i * NSA_KV_W:NSA_Q_W + (i + 1) * NSA_KV_W]
    new_rows = _pad_axis(pn3[:, :, NSA_Q_W:NSA_Q_W + 2 * NSA_KV_W], 1, CMP_BLOCK).reshape(n * CMP_BLOCK, 2 * NSA_KV_W)
    kc_new, vc_new = _compress_rows(new_rows, 0, W["cmp"], nblk=n)
    nb_pad = -(-(nb_past + 1) // 16) * 16
    kc = _pad_axis(jnp.concatenate([kc_past, kc_new[:, None]], axis=1), 1, nb_pad)
    vc = _pad_axis(jnp.concatenate([vc_past, vc_new[:, None]], axis=1), 1, nb_pad)
    end_rel = (jnp.arange(nb_pad, dtype=jnp.int32) + 1) * CMP_BLOCK - 1 - past
    kcp = _key_features(kc, end_rel)
    vct = vc.reshape(n, nb_pad, G, HEAD_DIM).transpose(0, 2, 3, 1).astype(BF16)
    heads = lambda a: a.reshape(n, t_new, G, HEAD_DIM).transpose(0, 2, 1, 3)
    new_k = lambda i: _pad_axis(heads(col(i)), 2, KC).astype(BF16)
    new_vt = lambda i: _pad_axis(heads(col(i)).transpose(0, 1, 3, 2), 3, KC).astype(BF16)
    n_real = NSA_REP * t_new
    qt = _query_features(pn3[:, :, :NSA_Q_W]).reshape(n, t_new, G, NSA_REP, LANES)
    qt = _pad_axis(qt.transpose(0, 2, 4, 3, 1).reshape(n, G, LANES, n_real), 3, LANES)
    gates = pn3[:, :, NSA_Q_W + 6 * NSA_KV_W:NSA_COLS].reshape(n, t_new, G, NSA_REP, 3)
    gates_t = _pad_axis(gates.transpose(0, 2, 4, 3, 1).reshape(n, G, 3, n_real), 3, LANES)
    o_c, imp = _nsa_sample_cmp(qt, kcp, vct, past=past, n_new=t_new)
    imp_all = imp[:, :, :, :t_new].transpose(1, 2, 0, 3).reshape(G, nb_pad, n * t_new)
    selb = _nsa_rank(imp_all, past=past, n_new=t_new).reshape(G, nb_pad, n, t_new).transpose(2, 0, 1, 3)
    selb = jnp.pad(jnp.tile(selb, (1, 1, 1, NSA_REP)), ((0, 0),) * 3 + ((0, LANES - n_real),),
                   constant_values=-jnp.inf)
    o_t = _nsa_sample(page_table, qt, o_c, selb, win_t, new_k(2), new_vt(3), new_k(4), new_vt(5), gates_t, cache_t,
                      past=past, n_new=t_new)
    o = o_t[:, :, :, :n_real].reshape(n, G, HEAD_DIM, NSA_REP, t_new)
    return o.transpose(0, 4, 1, 3, 2).reshape(n * t_new, NSA_Q_W)


def _layer(x, mkv, shift0, s0, W, nsa_fn):
    n, t, _ = x.shape
    m = n * t
    G = NSA_KV_HEADS
    x2 = x.reshape(m, D_MODEL)
    tm = _tile_rows(m, 512)
    x2 = _ffn(x2, *W["ffn1"], W["norm_out_g"], final_norm=False, tm=tm, tf=D_FF // 2)
    pn, pr = _proj(x2, *W["proj"], tm=_tile_rows(m, 256))
    o_nsa = nsa_fn(pn)
    o_rwkv, s_new = _rwkv_group(pr, shift0, s0, W, n, t)
    x2 = _outproj(x2, o_nsa, o_rwkv, *W["w_out"], tm=tm)
    x3 = _xattn(x2.reshape(n, t, D_MODEL), mkv, *W["xattn"], tm=_tile_rows(t, 512))
    y = _ffn(x3.reshape(m, D_MODEL), *W["ffn2"], W["norm_out_g"], final_norm=True, tm=tm, tf=D_FF // 2)
    pn3 = pn.reshape(n, t, NSA_PAD)
    kv_rows = pn3[:, :, NSA_Q_W:NSA_Q_W + 4 * NSA_KV_W].reshape(n, t, 4, G, HEAD_DIM)
    win_new = pn3[:, :, NSA_Q_W + 4 * NSA_KV_W:NSA_Q_W + 6 * NSA_KV_W].reshape(n, t, 2, G, HEAD_DIM)
    shift_new = pr.reshape(n, t, RWKV_COLS)[:, -1]
    return y.reshape(n, t, D_MODEL), kv_rows, win_new, s_new, shift_new


def kernel(x_prompt, x_sample, cache_nsa_kv, cache_nsa_win, state_rwkv_s, state_rwkv_shift, cache_mem_kv, page_table, mem_prompt, norm_ffn1_g, w_ffn1_gu, w_ffn1_down, norm_mix_g, w_in, w_out, q_norm_g, kc_norm_g, ks_norm_g, kw_norm_g, gate_b, cmp_pe_k, cmp_pe_v, w_cmp_k1, w_cmp_k2, w_cmp_v1, w_cmp_v2, rwkv_mu, rwkv_w0, w_decay2, rwkv_a0, w_iclr2, w_gate2, rwkv_k_k, rwkv_k_a, rwkv_r_k, lnx_g, lnx_b, norm_x_g, norm_mem_g, w_xq, w_xkv, xq_norm_g, xk_norm_g, w_xo, norm_ffn2_g, w_ffn2_gu, w_ffn2_down, norm_out_g):
    layer_weights = (norm_ffn1_g, w_ffn1_gu, w_ffn1_down, norm_mix_g, w_in, w_out, q_norm_g, kc_norm_g, ks_norm_g,
                     kw_norm_g, gate_b, cmp_pe_k, cmp_pe_v, w_cmp_k1, w_cmp_k2, w_cmp_v1, w_cmp_v2, rwkv_mu, rwkv_w0,
                     w_decay2, rwkv_a0, w_iclr2, w_gate2, rwkv_k_k, rwkv_k_a, rwkv_r_k, lnx_g, lnx_b, norm_x_g,
                     norm_mem_g, w_xq, w_xkv, xq_norm_g, xk_norm_g, w_xo, norm_ffn2_g, w_ffn2_gu, w_ffn2_down,
                     norm_out_g)
    assert w_in.shape[0] == 1, "single-layer trunk"
    W = _prep_weights(*(w[0] for w in layer_weights))
    n_p, t_p, _ = x_prompt.shape
    n_s, t_s, _ = x_sample.shape
    n_mem = mem_prompt.shape[1]
    xw = X_HEADS * X_HEAD_DIM

    mkv_p = _memkv(mem_prompt.reshape(n_p * n_mem, D_MODEL), *W["memkv"], tm=_tile_rows(n_p * n_mem, 512))
    mkv_p = mkv_p.reshape(n_p, n_mem, 2 * xw)
    y_p, kv_p, win_p, rs_p, sh_p = _layer(
        x_prompt, mkv_p, jnp.zeros((n_p, RWKV_COLS), F32),
        jnp.zeros((n_p, RWKV_HEADS, RWKV_HEAD_DIM, RWKV_HEAD_DIM), F32), W,
        lambda pn: _nsa_prompt_group(pn, W, n_p, t_p))
    win_p = win_p[:, t_p - min(WINDOW, t_p):]

    mkv_s = cache_mem_kv[0].reshape(n_s, n_mem, 2 * xw)
    y_s, kv_s, win_new, rs_s, sh_s = _layer(
        x_sample, mkv_s, state_rwkv_shift[0], state_rwkv_s[0], W,
        lambda pn: _nsa_sample_group(pn, W, n_s, t_s, cache_nsa_kv[0], page_table, cache_nsa_win[0]))
    win_s = jnp.concatenate([cache_nsa_win[0], win_new], axis=1)[:, t_s:]

    mkv_out = mkv_p.reshape(1, n_p, n_mem, 2, X_HEADS, X_HEAD_DIM)
    return (y_p, y_s, kv_p[None], kv_s[None], win_p[None], win_s[None], rs_p[None], rs_s[None], sh_p[None],
            sh_s[None], mkv_out)
```

```python
import functools
import math

import jax
import jax.numpy as jnp
from jax import lax
from jax.experimental import pallas as pl
from jax.experimental.pallas import tpu as pltpu

F32 = jnp.float32
BF16 = jnp.bfloat16

D_MODEL = 1024
NSA_HEADS = 8
NSA_KV_HEADS = 2
NSA_REP = NSA_HEADS // NSA_KV_HEADS
HEAD_DIM = 64
CMP_BLOCK = 64
N_SEL = 16
WINDOW = 512
CMP_HIDDEN = 128
PAGE_SIZE = 128
RWKV_HEADS = 8
RWKV_HEAD_DIM = 64
RWKV_W = RWKV_HEADS * RWKV_HEAD_DIM
DECAY_LORA = 64
ICLR_LORA = 64
GATE_LORA = 128
NSA_Q_W = NSA_HEADS * HEAD_DIM
NSA_KV_W = NSA_KV_HEADS * HEAD_DIM
NSA_COLS = NSA_Q_W + 6 * NSA_KV_W + 3 * NSA_HEADS
RWKV_COLS = 3 * RWKV_W + DECAY_LORA + ICLR_LORA + GATE_LORA
X_HEADS = 4
X_HEAD_DIM = 128
D_FF = 2816
RMS_EPS = 1e-6
LNX_EPS = 64e-5
FORCED_SCORE = 1e9

LANES = 128
SUBLANES = 8
NSA_PAD = 1408
P_COLS = NSA_PAD + RWKV_COLS
TQ = 128
KC = 128
VMEM_LIMIT = 56 * 1024 * 1024

ALIBI = tuple(2.0 ** (-8.0 * (h + 1.0) / NSA_HEADS) for h in range(NSA_HEADS))


def _cparams(sem):
    return pltpu.CompilerParams(dimension_semantics=sem, vmem_limit_bytes=VMEM_LIMIT)


def _rms(x, g):
    return x * lax.rsqrt(jnp.mean(x * x, axis=-1, keepdims=True) + RMS_EPS) * g


def _seg_ones(width, seg):
    r = lax.broadcasted_iota(jnp.int32, (width, width), 0) // seg
    c = lax.broadcasted_iota(jnp.int32, (width, width), 1) // seg
    return (r == c).astype(F32)


def _seg_sum(x, seg):
    ones = _seg_ones(LANES, seg)
    parts = [jnp.dot(x[:, c:c + LANES], ones, precision=lax.Precision.HIGHEST, preferred_element_type=F32)
             for c in range(0, x.shape[1], LANES)]
    return parts[0] if len(parts) == 1 else jnp.concatenate(parts, axis=1)


def _seg_rms(x, g, seg):
    if seg == LANES:
        parts = [_rms(x[:, c:c + LANES], g[:, c:c + LANES]) for c in range(0, x.shape[1], LANES)]
        return parts[0] if len(parts) == 1 else jnp.concatenate(parts, axis=1)
    return x * lax.rsqrt(_seg_sum(x * x, seg) * (1.0 / seg) + RMS_EPS) * g


def _ffn_kernel(x_ref, g_ref, wg_ref, wu_ref, wd_ref, go_ref, o_ref, xn_ref, acc_ref, *, final_norm):
    f = pl.program_id(1)

    @pl.when(f == 0)
    def _():
        xn_ref[...] = _rms(x_ref[...], g_ref[...]).astype(BF16)
        acc_ref[...] = jnp.zeros_like(acc_ref)

    xn = xn_ref[...]
    gate = jnp.dot(xn, wg_ref[...], preferred_element_type=F32)
    up = jnp.dot(xn, wu_ref[...], preferred_element_type=F32)
    h = (gate * jax.nn.sigmoid(gate) * up).astype(BF16)
    acc_ref[...] += jnp.dot(h, wd_ref[...], preferred_element_type=F32)

    @pl.when(f == pl.num_programs(1) - 1)
    def _():
        y = x_ref[...] + 0.5 * acc_ref[...]
        if final_norm:
            y = _rms(y, go_ref[...])
        o_ref[...] = y


def _ffn(x, g, w_gu, w_down, g_out, *, final_norm, tm, tf):
    m = x.shape[0]
    nf = D_FF // tf
    return pl.pallas_call(
        functools.partial(_ffn_kernel, final_norm=final_norm),
        grid=(m // tm, nf),
        in_specs=[
            pl.BlockSpec((tm, D_MODEL), lambda i, f: (i, 0)),
            pl.BlockSpec((1, D_MODEL), lambda i, f: (0, 0)),
            pl.BlockSpec((D_MODEL, tf), lambda i, f: (0, f)),
            pl.BlockSpec((D_MODEL, tf), lambda i, f: (0, nf + f)),
            pl.BlockSpec((tf, D_MODEL), lambda i, f: (f, 0)),
            pl.BlockSpec((1, D_MODEL), lambda i, f: (0, 0)),
        ],
        out_specs=pl.BlockSpec((tm, D_MODEL), lambda i, f: (i, 0)),
        out_shape=jax.ShapeDtypeStruct((m, D_MODEL), F32),
        scratch_shapes=[pltpu.VMEM((tm, D_MODEL), BF16), pltpu.VMEM((tm, D_MODEL), F32)],
        compiler_params=_cparams(("parallel", "arbitrary")),
        name="ffn",
    )(x, g, w_gu, w_gu, w_down, g_out)


def _proj_kernel(x_ref, g_ref, w_ref, hg_ref, gb_ref, on_ref, or_ref):
    xn = _rms(x_ref[...], g_ref[...]).astype(BF16)
    p = jnp.dot(xn, w_ref[...], preferred_element_type=F32)
    or_ref[...] = p[:, NSA_PAD:]
    on_ref[...] = p[:, 0:NSA_PAD]
    hg = hg_ref[...]
    on_ref[:, 0:NSA_Q_W] = _seg_rms(p[:, 0:NSA_Q_W], hg[:, 0:NSA_Q_W], HEAD_DIM)
    for c in (NSA_Q_W + 2 * NSA_KV_W, NSA_Q_W + 4 * NSA_KV_W):
        on_ref[:, c:c + NSA_KV_W] = _seg_rms(p[:, c:c + NSA_KV_W], hg[:, c:c + NSA_KV_W], HEAD_DIM)
    c = NSA_Q_W + 6 * NSA_KV_W
    on_ref[:, c:c + LANES] = jax.nn.sigmoid(p[:, c:c + LANES] + gb_ref[...])


def _proj(x, g, w_pad, head_g, gate_b, *, tm):
    m = x.shape[0]
    return pl.pallas_call(
        _proj_kernel,
        grid=(m // tm,),
        in_specs=[
            pl.BlockSpec((tm, D_MODEL), lambda i: (i, 0)),
            pl.BlockSpec((1, D_MODEL), lambda i: (0, 0)),
            pl.BlockSpec((D_MODEL, P_COLS), lambda i: (0, 0)),
            pl.BlockSpec((1, NSA_PAD), lambda i: (0, 0)),
            pl.BlockSpec((1, LANES), lambda i: (0, 0)),
        ],
        out_specs=[pl.BlockSpec((tm, NSA_PAD), lambda i: (i, 0)), pl.BlockSpec((tm, RWKV_COLS), lambda i: (i, 0))],
        out_shape=[jax.ShapeDtypeStruct((m, NSA_PAD), F32), jax.ShapeDtypeStruct((m, RWKV_COLS), F32)],
        compiler_params=_cparams(("parallel",)),
        name="proj",
    )(x, g, w_pad, head_g, gate_b)


def _compress_core(xk_ref, xv_ref, nblk, pe_ref, w1k_ref, w1v_ref, w2k_ref, w2v_ref, kcg_ref):
    def body(j, carry):
        acc_k, acc_v = carry
        pe = pe_ref[j]
        xk = xk_ref[pl.ds(j, nblk, stride=CMP_BLOCK), :] + pe[:, 0:LANES]
        xv = xv_ref[pl.ds(j, nblk, stride=CMP_BLOCK), :] + pe[:, LANES:2 * LANES]
        acc_k = acc_k + jnp.dot(xk.astype(BF16), w1k_ref[j], preferred_element_type=F32)
        acc_v = acc_v + jnp.dot(xv.astype(BF16), w1v_ref[j], preferred_element_type=F32)
        return acc_k, acc_v

    zero = jnp.zeros((nblk, 2 * CMP_HIDDEN), F32)
    acc_k, acc_v = lax.fori_loop(0, CMP_BLOCK, body, (zero, zero))
    hk = jax.nn.gelu(acc_k).astype(BF16)
    hv = jax.nn.gelu(acc_v).astype(BF16)
    kc = jnp.dot(hk, w2k_ref[...], preferred_element_type=F32)
    vc = jnp.dot(hv, w2v_ref[...], preferred_element_type=F32)
    return _seg_rms(kc, kcg_ref[...], HEAD_DIM), vc


def _compress_kernel(xk_ref, xv_ref, pe_ref, w1k_ref, w1v_ref, w2k_ref, w2v_ref, kcg_ref, kc_ref, vc_ref, *, nblk):
    kc, vc = _compress_core(xk_ref, xv_ref, nblk, pe_ref, w1k_ref, w1v_ref, w2k_ref, w2v_ref, kcg_ref)
    kc_ref[...] = kc
    vc_ref[...] = vc


def _cmp_weight_specs(imap):
    return [
        pl.BlockSpec((CMP_BLOCK, 1, 2 * LANES), imap(3)),
        pl.BlockSpec((CMP_BLOCK, LANES, 2 * CMP_HIDDEN), imap(3)),
        pl.BlockSpec((CMP_BLOCK, LANES, 2 * CMP_HIDDEN), imap(3)),
        pl.BlockSpec((2 * CMP_HIDDEN, LANES), imap(2)),
        pl.BlockSpec((2 * CMP_HIDDEN, LANES), imap(2)),
        pl.BlockSpec((1, LANES), imap(2)),
    ]


def _compress_rows(rows, col_block, cw, *, nblk):
    m = rows.shape[0]
    steps = m // (nblk * CMP_BLOCK)
    imap = lambda nd: (lambda i: (0,) * nd)
    return pl.pallas_call(
        functools.partial(_compress_kernel, nblk=nblk),
        grid=(steps,),
        in_specs=[pl.BlockSpec((nblk * CMP_BLOCK, LANES), lambda i: (i, col_block)),
                  pl.BlockSpec((nblk * CMP_BLOCK, LANES), lambda i: (i, col_block + 1))] + _cmp_weight_specs(imap),
        out_specs=[pl.BlockSpec((nblk, LANES), lambda i: (i, 0))] * 2,
        out_shape=[jax.ShapeDtypeStruct((steps * nblk, LANES), F32)] * 2,
        compiler_params=_cparams(("parallel",)),
        name="compress",
    )(rows, rows, *cw)


CMP_ROWS = 2 * NSA_KV_HEADS * HEAD_DIM
CMP_FEATS = 8


def _compress_paged_kernel(pt_ref, cache_ref, pe_ref, w1_ref, w2_ref, kcg_ref, kc_ref, vc_ref, buf_ref, sem, *,
                           n_pages):
    n = pl.program_id(0)
    slot = n % 2

    def page_copy(entry, buf_slot, i):
        return pltpu.make_async_copy(cache_ref.at[pt_ref[entry, i], pl.ds(0, CMP_ROWS), :],
                                     buf_ref.at[buf_slot, :, i, :], sem.at[buf_slot])

    def start_entry(entry, buf_slot):
        def start(i, c):
            page_copy(entry, buf_slot, i).start()
            return c
        lax.fori_loop(0, n_pages, start, 0)

    @pl.when(n == 0)
    def _():
        start_entry(0, 0)

    @pl.when(n + 1 < pl.num_programs(0))
    def _():
        start_entry(n + 1, 1 - slot)

    def wait(i, c):
        page_copy(n, slot, i).wait()
        return c

    lax.fori_loop(0, n_pages, wait, 0)
    blocks_per_page = PAGE_SIZE // CMP_BLOCK
    for c, out_ref in enumerate((kc_ref, vc_ref)):
        for g in range(NSA_KV_HEADS):
            def body(it, acc):
                d0 = pl.multiple_of(it * CMP_FEATS, CMP_FEATS)
                x = jnp.concatenate(
                    [buf_ref[slot, (c * NSA_KV_HEADS + g) * HEAD_DIM + d0 + u] + pe_ref[c * HEAD_DIM + d0 + u]
                     for u in range(CMP_FEATS)], axis=1)
                w = w1_ref[c, pl.ds(d0, CMP_FEATS)].reshape(CMP_FEATS * PAGE_SIZE, blocks_per_page * CMP_HIDDEN)
                return acc + jnp.dot(x.astype(BF16), w, preferred_element_type=F32)

            acc = lax.fori_loop(0, HEAD_DIM // CMP_FEATS, body,
                                jnp.zeros((n_pages, blocks_per_page * CMP_HIDDEN), F32))
            out = jnp.dot(jax.nn.gelu(acc).astype(BF16), w2_ref[c], preferred_element_type=F32)
            if c == 0:
                out = _seg_rms(out, kcg_ref[...], HEAD_DIM)
            out_ref[0, g] = out


def _compress_paged(page_table, cache_t, cw):
    nb, n_pages = page_table.shape
    width = (PAGE_SIZE // CMP_BLOCK) * HEAD_DIM
    const = lambda shape: pl.BlockSpec(shape, lambda n, pt: (0,) * len(shape))
    out = pl.BlockSpec((1, NSA_KV_HEADS, n_pages, width), lambda n, pt: (n, 0, 0, 0))
    return pl.pallas_call(
        functools.partial(_compress_paged_kernel, n_pages=n_pages),
        grid_spec=pltpu.PrefetchScalarGridSpec(
            num_scalar_prefetch=1,
            grid=(nb,),
            in_specs=[pl.BlockSpec(memory_space=pl.ANY)] + [const(w.shape) for w in cw],
            out_specs=[out, out],
            scratch_shapes=[pltpu.VMEM((2, CMP_ROWS, n_pages, PAGE_SIZE), F32), pltpu.SemaphoreType.DMA((2,))],
        ),
        out_shape=[jax.ShapeDtypeStruct((nb, NSA_KV_HEADS, n_pages, width), F32)] * 2,
        compiler_params=_cparams(("arbitrary",)),
        name="compress_paged",
    )(page_table, cache_t, *cw)


QW = NSA_REP * TQ
SEL_KC = 512
RANK_UNROLL = 8


def _wide_init(width=QW):
    return (jnp.full((1, width), -jnp.inf, F32), jnp.zeros((1, width), F32), jnp.zeros((HEAD_DIM, width), F32))


def _wide_step(state, s, vt=None, v_rows=None, group=0):
    m, l, acc = state
    m_new = jnp.maximum(m, jnp.max(s, axis=0, keepdims=True))
    m_safe = jnp.where(m_new == -jnp.inf, 0.0, m_new)
    alpha = jnp.exp(m - m_safe)
    p = jnp.exp(s - m_safe)
    l = alpha * l + jnp.sum(p, axis=0, keepdims=True)
    if vt is not None:
        pv = jnp.dot(vt, p.astype(BF16), preferred_element_type=F32)
    else:
        pv = lax.dot_general(v_rows, p.astype(BF16), (((0,), (0,)), ((), ())), preferred_element_type=F32)
        pv = pv[group * HEAD_DIM:(group + 1) * HEAD_DIM, :]
    return m_new, l, alpha * acc + pv


def _wide_out(state):
    _, l, acc = state
    return acc / jnp.maximum(l, 1e-30)


def _cmp_branch(qt, kcp, vct, t_row, n_blk):
    b_col = lax.broadcasted_iota(jnp.int32, (kcp.shape[0], qt.shape[1]), 0)
    valid = (t_row >= (b_col + 1) * CMP_BLOCK - 1) & (b_col < n_blk)
    s = jnp.where(valid, jnp.dot(kcp, qt, preferred_element_type=F32), -jnp.inf)
    m = jnp.max(s, axis=0, keepdims=True)
    e = jnp.exp(s - jnp.where(m == -jnp.inf, 0.0, m))
    p = e / jnp.maximum(jnp.sum(e, axis=0, keepdims=True), 1e-30)
    return jnp.dot(vct, p.astype(BF16), preferred_element_type=F32), p


def _select_blocks(imp, t_row, n_blk, score_ref, unrolled=False):
    nb_pad, w = imp.shape
    b_col = lax.broadcasted_iota(jnp.int32, (nb_pad, w), 0)
    cur = t_row // CMP_BLOCK
    forced = (b_col == 0) | (b_col == cur) | (b_col == cur - 1)
    score = jnp.where(forced, FORCED_SCORE, jnp.where(b_col <= cur, imp, -FORCED_SCORE))
    score = jnp.where(b_col < n_blk, score, -jnp.inf)
    score_ref[...] = score

    if unrolled:
        n_tiles = nb_pad // SUBLANES
        tile = lambda a, k: a[k * SUBLANES:(k + 1) * SUBLANES, :]
        cnt = [jnp.zeros((SUBLANES, w), F32) for _ in range(n_tiles)]
        for bp in range(n_blk):
            row = jnp.broadcast_to(score_ref[pl.ds(bp, 1), :], (SUBLANES, w))
            for k in range(n_tiles):
                sc = tile(score, k)
                if (k + 1) * SUBLANES - 1 < bp:
                    ahead = row > sc
                elif k * SUBLANES > bp:
                    ahead = row >= sc
                else:
                    ahead = (row > sc) | ((row == sc) & (tile(b_col, k) > bp))
                cnt[k] = cnt[k] + jnp.where(ahead, 1.0, 0.0)
        cnt = jnp.concatenate(cnt, axis=0)
        return jnp.where((cnt < N_SEL) & (b_col < n_blk), 0.0, -jnp.inf)

    def rank_body(it, cnt):
        base = pl.multiple_of(it * RANK_UNROLL, RANK_UNROLL)
        for u in range(RANK_UNROLL):
            bp = base + u
            row = jnp.broadcast_to(score_ref[pl.ds(bp, 1), :], (nb_pad, w))
            ahead = (row > score) | ((row == score) & (b_col > bp))
            cnt = cnt + jnp.where(ahead, 1.0, 0.0)
        return cnt

    cnt = lax.fori_loop(0, nb_pad // RANK_UNROLL, rank_body, jnp.zeros((nb_pad, w), F32))
    return jnp.where((cnt < N_SEL) & (b_col < n_blk), 0.0, -jnp.inf)


def _sum_lane_chunks(p):
    imp = p[:, 0:TQ]
    for r in range(1, NSA_REP):
        imp = imp + p[:, r * TQ:(r + 1) * TQ]
    return imp


def _nsa_prompt_kernel(q_ref, gate_ref, kcp_ref, vct_ref, ksp_ref, vs_ref, kwp_ref, vw_ref, o_ref,
                       score_ref, selb_ref, *, n_blk):
    i = pl.program_id(1)
    t0 = i * TQ
    t_row = t0 + (lax.broadcasted_iota(jnp.int32, (1, QW), 1) & (TQ - 1))
    blocks_per_step = SEL_KC // CMP_BLOCK
    wk = WINDOW + TQ
    ws = pl.multiple_of(jnp.maximum(t0 - WINDOW, 0), TQ)
    dist = t_row - (ws + lax.broadcasted_iota(jnp.int32, (wk, QW), 0))
    wbias = jnp.where((dist >= 0) & (dist < WINDOW), 0.0, -jnp.inf)
    n_steps = (t0 + TQ + SEL_KC - 1) // SEL_KC

    q_t = (q_ref[...] * (HEAD_DIM ** -0.5)).T
    gate_t = gate_ref[...].T
    f_row = lax.broadcasted_iota(jnp.int32, (HEAD_DIM, TQ), 0)
    groups = range(NSA_KV_HEADS)
    qts, o_c = [], []
    for g in groups:
        cols = []
        for r in range(NSA_REP):
            h = g * NSA_REP + r
            feat = jnp.where(f_row == 0, ALIBI[h] * CMP_BLOCK, jnp.where(f_row == 1, ALIBI[h], 0.0))
            cols.append(jnp.concatenate([q_t[h * HEAD_DIM:(h + 1) * HEAD_DIM, :], feat], axis=0))
        qts.append(jnp.concatenate(cols, axis=1).astype(BF16))
        o, p = _cmp_branch(qts[g], kcp_ref[0, g], vct_ref[0, g], t_row, n_blk)
        o_c.append(o)
        selb = _select_blocks(_sum_lane_chunks(p), t_row[:, 0:TQ], n_blk, score_ref, unrolled=True)
        selb_ref[g] = jnp.concatenate([selb] * NSA_REP, axis=1)

    def step_scores(c, g):
        k0 = pl.multiple_of(c * SEL_KC, SEL_KC)
        s = jnp.dot(ksp_ref[0, g, pl.ds(k0, SEL_KC), :], qts[g], preferred_element_type=F32)
        return jnp.concatenate(
            [s[b * CMP_BLOCK:(b + 1) * CMP_BLOCK, :] + selb_ref[g, pl.ds(c * blocks_per_step + b, 1), :]
             for b in range(blocks_per_step)], axis=0)

    def step_values(c):
        return vs_ref[pl.ds(pl.multiple_of(c * SEL_KC, SEL_KC), SEL_KC), :].astype(BF16)

    def sel_step(c, states):
        v = step_values(c)
        return tuple(_wide_step(states[g], step_scores(c, g), v_rows=v, group=g) for g in groups)

    def sel_body(c, states):
        first = pl.multiple_of(c * blocks_per_step, blocks_per_step)
        rows = functools.reduce(jnp.maximum, [selb_ref[g, pl.ds(first, blocks_per_step), 0:TQ] for g in groups])
        return lax.cond(jnp.max(rows) > -1.0, functools.partial(sel_step, c), lambda s: s, states)

    states = lax.fori_loop(0, n_steps - 1, sel_body, tuple(_wide_init() for _ in groups))
    c_last = n_steps - 1
    causal = c_last * SEL_KC + lax.broadcasted_iota(jnp.int32, (SEL_KC, QW), 0) <= t_row
    v = step_values(c_last)
    o_s = [_wide_out(_wide_step(states[g], jnp.where(causal, step_scores(c_last, g), -jnp.inf), v_rows=v, group=g))
           for g in groups]

    v = vw_ref[pl.ds(ws, wk), :].astype(BF16)
    o_w = [_wide_out(_wide_step(
        _wide_init(), jnp.dot(kwp_ref[0, g, pl.ds(ws, wk), :], qts[g], preferred_element_type=F32) + wbias,
        v_rows=v, group=g)) for g in groups]

    outs = []
    for g in groups:
        for r in range(NSA_REP):
            lanes = slice(r * TQ, (r + 1) * TQ)
            row = (g * NSA_REP + r) * 3
            outs.append(gate_t[row:row + 1, :] * o_c[g][:, lanes] + gate_t[row + 1:row + 2, :] * o_s[g][:, lanes]
                        + gate_t[row + 2:row + 3, :] * o_w[g][:, lanes])
    o_ref[...] = jnp.concatenate(outs, axis=0).T


def _nsa_prompt(pn, kcp, vct, ksp, kwp, *, n, t):
    G = NSA_KV_HEADS
    n_tiles = t // TQ
    n_blk = t // CMP_BLOCK
    col_block = lambda c: c // LANES
    full = lambda shape: pl.BlockSpec((1,) + shape, lambda b, i: (b,) + (0,) * len(shape))
    seq_cols = lambda c: pl.BlockSpec((t, LANES), lambda b, i: (b, col_block(c)))
    return pl.pallas_call(
        functools.partial(_nsa_prompt_kernel, n_blk=n_blk),
        grid=(n, n_tiles),
        in_specs=[
            pl.BlockSpec((TQ, NSA_Q_W), lambda b, i: (b * n_tiles + i, 0)),
            pl.BlockSpec((TQ, LANES), lambda b, i: (b * n_tiles + i, col_block(NSA_Q_W + 6 * NSA_KV_W))),
            full((G, n_blk, LANES)),
            full((G, HEAD_DIM, n_blk)),
            full((G, t, LANES)),
            seq_cols(NSA_Q_W + 3 * NSA_KV_W),
            full((G, t, LANES)),
            seq_cols(NSA_Q_W + 5 * NSA_KV_W),
        ],
        out_specs=pl.BlockSpec((TQ, NSA_Q_W), lambda b, i: (b * n_tiles + i, 0)),
        out_shape=jax.ShapeDtypeStruct((n * t, NSA_Q_W), F32),
        scratch_shapes=[pltpu.VMEM((n_blk, TQ), F32), pltpu.VMEM((G, n_blk, QW), F32)],
        compiler_params=_cparams(("parallel", "arbitrary")),
        name="nsa_prompt",
    )(pn, pn, kcp, vct, ksp, pn, kwp, pn)


PAGES_PER_STEP = 4
PAGE_GROUP = 8


SEL_ROWS = 2 * NSA_KV_HEADS * HEAD_DIM


def _nsa_sample_cmp_kernel(qt_ref, kcp_ref, vct_ref, o_ref, imp_ref, *, past, n_new, n_blk):
    n_real = NSA_REP * n_new
    t_row = past + lax.broadcasted_iota(jnp.int32, (1, LANES), 1) % n_new
    li = lax.broadcasted_iota(jnp.int32, (LANES, LANES), 0)
    lj = lax.broadcasted_iota(jnp.int32, (LANES, LANES), 1)
    same_token = ((li % n_new == lj) & (li < n_real)).astype(F32)
    for g in range(NSA_KV_HEADS):
        o, p = _cmp_branch(qt_ref[0, g], kcp_ref[0, g], vct_ref[0, g], t_row, n_blk)
        o_ref[0, g] = o
        imp_ref[0, g] = jnp.dot(p, same_token, precision=lax.Precision.HIGHEST, preferred_element_type=F32)


def _nsa_sample_cmp(qt, kcp, vct, *, past, n_new):
    n, G, nb_pad = kcp.shape[:3]
    full = lambda shape: pl.BlockSpec((1,) + shape, lambda b: (b,) + (0,) * len(shape))
    return pl.pallas_call(
        functools.partial(_nsa_sample_cmp_kernel, past=past, n_new=n_new, n_blk=past // CMP_BLOCK + 1),
        grid=(n,),
        in_specs=[full((G, LANES, LANES)), full((G, nb_pad, LANES)), full((G, HEAD_DIM, nb_pad))],
        out_specs=[full((G, HEAD_DIM, LANES)), full((G, nb_pad, LANES))],
        out_shape=[jax.ShapeDtypeStruct((n, G, HEAD_DIM, LANES), F32),
                   jax.ShapeDtypeStruct((n, G, nb_pad, LANES), F32)],
        compiler_params=_cparams(("parallel",)),
        name="nsa_sample_cmp",
    )(qt, kcp, vct)


def _nsa_rank_kernel(imp_ref, selb_ref, score_ref, *, past, n_new, n_blk):
    t_row = past + lax.broadcasted_iota(jnp.int32, (1, imp_ref.shape[2]), 1) % n_new
    selb_ref[0] = _select_blocks(imp_ref[0], t_row, n_blk, score_ref)


def _nsa_rank(imp, *, past, n_new):
    G, nb_pad, w = imp.shape
    blk = pl.BlockSpec((1, nb_pad, w), lambda g: (g, 0, 0))
    return pl.pallas_call(
        functools.partial(_nsa_rank_kernel, past=past, n_new=n_new, n_blk=past // CMP_BLOCK + 1),
        grid=(G,),
        in_specs=[blk],
        out_specs=blk,
        out_shape=jax.ShapeDtypeStruct((G, nb_pad, w), F32),
        scratch_shapes=[pltpu.VMEM((nb_pad, w), F32)],
        compiler_params=_cparams(("parallel",)),
        name="nsa_rank",
    )(imp)


def _nsa_sample_kernel(pt_ref, qt_ref, oc_ref, selb_ref, win_ref, knew_ref, vnewt_ref, kwnew_ref, vwnewt_ref, gt_ref,
                       cache_ref, o_ref, buf_ref, slot_ref, sem, *, past, n_new, n_pages):
    n = pl.program_id(0)
    G = NSA_KV_HEADS
    n_real = NSA_REP * n_new
    blocks_per_page = PAGE_SIZE // CMP_BLOCK
    nb_past = past // CMP_BLOCK
    w_buf = min(WINDOW, past)
    lane = lax.broadcasted_iota(jnp.int32, (1, LANES), 1)
    t_row = past + lane % n_new
    real = lane < n_real
    selb_ref = selb_ref.at[0]

    def page_copy(lp, slot):
        return pltpu.make_async_copy(cache_ref.at[pt_ref[n, lp], pl.ds(CMP_ROWS, SEL_ROWS), :],
                                     buf_ref.at[pl.ds(slot * SEL_ROWS, SEL_ROWS), :], sem)

    def wanted_rows(first, count):
        rows = jnp.maximum(selb_ref[0, pl.ds(first, count), :], selb_ref[1, pl.ds(first, count), :])
        return jnp.max(jnp.where(real, rows, -jnp.inf)) > -1.0

    def page_body(lp, cnt):
        wanted = wanted_rows(lp * blocks_per_page, 1)
        for b in range(1, blocks_per_page):
            wanted = wanted | wanted_rows(lp * blocks_per_page + b, 1)

        @pl.when(wanted)
        def _():
            page_copy(lp, cnt).start()
            slot_ref[cnt] = lp

        return cnt + wanted.astype(jnp.int32)

    def group_body(pg, cnt):
        first = pl.multiple_of(pg * (PAGE_GROUP * blocks_per_page), PAGE_GROUP * blocks_per_page)
        return lax.cond(wanted_rows(first, PAGE_GROUP * blocks_per_page),
                        lambda c: lax.fori_loop(pg * PAGE_GROUP, (pg + 1) * PAGE_GROUP, page_body, c),
                        lambda c: c, cnt)

    n_slots = lax.fori_loop(0, n_pages // PAGE_GROUP, group_body, jnp.int32(0))
    n_steps = (n_slots + PAGES_PER_STEP - 1) // PAGES_PER_STEP

    def fill_body(j, c):
        slot = n_slots + j

        @pl.when(slot < n_steps * PAGES_PER_STEP)
        def _():
            page_copy(0, slot).start()
            slot_ref[slot] = -1

        return c

    lax.fori_loop(0, PAGES_PER_STEP - 1, fill_body, 0)

    def wait_body(s, c):
        page_copy(jnp.maximum(slot_ref[s], 0), s).wait()
        return c

    lax.fori_loop(0, n_steps * PAGES_PER_STEP, wait_body, 0)

    k_idx = lax.broadcasted_iota(jnp.int32, (PAGE_SIZE, LANES), 0)
    new_idx = lax.broadcasted_iota(jnp.int32, (knew_ref.shape[2], LANES), 0)
    pos_new = past + new_idx
    pos_win = past - w_buf + lax.broadcasted_iota(jnp.int32, (w_buf, LANES), 0)
    tn_dims = (((0,), (0,)), ((), ()))

    for g in range(G):
        q64 = qt_ref[0, g][0:HEAD_DIM, :]
        slope = jnp.zeros((1, LANES), F32)
        for r in range(NSA_REP):
            slope = jnp.where((lane >= r * n_new) & (lane < (r + 1) * n_new), ALIBI[g * NSA_REP + r], slope)

        def alibi(s, pos):
            return s - slope * (t_row - pos).astype(F32)

        def sel_body(st, state):
            kts, vts, pos, bias = [], [], [], []
            for j in range(PAGES_PER_STEP):
                slot = st * PAGES_PER_STEP + j
                base = pl.multiple_of(slot * SEL_ROWS, SEL_ROWS)
                kts.append(buf_ref[pl.ds(base + g * HEAD_DIM, HEAD_DIM), :])
                vts.append(buf_ref[pl.ds(base + (NSA_KV_HEADS + g) * HEAD_DIM, HEAD_DIM), :])
                lp = slot_ref[slot]
                live = jnp.where(lp >= 0, 0.0, -jnp.inf)
                lp = jnp.maximum(lp, 0)
                pos.append(lp * PAGE_SIZE + k_idx)
                bias += [jnp.broadcast_to(selb_ref[g, pl.ds(lp * blocks_per_page + b, 1), :] + live,
                                          (CMP_BLOCK, LANES)) for b in range(blocks_per_page)]
            kt = jnp.concatenate(kts, axis=1).astype(BF16)
            s = lax.dot_general(kt, q64, tn_dims, preferred_element_type=F32)
            s = alibi(s, jnp.concatenate(pos, axis=0)) + jnp.concatenate(bias, axis=0)
            return _wide_step(state, s, jnp.concatenate(vts, axis=1).astype(BF16))

        state = lax.fori_loop(0, n_steps, sel_body, _wide_init(LANES))
        s = alibi(jnp.dot(knew_ref[0, g], q64, preferred_element_type=F32), pos_new)
        s = jnp.where(pos_new <= t_row, s + selb_ref[g, pl.ds(nb_past, 1), :], -jnp.inf)
        o_s = _wide_out(_wide_step(state, s, vnewt_ref[0, g]))

        s = lax.dot_general(win_ref[0, 0, g].astype(BF16), q64, tn_dims, preferred_element_type=F32)
        s = jnp.where(t_row - pos_win < WINDOW, alibi(s, pos_win), -jnp.inf)
        state = _wide_step(_wide_init(LANES), s, win_ref[0, 1, g].astype(BF16))
        s = alibi(jnp.dot(kwnew_ref[0, g], q64, preferred_element_type=F32), pos_new)
        s = jnp.where((pos_new <= t_row) & (new_idx < n_new), s, -jnp.inf)
        o_w = _wide_out(_wide_step(state, s, vwnewt_ref[0, g]))

        o_ref[0, g] = (gt_ref[0, g, pl.ds(0, 1), :] * oc_ref[0, g] + gt_ref[0, g, pl.ds(1, 1), :] * o_s
                       + gt_ref[0, g, pl.ds(2, 1), :] * o_w)


def _nsa_sample(page_table, qt, o_c, selb, win_t, knew, vnewt, kwnew, vwnewt, gates_t, cache_t, *, past, n_new):
    n = qt.shape[0]
    G = NSA_KV_HEADS
    n_pages = page_table.shape[1]
    assert n_pages % PAGE_GROUP == 0
    nb_pad = selb.shape[2]
    w_buf = win_t.shape[4]
    assert w_buf == min(WINDOW, past)
    n_newp = knew.shape[2]
    full = lambda shape: pl.BlockSpec((1,) + shape, lambda b, pt: (b,) + (0,) * len(shape))
    kern = functools.partial(_nsa_sample_kernel, past=past, n_new=n_new, n_pages=n_pages)
    return pl.pallas_call(
        kern,
        grid_spec=pltpu.PrefetchScalarGridSpec(
            num_scalar_prefetch=1,
            grid=(n,),
            in_specs=[
                full((G, LANES, LANES)),
                full((G, HEAD_DIM, LANES)),
                full((G, nb_pad, LANES)),
                full((2, G, HEAD_DIM, w_buf)),
                full((G, n_newp, HEAD_DIM)),
                full((G, HEAD_DIM, n_newp)),
                full((G, n_newp, HEAD_DIM)),
                full((G, HEAD_DIM, n_newp)),
                full((G, 3, LANES)),
                pl.BlockSpec(memory_space=pl.ANY),
            ],
            out_specs=full((G, HEAD_DIM, LANES)),
            scratch_shapes=[
                pltpu.VMEM((n_pages * SEL_ROWS, PAGE_SIZE), F32),
                pltpu.SMEM((n_pages + PAGES_PER_STEP,), jnp.int32),
                pltpu.SemaphoreType.DMA(()),
            ],
        ),
        out_shape=jax.ShapeDtypeStruct((n, G, HEAD_DIM, LANES), F32),
        compiler_params=_cparams(("arbitrary",)),
        name="nsa_sample",
    )(page_table, qt, o_c, selb, win_t, knew, vnewt, kwnew, vwnewt, gates_t, cache_t)


def _alibi_features(pos):
    blk = (pos // CMP_BLOCK).astype(F32)
    off = (pos % CMP_BLOCK).astype(F32)
    pad = jnp.zeros(pos.shape + (HEAD_DIM - 2,), F32)
    return jnp.concatenate([blk[..., None], off[..., None], pad], axis=-1)


def _query_features(q):
    n, t, _ = q.shape
    qh = q.reshape(n, t, NSA_HEADS, HEAD_DIM) * (HEAD_DIM ** -0.5)
    slope = jnp.asarray(ALIBI, F32)
    feat = jnp.zeros((NSA_HEADS, HEAD_DIM), F32).at[:, 0].set(slope * CMP_BLOCK).at[:, 1].set(slope)
    feat = jnp.broadcast_to(feat, (n, t, NSA_HEADS, HEAD_DIM))
    return jnp.concatenate([qh, feat], axis=-1).reshape(n, t, NSA_HEADS * LANES).astype(BF16)


def _key_features(k, pos):
    n, l, _ = k.shape
    kh = k.reshape(n, l, NSA_KV_HEADS, HEAD_DIM).transpose(0, 2, 1, 3)
    feat = jnp.broadcast_to(_alibi_features(pos), (n, NSA_KV_HEADS, l, HEAD_DIM))
    return jnp.concatenate([kh, feat], axis=-1).astype(BF16)


def _rwkv_pre_kernel(p_ref, before_ref, shift_ref, mu_ref, w0_ref, wd_ref, a0_ref, wa_ref, wg_ref, kk_ref, ka_ref,
                     rk_ref, r_o, d_o, k_o, v_o, kk_o, b_o, g_o, bonus_o):
    p = p_ref[...]
    first = jnp.where(pl.program_id(1) == 0, shift_ref[0], before_ref[7:8, :])
    row = lax.broadcasted_iota(jnp.int32, p.shape, 0)
    prev = jnp.where(row == 0, first, pltpu.roll(p, 1, axis=0))
    xs = p + (prev - p) * mu_ref[...]
    W = RWKV_W
    r, k, v = xs[:, 0:W], xs[:, W:2 * W], xs[:, 2 * W:3 * W]
    lora = xs[:, 3 * W:3 * W + LANES]
    xg = xs[:, 3 * W + LANES:3 * W + 2 * LANES]
    z = w0_ref[...] + jnp.dot(jnp.tanh(lora).astype(BF16), wd_ref[...], preferred_element_type=F32)
    nz = -z
    softplus = jnp.maximum(nz, 0.0) + jnp.log(1.0 + jnp.exp(-jnp.abs(nz)))
    decay = jnp.exp(-jnp.exp(-softplus - 0.5))
    a = jax.nn.sigmoid(a0_ref[...] + jnp.dot(lora.astype(BF16), wa_ref[...], preferred_element_type=F32))
    g = jnp.dot(jax.nn.sigmoid(xg).astype(BF16), wg_ref[...], preferred_element_type=F32)
    kk = k * kk_ref[...]
    kk = kk * lax.rsqrt(jnp.maximum(_seg_sum(kk * kk, RWKV_HEAD_DIM), 1e-24))
    k_h = k * (1.0 + (a - 1.0) * ka_ref[...])
    r_o[...] = r
    d_o[...] = decay
    k_o[...] = k_h
    v_o[...] = v
    kk_o[...] = kk
    b_o[...] = kk * a
    g_o[...] = g
    bonus_o[...] = _seg_sum(r * k_h * rk_ref[...], RWKV_HEAD_DIM) * v


def _rwkv_pre(p, shift0, rw, *, n, t, tm):
    m = n * t
    assert tm % 8 == 0 and t % tm == 0
    tiles = t // tm
    row = lambda w: pl.BlockSpec((1, w), lambda b, i: (0, 0))
    mat = lambda a, c: pl.BlockSpec((a, c), lambda b, i: (0, 0))
    out = pl.BlockSpec((tm, RWKV_W), lambda b, i: (b * tiles + i, 0))
    return pl.pallas_call(
        _rwkv_pre_kernel,
        grid=(n, tiles),
        in_specs=[
            pl.BlockSpec((tm, RWKV_COLS), lambda b, i: (b * tiles + i, 0)),
            pl.BlockSpec((8, RWKV_COLS), lambda b, i: (jnp.maximum((b * tiles + i) * (tm // 8) - 1, 0), 0)),
            pl.BlockSpec((1, 1, RWKV_COLS), lambda b, i: (b, 0, 0)),
            row(RWKV_COLS), row(RWKV_W), mat(LANES, RWKV_W), row(RWKV_W), mat(LANES, RWKV_W),
            mat(GATE_LORA, RWKV_W), row(RWKV_W), row(RWKV_W), row(RWKV_W),
        ],
        out_specs=[out] * 8,
        out_shape=[jax.ShapeDtypeStruct((m, RWKV_W), F32)] * 8,
        compiler_params=_cparams(("parallel", "arbitrary")),
        name="rwkv_pre",
    )(p, p, shift0.reshape(n, 1, RWKV_COLS), *rw)


RW_J = RWKV_HEAD_DIM // 2
RW_PAIRS = LANES // 2


def _rwkv_scan_kernel(r_ref, d_ref, k_ref, kk_ref, b_ref, v_ref, s0_ref, o_ref, sout_ref, s_ref, *, tc):
    c = pl.program_id(1)

    @pl.when(c == 0)
    def _():
        s_ref[...] = s0_ref[0]

    def both_halves(x):
        return x + pltpu.roll(x, RW_PAIRS, axis=1)

    tiles = [pl.ds(k * SUBLANES, SUBLANES) for k in range(RWKV_HEAD_DIM // SUBLANES)]

    def key_row(ref, t, j):
        return jnp.broadcast_to(ref[0, t, pl.ds(j, 1), :], (SUBLANES, LANES))

    def step(t, sa_parts):
        t_next = jnp.minimum(t + 1, tc - 1)
        sa_next = []
        for i, rows in enumerate(tiles):
            u = -both_halves(sa_parts[i])
            vt = v_ref[0, t, rows, :]
            o = jnp.zeros((SUBLANES, LANES), F32)
            sa = jnp.zeros((SUBLANES, LANES), F32)
            for j in range(RW_J):
                h = (s_ref[j, rows, :] * key_row(d_ref, t, j) + u * key_row(b_ref, t, j)
                     + vt * key_row(k_ref, t, j))
                s_ref[j, rows, :] = h
                o = o + h * key_row(r_ref, t, j)
                sa = sa + h * key_row(kk_ref, t_next, j)
            o_ref[0, t, rows, :] = both_halves(o)
            sa_next.append(sa)
        return tuple(sa_next)

    sa0 = []
    for rows in tiles:
        sa = jnp.zeros((SUBLANES, LANES), F32)
        for j in range(RW_J):
            sa = sa + s_ref[j, rows, :] * key_row(kk_ref, 0, j)
        sa0.append(sa)
    lax.fori_loop(0, tc, step, tuple(sa0))
    sout_ref[0] = s_ref[...]


def _rwkv_scan(r, d, k, kk, b, v, s0, *, tc):
    pg, t = r.shape[:2]
    vec = pl.BlockSpec((1, tc, RW_J, LANES), lambda g, c: (g, c, 0, 0))
    val = pl.BlockSpec((1, tc, RWKV_HEAD_DIM, LANES), lambda g, c: (g, c, 0, 0))
    st = pl.BlockSpec((1, RW_J, RWKV_HEAD_DIM, LANES), lambda g, c: (g, 0, 0, 0))
    return pl.pallas_call(
        functools.partial(_rwkv_scan_kernel, tc=tc),
        grid=(pg, t // tc),
        in_specs=[vec, vec, vec, vec, vec, val, st],
        out_specs=[val, st],
        out_shape=[jax.ShapeDtypeStruct((pg, t, RWKV_HEAD_DIM, LANES), F32),
                   jax.ShapeDtypeStruct((pg, RW_J, RWKV_HEAD_DIM, LANES), F32)],
        scratch_shapes=[pltpu.VMEM((RW_J, RWKV_HEAD_DIM, LANES), F32)],
        compiler_params=_cparams(("parallel", "arbitrary")),
        name="rwkv_scan",
    )(r, d, k, kk, b, v, s0)


def _rwkv_post_kernel(o_ref, bonus_ref, g_ref, lg_ref, lb_ref, out_ref):
    o = o_ref[...]
    inv = 1.0 / RWKV_HEAD_DIM
    mean = _seg_sum(o, RWKV_HEAD_DIM) * inv
    cen = o - mean
    var = _seg_sum(cen * cen, RWKV_HEAD_DIM) * inv
    y = cen * lax.rsqrt(var + LNX_EPS) * lg_ref[...] + lb_ref[...]
    out_ref[...] = (y + bonus_ref[...]) * g_ref[...]


def _rwkv_post(o, bonus, g, lnx_g, lnx_b, *, tm):
    m = o.shape[0]
    blk = pl.BlockSpec((tm, RWKV_W), lambda i: (i, 0))
    row = pl.BlockSpec((1, RWKV_W), lambda i: (0, 0))
    return pl.pallas_call(
        _rwkv_post_kernel,
        grid=(m // tm,),
        in_specs=[blk, blk, blk, row, row],
        out_specs=blk,
        out_shape=jax.ShapeDtypeStruct((m, RWKV_W), F32),
        compiler_params=_cparams(("parallel",)),
        name="rwkv_post",
    )(o, bonus, g, lnx_g, lnx_b)


def _pairs_layout(x, n, t):
    pg = n * RWKV_HEADS // RW_PAIRS
    y = x.reshape(n, t, RWKV_HEADS, 2, RW_J).transpose(1, 4, 3, 0, 2).reshape(t, RW_J, 2, pg, RW_PAIRS)
    return y.transpose(3, 0, 1, 2, 4).reshape(pg, t, RW_J, LANES)


def _pairs_layout_v(x, n, t):
    pg = n * RWKV_HEADS // RW_PAIRS
    y = x.reshape(n, t, RWKV_HEADS, RWKV_HEAD_DIM).transpose(1, 3, 0, 2).reshape(t, RWKV_HEAD_DIM, pg, RW_PAIRS)
    y = y.transpose(2, 0, 1, 3)
    return jnp.concatenate([y, y], axis=-1)


def _pairs_unlayout_v(y, n, t):
    pg = y.shape[0]
    z = y[..., :RW_PAIRS].transpose(1, 2, 0, 3).reshape(t, RWKV_HEAD_DIM, n, RWKV_HEADS)
    return z.transpose(2, 0, 3, 1).reshape(n * t, RWKV_W)


def _state_layout(s0):
    n = s0.shape[0]
    pg = n * RWKV_HEADS // RW_PAIRS
    y = s0.reshape(pg, RW_PAIRS, RWKV_HEAD_DIM, 2, RW_J)
    return y.transpose(0, 4, 2, 3, 1).reshape(pg, RW_J, RWKV_HEAD_DIM, LANES)


def _state_unlayout(y, n):
    pg = y.shape[0]
    z = y.reshape(pg, RW_J, RWKV_HEAD_DIM, 2, RW_PAIRS).transpose(0, 4, 2, 3, 1)
    return z.reshape(n, RWKV_HEADS, RWKV_HEAD_DIM, RWKV_HEAD_DIM)


def _outproj_kernel(x_ref, a_ref, b_ref, wa_ref, wb_ref, o_ref):
    y = jnp.dot(a_ref[...].astype(BF16), wa_ref[...], preferred_element_type=F32)
    y = y + jnp.dot(b_ref[...].astype(BF16), wb_ref[...], preferred_element_type=F32)
    o_ref[...] = x_ref[...] + y


def _outproj(x, a, b, wa, wb, *, tm):
    m = x.shape[0]
    return pl.pallas_call(
        _outproj_kernel,
        grid=(m // tm,),
        in_specs=[
            pl.BlockSpec((tm, D_MODEL), lambda i: (i, 0)),
            pl.BlockSpec((tm, a.shape[1]), lambda i: (i, 0)),
            pl.BlockSpec((tm, b.shape[1]), lambda i: (i, 0)),
            pl.BlockSpec(wa.shape, lambda i: (0, 0)),
            pl.BlockSpec(wb.shape, lambda i: (0, 0)),
        ],
        out_specs=pl.BlockSpec((tm, D_MODEL), lambda i: (i, 0)),
        out_shape=jax.ShapeDtypeStruct((m, D_MODEL), F32),
        compiler_params=_cparams(("parallel",)),
        name="outproj",
    )(x, a, b, wa, wb)


def _memkv_kernel(x_ref, g_ref, w_ref, kg_ref, o_ref):
    xn = _rms(x_ref[...], g_ref[...]).astype(BF16)
    kv = jnp.dot(xn, w_ref[...], preferred_element_type=F32)
    xw = X_HEADS * X_HEAD_DIM
    o_ref[:, 0:xw] = _seg_rms(kv[:, 0:xw], kg_ref[...], X_HEAD_DIM)
    o_ref[:, xw:2 * xw] = kv[:, xw:2 * xw]


def _memkv(mem, g, w, kg, *, tm):
    m = mem.shape[0]
    xw = X_HEADS * X_HEAD_DIM
    return pl.pallas_call(
        _memkv_kernel,
        grid=(m // tm,),
        in_specs=[
            pl.BlockSpec((tm, D_MODEL), lambda i: (i, 0)),
            pl.BlockSpec((1, D_MODEL), lambda i: (0, 0)),
            pl.BlockSpec((D_MODEL, 2 * xw), lambda i: (0, 0)),
            pl.BlockSpec((1, xw), lambda i: (0, 0)),
        ],
        out_specs=pl.BlockSpec((tm, 2 * xw), lambda i: (i, 0)),
        out_shape=jax.ShapeDtypeStruct((m, 2 * xw), F32),
        compiler_params=_cparams(("parallel",)),
        name="memkv",
    )(mem, g, w, kg)


def _xattn_kernel(x_ref, mkv_ref, g_ref, wq_ref, qg_ref, wo_ref, o_ref):
    x = x_ref[0]
    xn = _rms(x, g_ref[...]).astype(BF16)
    q = jnp.dot(xn, wq_ref[...], preferred_element_type=F32)
    q = _seg_rms(q, qg_ref[...], X_HEAD_DIM) * (X_HEAD_DIM ** -0.5)
    xw = X_HEADS * X_HEAD_DIM
    outs = []
    for h in range(X_HEADS):
        lo = h * X_HEAD_DIM
        k = mkv_ref[0, :, lo:lo + X_HEAD_DIM].astype(BF16)
        v = mkv_ref[0, :, xw + lo:xw + lo + X_HEAD_DIM].astype(BF16)
        s = lax.dot_general(q[:, lo:lo + X_HEAD_DIM].astype(BF16), k, (((1,), (1,)), ((), ())),
                            preferred_element_type=F32)
        e = jnp.exp(s - jnp.max(s, axis=-1, keepdims=True))
        p = e / jnp.sum(e, axis=-1, keepdims=True)
        outs.append(jnp.dot(p.astype(BF16), v, preferred_element_type=F32))
    o = jnp.concatenate(outs, axis=-1).astype(BF16)
    o_ref[0] = x + jnp.dot(o, wo_ref[...], preferred_element_type=F32)


def _xattn(x, mkv, g, wq, qg, wo, *, tm):
    n, t, _ = x.shape
    xw = X_HEADS * X_HEAD_DIM
    n_mem = mkv.shape[1]
    return pl.pallas_call(
        _xattn_kernel,
        grid=(n, t // tm),
        in_specs=[
            pl.BlockSpec((1, tm, D_MODEL), lambda b, i: (b, i, 0)),
            pl.BlockSpec((1, n_mem, 2 * xw), lambda b, i: (b, 0, 0)),
            pl.BlockSpec((1, D_MODEL), lambda b, i: (0, 0)),
            pl.BlockSpec((D_MODEL, xw), lambda b, i: (0, 0)),
            pl.BlockSpec((1, xw), lambda b, i: (0, 0)),
            pl.BlockSpec((xw, D_MODEL), lambda b, i: (0, 0)),
        ],
        out_specs=pl.BlockSpec((1, tm, D_MODEL), lambda b, i: (b, i, 0)),
        out_shape=jax.ShapeDtypeStruct((n, t, D_MODEL), F32),
        compiler_params=_cparams(("parallel", "parallel")),
        name="xattn",
    )(x, mkv, g, wq, qg, wo)


def _row(v):
    return v.reshape(1, -1).astype(F32)


def _block_diag2(w):
    z = jnp.zeros_like(w)
    return jnp.concatenate([jnp.concatenate([w, z], axis=-1), jnp.concatenate([z, w], axis=-1)], axis=-2)


def _prep_weights(norm_ffn1_g, w_ffn1_gu, w_ffn1_down, norm_mix_g, w_in, w_out, q_norm_g, kc_norm_g, ks_norm_g,
                  kw_norm_g, gate_b, cmp_pe_k, cmp_pe_v, w_cmp_k1, w_cmp_k2, w_cmp_v1, w_cmp_v2, rwkv_mu, rwkv_w0,
                  w_decay2, rwkv_a0, w_iclr2, w_gate2, rwkv_k_k, rwkv_k_a, rwkv_r_k, lnx_g, lnx_b, norm_x_g,
                  norm_mem_g, w_xq, w_xkv, xq_norm_g, xk_norm_g, w_xo, norm_ffn2_g, w_ffn2_gu, w_ffn2_down,
                  norm_out_g):
    G = NSA_KV_HEADS
    W = {}
    W["ffn1"] = (_row(norm_ffn1_g), w_ffn1_gu.astype(BF16), w_ffn1_down.astype(BF16))
    W["ffn2"] = (_row(norm_ffn2_g), w_ffn2_gu.astype(BF16), w_ffn2_down.astype(BF16))
    W["norm_out_g"] = _row(norm_out_g)
    w_pad = jnp.concatenate([w_in[:, :NSA_COLS], jnp.zeros((D_MODEL, NSA_PAD - NSA_COLS), F32), w_in[:, NSA_COLS:]],
                            axis=1).astype(BF16)
    head_g = jnp.zeros((NSA_PAD,), F32)
    head_g = head_g.at[0:NSA_Q_W].set(jnp.tile(q_norm_g, NSA_HEADS))
    c = NSA_Q_W + 2 * NSA_KV_W
    head_g = head_g.at[c:c + NSA_KV_W].set(jnp.tile(ks_norm_g, G))
    c = NSA_Q_W + 4 * NSA_KV_W
    head_g = head_g.at[c:c + NSA_KV_W].set(jnp.tile(kw_norm_g, G))
    gb = jnp.zeros((LANES,), F32).at[0:3 * NSA_HEADS].set(gate_b)
    W["proj"] = (_row(norm_mix_g), w_pad, _row(head_g), _row(gb))
    pe = jnp.concatenate([cmp_pe_k, cmp_pe_k, cmp_pe_v, cmp_pe_v], axis=-1)[:, None, :]
    W["cmp"] = (pe, _block_diag2(w_cmp_k1).astype(BF16), _block_diag2(w_cmp_v1).astype(BF16),
                _block_diag2(w_cmp_k2).astype(BF16), _block_diag2(w_cmp_v2).astype(BF16),
                _row(jnp.tile(kc_norm_g, G)))
    pe_t = jnp.concatenate([jnp.tile(cmp_pe_k.T, (1, 2)), jnp.tile(cmp_pe_v.T, (1, 2))], axis=0)[:, None, :]
    w1_t = jnp.stack([_block_diag2(w_cmp_k1.transpose(1, 0, 2)), _block_diag2(w_cmp_v1.transpose(1, 0, 2))])
    w2_t = jnp.stack([_block_diag2(w_cmp_k2), _block_diag2(w_cmp_v2)])
    W["cmp_paged"] = (pe_t, w1_t.astype(BF16), w2_t.astype(BF16), _row(jnp.tile(kc_norm_g, G)))
    zl = jnp.zeros((DECAY_LORA, RWKV_W), F32)
    W["rwkv_pre"] = (_row(rwkv_mu), _row(rwkv_w0), jnp.concatenate([w_decay2, zl], axis=0).astype(BF16),
                     _row(rwkv_a0), jnp.concatenate([zl, w_iclr2], axis=0).astype(BF16), w_gate2.astype(BF16),
                     _row(rwkv_k_k), _row(rwkv_k_a), _row(rwkv_r_k))
    W["rwkv_post"] = (_row(lnx_g), _row(lnx_b))
    W["w_out"] = (w_out[:NSA_Q_W].astype(BF16), w_out[NSA_Q_W:].astype(BF16))
    W["xattn"] = (_row(norm_x_g), w_xq.astype(BF16), _row(jnp.tile(xq_norm_g, X_HEADS)), w_xo.astype(BF16))
    W["memkv"] = (_row(norm_mem_g), w_xkv.astype(BF16), _row(jnp.tile(xk_norm_g, X_HEADS)))
    return W


def _tile_rows(m, pref):
    return pref if m % pref == 0 else m


def _rwkv_group(pr, shift0, s0, W, n, t):
    m = n * t
    r, d, k, v, kk, b, g, bonus = _rwkv_pre(pr, shift0, W["rwkv_pre"], n=n, t=t, tm=_tile_rows(t, 256))
    lay = lambda a: _pairs_layout(a, n, t)
    o, s_new = _rwkv_scan(lay(r), lay(d), lay(k), lay(kk), lay(b), _pairs_layout_v(v, n, t), _state_layout(s0),
                          tc=min(t, 32))
    o = _pairs_unlayout_v(o, n, t)
    o = _rwkv_post(o, bonus, g, *W["rwkv_post"], tm=_tile_rows(m, 512))
    return o, _state_unlayout(s_new, n)


def _nsa_prompt_group(pn, W, n, t):
    G = NSA_KV_HEADS
    n_blk = t // CMP_BLOCK
    kc, vc = _compress_rows(pn, NSA_Q_W // LANES, W["cmp"], nblk=n_blk)
    pn3 = pn.reshape(n, t, NSA_PAD)
    col = lambda i: pn3[:, :, NSA_Q_W + i * NSA_KV_W:NSA_Q_W + (i + 1) * NSA_KV_W]
    pos = jnp.arange(t, dtype=jnp.int32)
    ksp = _key_features(col(2), pos)
    kwp = _key_features(col(4), pos)
    end = (jnp.arange(n_blk, dtype=jnp.int32) + 1) * CMP_BLOCK - 1
    kcp = _key_features(kc.reshape(n, n_blk, NSA_KV_W), end)
    vct = vc.reshape(n, n_blk, G, HEAD_DIM).transpose(0, 2, 3, 1).astype(BF16)
    return _nsa_prompt(pn, kcp, vct, ksp, kwp, n=n, t=t)


def _pad_axis(a, axis, size):
    pad = [(0, 0)] * a.ndim
    pad[axis] = (0, size - a.shape[axis])
    return jnp.pad(a, pad)


def _nsa_sample_group(pn, W, n, t_new, cache_kv, page_table, cache_win):
    G = NSA_KV_HEADS
    assert t_new <= CMP_BLOCK and t_new <= TQ
    past = page_table.shape[1] * PAGE_SIZE
    nb_past = past // CMP_BLOCK
    cache_t = cache_kv.transpose(0, 2, 3, 4, 1).reshape(cache_kv.shape[0], 4 * G * HEAD_DIM, PAGE_SIZE)
    win_t = cache_win.transpose(0, 2, 3, 4, 1)
    kc_past, vc_past = _compress_paged(page_table, cache_t, W["cmp_paged"])
    per_block = lambda a: a.reshape(n, G, nb_past, HEAD_DIM).transpose(0, 2, 1, 3).reshape(n, nb_past, NSA_KV_W)
    kc_past, vc_past = per_block(kc_past), per_block(vc_past)
    pn3 = pn.reshape(n, t_new, NSA_PAD)
    col = lambda i: pn3[:, :, NSA_Q_W + i * NSA_KV_W:NSA_Q_W + (i + 1) * NSA_KV_W]
    new_rows = _pad_axis(pn3[:, :, NSA_Q_W:NSA_Q_W + 2 * NSA_KV_W], 1, CMP_BLOCK).reshape(n * CMP_BLOCK, 2 * NSA_KV_W)
    kc_new, vc_new = _compress_rows(new_rows, 0, W["cmp"], nblk=n)
    nb_pad = -(-(nb_past + 1) // 16) * 16
    kc = _pad_axis(jnp.concatenate([kc_past, kc_new[:, None]], axis=1), 1, nb_pad)
    vc = _pad_axis(jnp.concatenate([vc_past, vc_new[:, None]], axis=1), 1, nb_pad)
    end_rel = (jnp.arange(nb_pad, dtype=jnp.int32) + 1) * CMP_BLOCK - 1 - past
    kcp = _key_features(kc, end_rel)
    vct = vc.reshape(n, nb_pad, G, HEAD_DIM).transpose(0, 2, 3, 1).astype(BF16)
    heads = lambda a: a.reshape(n, t_new, G, HEAD_DIM).transpose(0, 2, 1, 3)
    new_k = lambda i: _pad_axis(heads(col(i)), 2, KC).astype(BF16)
    new_vt = lambda i: _pad_axis(heads(col(i)).transpose(0, 1, 3, 2), 3, KC).astype(BF16)
    n_real = NSA_REP * t_new
    qt = _query_features(pn3[:, :, :NSA_Q_W]).reshape(n, t_new, G, NSA_REP, LANES)
    qt = _pad_axis(qt.transpose(0, 2, 4, 3, 1).reshape(n, G, LANES, n_real), 3, LANES)
    gates = pn3[:, :, NSA_Q_W + 6 * NSA_KV_W:NSA_COLS].reshape(n, t_new, G, NSA_REP, 3)
    gates_t = _pad_axis(gates.transpose(0, 2, 4, 3, 1).reshape(n, G, 3, n_real), 3, LANES)
    o_c, imp = _nsa_sample_cmp(qt, kcp, vct, past=past, n_new=t_new)
    imp_all = imp[:, :, :, :t_new].transpose(1, 2, 0, 3).reshape(G, nb_pad, n * t_new)
    selb = _nsa_rank(imp_all, past=past, n_new=t_new).reshape(G, nb_pad, n, t_new).transpose(2, 0, 1, 3)
    selb = jnp.pad(jnp.tile(selb, (1, 1, 1, NSA_REP)), ((0, 0),) * 3 + ((0, LANES - n_real),),
                   constant_values=-jnp.inf)
    o_t = _nsa_sample(page_table, qt, o_c, selb, win_t, new_k(2), new_vt(3), new_k(4), new_vt(5), gates_t, cache_t,
                      past=past, n_new=t_new)
    o = o_t[:, :, :, :n_real].reshape(n, G, HEAD_DIM, NSA_REP, t_new)
    return o.transpose(0, 4, 1, 3, 2).reshape(n * t_new, NSA_Q_W)


def _layer(x, mkv, shift0, s0, W, nsa_fn):
    n, t, _ = x.shape
    m = n * t
    G = NSA_KV_HEADS
    x2 = x.reshape(m, D_MODEL)
    tm = _tile_rows(m, 512)
    x2 = _ffn(x2, *W["ffn1"], W["norm_out_g"], final_norm=False, tm=tm, tf=D_FF // 2)
    pn, pr = _proj(x2, *W["proj"], tm=_tile_rows(m, 256))
    o_nsa = nsa_fn(pn)
    o_rwkv, s_new = _rwkv_group(pr, shift0, s0, W, n, t)
    x2 = _outproj(x2, o_nsa, o_rwkv, *W["w_out"], tm=tm)
    x3 = _xattn(x2.reshape(n, t, D_MODEL), mkv, *W["xattn"], tm=_tile_rows(t, 512))
    y = _ffn(x3.reshape(m, D_MODEL), *W["ffn2"], W["norm_out_g"], final_norm=True, tm=tm, tf=D_FF // 2)
    pn3 = pn.reshape(n, t, NSA_PAD)
    kv_rows = pn3[:, :, NSA_Q_W:NSA_Q_W + 4 * NSA_KV_W].reshape(n, t, 4, G, HEAD_DIM)
    win_new = pn3[:, :, NSA_Q_W + 4 * NSA_KV_W:NSA_Q_W + 6 * NSA_KV_W].reshape(n, t, 2, G, HEAD_DIM)
    shift_new = pr.reshape(n, t, RWKV_COLS)[:, -1]
    return y.reshape(n, t, D_MODEL), kv_rows, win_new, s_new, shift_new


def kernel(x_prompt, x_sample, cache_nsa_kv, cache_nsa_win, state_rwkv_s, state_rwkv_shift, cache_mem_kv, page_table, mem_prompt, norm_ffn1_g, w_ffn1_gu, w_ffn1_down, norm_mix_g, w_in, w_out, q_norm_g, kc_norm_g, ks_norm_g, kw_norm_g, gate_b, cmp_pe_k, cmp_pe_v, w_cmp_k1, w_cmp_k2, w_cmp_v1, w_cmp_v2, rwkv_mu, rwkv_w0, w_decay2, rwkv_a0, w_iclr2, w_gate2, rwkv_k_k, rwkv_k_a, rwkv_r_k, lnx_g, lnx_b, norm_x_g, norm_mem_g, w_xq, w_xkv, xq_norm_g, xk_norm_g, w_xo, norm_ffn2_g, w_ffn2_gu, w_ffn2_down, norm_out_g):
    layer_weights = (norm_ffn1_g, w_ffn1_gu, w_ffn1_down, norm_mix_g, w_in, w_out, q_norm_g, kc_norm_g, ks_norm_g,
                     kw_norm_g, gate_b, cmp_pe_k, cmp_pe_v, w_cmp_k1, w_cmp_k2, w_cmp_v1, w_cmp_v2, rwkv_mu, rwkv_w0,
                     w_decay2, rwkv_a0, w_iclr2, w_gate2, rwkv_k_k, rwkv_k_a, rwkv_r_k, lnx_g, lnx_b, norm_x_g,
                     norm_mem_g, w_xq, w_xkv, xq_norm_g, xk_norm_g, w_xo, norm_ffn2_g, w_ffn2_gu, w_ffn2_down,
                     norm_out_g)
    assert w_in.shape[0] == 1, "single-layer trunk"
    W = _prep_weights(*(w[0] for w in layer_weights))
    n_p, t_p, _ = x_prompt.shape
    n_s, t_s, _ = x_sample.shape
    n_mem = mem_prompt.shape[1]
    xw = X_HEADS * X_HEAD_DIM

    mkv_p = _memkv(mem_prompt.reshape(n_p * n_mem, D_MODEL), *W["memkv"], tm=_tile_rows(n_p * n_mem, 512))
    mkv_p = mkv_p.reshape(n_p, n_mem, 2 * xw)
    y_p, kv_p, win_p, rs_p, sh_p = _layer(
        x_prompt, mkv_p, jnp.zeros((n_p, RWKV_COLS), F32),
        jnp.zeros((n_p, RWKV_HEADS, RWKV_HEAD_DIM, RWKV_HEAD_DIM), F32), W,
        lambda pn: _nsa_prompt_group(pn, W, n_p, t_p))
    win_p = win_p[:, t_p - min(WINDOW, t_p):]

    mkv_s = cache_mem_kv[0].reshape(n_s, n_mem, 2 * xw)
    y_s, kv_s, win_new, rs_s, sh_s = _layer(
        x_sample, mkv_s, state_rwkv_shift[0], state_rwkv_s[0], W,
        lambda pn: _nsa_sample_group(pn, W, n_s, t_s, cache_nsa_kv[0], page_table, cache_nsa_win[0]))
    win_s = jnp.concatenate([cache_nsa_win[0], win_new], axis=1)[:, t_s:]

    mkv_out = mkv_p.reshape(1, n_p, n_mem, 2, X_HEADS, X_HEAD_DIM)
    return (y_p, y_s, kv_p[None], kv_s[None], win_p[None], win_s[None], rs_p[None], rs_s[None], sh_p[None],
            sh_s[None], mkv_out)
```

```python
import functools
import math

import jax
import jax.numpy as jnp
from jax import lax
from jax.experimental import pallas as pl
from jax.experimental.pallas import tpu as pltpu

F32 = jnp.float32
BF16 = jnp.bfloat16

D_MODEL = 1024
NSA_HEADS = 8
NSA_KV_HEADS = 2
NSA_REP = NSA_HEADS // NSA_KV_HEADS
HEAD_DIM = 64
CMP_BLOCK = 64
N_SEL = 16
WINDOW = 512
CMP_HIDDEN = 128
PAGE_SIZE = 128
RWKV_HEADS = 8
RWKV_HEAD_DIM = 64
RWKV_W = RWKV_HEADS * RWKV_HEAD_DIM
DECAY_LORA = 64
ICLR_LORA = 64
GATE_LORA = 128
NSA_Q_W = NSA_HEADS * HEAD_DIM
NSA_KV_W = NSA_KV_HEADS * HEAD_DIM
NSA_COLS = NSA_Q_W + 6 * NSA_KV_W + 3 * NSA_HEADS
RWKV_COLS = 3 * RWKV_W + DECAY_LORA + ICLR_LORA + GATE_LORA
X_HEADS = 4
X_HEAD_DIM = 128
D_FF = 2816
RMS_EPS = 1e-6
LNX_EPS = 64e-5
FORCED_SCORE = 1e9

LANES = 128
SUBLANES = 8
NSA_PAD = 1408
P_COLS = NSA_PAD + RWKV_COLS
TQ = 128
KC = 128
VMEM_LIMIT = 56 * 1024 * 1024

ALIBI = tuple(2.0 ** (-8.0 * (h + 1.0) / NSA_HEADS) for h in range(NSA_HEADS))


def _cparams(sem):
    return pltpu.CompilerParams(dimension_semantics=sem, vmem_limit_bytes=VMEM_LIMIT)


def _rms(x, g):
    return x * lax.rsqrt(jnp.mean(x * x, axis=-1, keepdims=True) + RMS_EPS) * g


def _seg_ones(width, seg):
    r = lax.broadcasted_iota(jnp.int32, (width, width), 0) // seg
    c = lax.broadcasted_iota(jnp.int32, (width, width), 1) // seg
    return (r == c).astype(F32)


def _seg_sum(x, seg):
    ones = _seg_ones(LANES, seg)
    parts = [jnp.dot(x[:, c:c + LANES], ones, precision=lax.Precision.HIGHEST, preferred_element_type=F32)
             for c in range(0, x.shape[1], LANES)]
    return parts[0] if len(parts) == 1 else jnp.concatenate(parts, axis=1)


def _seg_rms(x, g, seg):
    if seg == LANES:
        parts = [_rms(x[:, c:c + LANES], g[:, c:c + LANES]) for c in range(0, x.shape[1], LANES)]
        return parts[0] if len(parts) == 1 else jnp.concatenate(parts, axis=1)
    return x * lax.rsqrt(_seg_sum(x * x, seg) * (1.0 / seg) + RMS_EPS) * g


def _ffn_kernel(x_ref, g_ref, wg_ref, wu_ref, wd_ref, go_ref, o_ref, xn_ref, acc_ref, *, final_norm):
    f = pl.program_id(1)

    @pl.when(f == 0)
    def _():
        xn_ref[...] = _rms(x_ref[...], g_ref[...]).astype(BF16)
        acc_ref[...] = jnp.zeros_like(acc_ref)

    xn = xn_ref[...]
    gate = jnp.dot(xn, wg_ref[...], preferred_element_type=F32)
    up = jnp.dot(xn, wu_ref[...], preferred_element_type=F32)
    h = (gate * jax.nn.sigmoid(gate) * up).astype(BF16)
    acc_ref[...] += jnp.dot(h, wd_ref[...], preferred_element_type=F32)

    @pl.when(f == pl.num_programs(1) - 1)
    def _():
        y = x_ref[...] + 0.5 * acc_ref[...]
        if final_norm:
            y = _rms(y, go_ref[...])
        o_ref[...] = y


def _ffn(x, g, w_gu, w_down, g_out, *, final_norm, tm, tf):
    m = x.shape[0]
    nf = D_FF // tf
    return pl.pallas_call(
        functools.partial(_ffn_kernel, final_norm=final_norm),
        grid=(m // tm, nf),
        in_specs=[
            pl.BlockSpec((tm, D_MODEL), lambda i, f: (i, 0)),
            pl.BlockSpec((1, D_MODEL), lambda i, f: (0, 0)),
            pl.BlockSpec((D_MODEL, tf), lambda i, f: (0, f)),
            pl.BlockSpec((D_MODEL, tf), lambda i, f: (0, nf + f)),
            pl.BlockSpec((tf, D_MODEL), lambda i, f: (f, 0)),
            pl.BlockSpec((1, D_MODEL), lambda i, f: (0, 0)),
        ],
        out_specs=pl.BlockSpec((tm, D_MODEL), lambda i, f: (i, 0)),
        out_shape=jax.ShapeDtypeStruct((m, D_MODEL), F32),
        scratch_shapes=[pltpu.VMEM((tm, D_MODEL), BF16), pltpu.VMEM((tm, D_MODEL), F32)],
        compiler_params=_cparams(("parallel", "arbitrary")),
        name="ffn",
    )(x, g, w_gu, w_gu, w_down, g_out)


KF_COLS = 2 * NSA_KV_HEADS * LANES


def _proj_kernel(x_ref, g_ref, w_ref, hg_ref, gb_ref, place_ref, on_ref, or_ref, kf_ref, *, t):
    xn = _rms(x_ref[...], g_ref[...]).astype(BF16)
    p = jnp.dot(xn, w_ref[...], preferred_element_type=F32)
    or_ref[...] = p[:, NSA_PAD:]
    on_ref[...] = p[:, 0:NSA_PAD]
    hg = hg_ref[...]
    on_ref[:, 0:NSA_Q_W] = _seg_rms(p[:, 0:NSA_Q_W], hg[:, 0:NSA_Q_W], HEAD_DIM)
    keys = []
    for c in (NSA_Q_W + 2 * NSA_KV_W, NSA_Q_W + 4 * NSA_KV_W):
        keys.append(_seg_rms(p[:, c:c + NSA_KV_W], hg[:, c:c + NSA_KV_W], HEAD_DIM))
        on_ref[:, c:c + NSA_KV_W] = keys[-1]
    c = NSA_Q_W + 6 * NSA_KV_W
    on_ref[:, c:c + LANES] = jax.nn.sigmoid(p[:, c:c + LANES] + gb_ref[...])
    tm = p.shape[0]
    kf = (jnp.dot(keys[0].astype(BF16), place_ref[0], preferred_element_type=F32)
          + jnp.dot(keys[1].astype(BF16), place_ref[1], preferred_element_type=F32))
    pos = (pl.program_id(0) * tm + lax.broadcasted_iota(jnp.int32, (tm, KF_COLS), 0)) % t
    lane = lax.broadcasted_iota(jnp.int32, (tm, KF_COLS), 1) % LANES
    feat = jnp.where(lane == HEAD_DIM, pos // CMP_BLOCK, jnp.where(lane == HEAD_DIM + 1, pos % CMP_BLOCK, 0))
    kf_ref[...] = (kf + feat.astype(F32)).astype(BF16)


def _proj(x, g, w_pad, head_g, gate_b, place, *, t, tm):
    m = x.shape[0]
    return pl.pallas_call(
        functools.partial(_proj_kernel, t=t),
        grid=(m // tm,),
        in_specs=[
            pl.BlockSpec((tm, D_MODEL), lambda i: (i, 0)),
            pl.BlockSpec((1, D_MODEL), lambda i: (0, 0)),
            pl.BlockSpec((D_MODEL, P_COLS), lambda i: (0, 0)),
            pl.BlockSpec((1, NSA_PAD), lambda i: (0, 0)),
            pl.BlockSpec((1, LANES), lambda i: (0, 0)),
            pl.BlockSpec((2, NSA_KV_W, KF_COLS), lambda i: (0, 0, 0)),
        ],
        out_specs=[pl.BlockSpec((tm, NSA_PAD), lambda i: (i, 0)), pl.BlockSpec((tm, RWKV_COLS), lambda i: (i, 0)),
                   pl.BlockSpec((tm, KF_COLS), lambda i: (i, 0))],
        out_shape=[jax.ShapeDtypeStruct((m, NSA_PAD), F32), jax.ShapeDtypeStruct((m, RWKV_COLS), F32),
                   jax.ShapeDtypeStruct((m, KF_COLS), BF16)],
        compiler_params=_cparams(("parallel",)),
        name="proj",
    )(x, g, w_pad, head_g, gate_b, place)


def _compress_core(xk_ref, xv_ref, nblk, pe_ref, w1k_ref, w1v_ref, w2k_ref, w2v_ref, kcg_ref):
    def body(j, carry):
        acc_k, acc_v = carry
        pe = pe_ref[j]
        xk = xk_ref[pl.ds(j, nblk, stride=CMP_BLOCK), :] + pe[:, 0:LANES]
        xv = xv_ref[pl.ds(j, nblk, stride=CMP_BLOCK), :] + pe[:, LANES:2 * LANES]
        acc_k = acc_k + jnp.dot(xk.astype(BF16), w1k_ref[j], preferred_element_type=F32)
        acc_v = acc_v + jnp.dot(xv.astype(BF16), w1v_ref[j], preferred_element_type=F32)
        return acc_k, acc_v

    zero = jnp.zeros((nblk, 2 * CMP_HIDDEN), F32)
    acc_k, acc_v = lax.fori_loop(0, CMP_BLOCK, body, (zero, zero))
    hk = jax.nn.gelu(acc_k).astype(BF16)
    hv = jax.nn.gelu(acc_v).astype(BF16)
    kc = jnp.dot(hk, w2k_ref[...], preferred_element_type=F32)
    vc = jnp.dot(hv, w2v_ref[...], preferred_element_type=F32)
    return _seg_rms(kc, kcg_ref[...], HEAD_DIM), vc


def _compress_kernel(xk_ref, xv_ref, pe_ref, w1k_ref, w1v_ref, w2k_ref, w2v_ref, kcg_ref, kc_ref, vc_ref, *, nblk):
    kc, vc = _compress_core(xk_ref, xv_ref, nblk, pe_ref, w1k_ref, w1v_ref, w2k_ref, w2v_ref, kcg_ref)
    kc_ref[...] = kc
    vc_ref[...] = vc


def _cmp_weight_specs(imap):
    return [
        pl.BlockSpec((CMP_BLOCK, 1, 2 * LANES), imap(3)),
        pl.BlockSpec((CMP_BLOCK, LANES, 2 * CMP_HIDDEN), imap(3)),
        pl.BlockSpec((CMP_BLOCK, LANES, 2 * CMP_HIDDEN), imap(3)),
        pl.BlockSpec((2 * CMP_HIDDEN, LANES), imap(2)),
        pl.BlockSpec((2 * CMP_HIDDEN, LANES), imap(2)),
        pl.BlockSpec((1, LANES), imap(2)),
    ]


def _compress_rows(rows, col_block, cw, *, nblk):
    m = rows.shape[0]
    steps = m // (nblk * CMP_BLOCK)
    imap = lambda nd: (lambda i: (0,) * nd)
    return pl.pallas_call(
        functools.partial(_compress_kernel, nblk=nblk),
        grid=(steps,),
        in_specs=[pl.BlockSpec((nblk * CMP_BLOCK, LANES), lambda i: (i, col_block)),
                  pl.BlockSpec((nblk * CMP_BLOCK, LANES), lambda i: (i, col_block + 1))] + _cmp_weight_specs(imap),
        out_specs=[pl.BlockSpec((nblk, LANES), lambda i: (i, 0))] * 2,
        out_shape=[jax.ShapeDtypeStruct((steps * nblk, LANES), F32)] * 2,
        compiler_params=_cparams(("parallel",)),
        name="compress",
    )(rows, rows, *cw)


CMP_ROWS = 2 * NSA_KV_HEADS * HEAD_DIM
CMP_FEATS = 8


def _compress_paged_kernel(pt_ref, cache_ref, pe_ref, w1_ref, w2_ref, kcg_ref, kc_ref, vc_ref, buf_ref, sem, *,
                           n_pages):
    n = pl.program_id(0)
    slot = n % 2

    def page_copy(entry, buf_slot, i):
        return pltpu.make_async_copy(cache_ref.at[pt_ref[entry, i], pl.ds(0, CMP_ROWS), :],
                                     buf_ref.at[buf_slot, :, i, :], sem.at[buf_slot])

    def start_entry(entry, buf_slot):
        def start(i, c):
            page_copy(entry, buf_slot, i).start()
            return c
        lax.fori_loop(0, n_pages, start, 0)

    @pl.when(n == 0)
    def _():
        start_entry(0, 0)

    @pl.when(n + 1 < pl.num_programs(0))
    def _():
        start_entry(n + 1, 1 - slot)

    def wait(i, c):
        page_copy(n, slot, i).wait()
        return c

    lax.fori_loop(0, n_pages, wait, 0)
    blocks_per_page = PAGE_SIZE // CMP_BLOCK
    for c, out_ref in enumerate((kc_ref, vc_ref)):
        for g in range(NSA_KV_HEADS):
            def body(it, acc):
                d0 = pl.multiple_of(it * CMP_FEATS, CMP_FEATS)
                x = jnp.concatenate(
                    [buf_ref[slot, (c * NSA_KV_HEADS + g) * HEAD_DIM + d0 + u] + pe_ref[c * HEAD_DIM + d0 + u]
                     for u in range(CMP_FEATS)], axis=1)
                w = w1_ref[c, pl.ds(d0, CMP_FEATS)].reshape(CMP_FEATS * PAGE_SIZE, blocks_per_page * CMP_HIDDEN)
                return acc + jnp.dot(x.astype(BF16), w, preferred_element_type=F32)

            acc = lax.fori_loop(0, HEAD_DIM // CMP_FEATS, body,
                                jnp.zeros((n_pages, blocks_per_page * CMP_HIDDEN), F32))
            out = jnp.dot(jax.nn.gelu(acc).astype(BF16), w2_ref[c], preferred_element_type=F32)
            if c == 0:
                out = _seg_rms(out, kcg_ref[...], HEAD_DIM)
            out_ref[0, g] = out


def _compress_paged(page_table, cache_t, cw):
    nb, n_pages = page_table.shape
    width = (PAGE_SIZE // CMP_BLOCK) * HEAD_DIM
    const = lambda shape: pl.BlockSpec(shape, lambda n, pt: (0,) * len(shape))
    out = pl.BlockSpec((1, NSA_KV_HEADS, n_pages, width), lambda n, pt: (n, 0, 0, 0))
    return pl.pallas_call(
        functools.partial(_compress_paged_kernel, n_pages=n_pages),
        grid_spec=pltpu.PrefetchScalarGridSpec(
            num_scalar_prefetch=1,
            grid=(nb,),
            in_specs=[pl.BlockSpec(memory_space=pl.ANY)] + [const(w.shape) for w in cw],
            out_specs=[out, out],
            scratch_shapes=[pltpu.VMEM((2, CMP_ROWS, n_pages, PAGE_SIZE), F32), pltpu.SemaphoreType.DMA((2,))],
        ),
        out_shape=[jax.ShapeDtypeStruct((nb, NSA_KV_HEADS, n_pages, width), F32)] * 2,
        compiler_params=_cparams(("arbitrary",)),
        name="compress_paged",
    )(page_table, cache_t, *cw)


QW = NSA_REP * TQ
SEL_KC = 512
RANK_UNROLL = 8


def _wide_init(width=QW):
    return (jnp.full((1, width), -jnp.inf, F32), jnp.zeros((1, width), F32), jnp.zeros((HEAD_DIM, width), F32))


def _wide_step(state, s, vt=None, v_rows=None, group=0):
    m, l, acc = state
    m_new = jnp.maximum(m, jnp.max(s, axis=0, keepdims=True))
    m_safe = jnp.where(m_new == -jnp.inf, 0.0, m_new)
    alpha = jnp.exp(m - m_safe)
    p = jnp.exp(s - m_safe)
    l = alpha * l + jnp.sum(p, axis=0, keepdims=True)
    if vt is not None:
        pv = jnp.dot(vt, p.astype(BF16), preferred_element_type=F32)
    else:
        pv = lax.dot_general(v_rows, p.astype(BF16), (((0,), (0,)), ((), ())), preferred_element_type=F32)
        pv = pv[group * HEAD_DIM:(group + 1) * HEAD_DIM, :]
    return m_new, l, alpha * acc + pv


def _wide_out(state):
    _, l, acc = state
    return acc / jnp.maximum(l, 1e-30)


def _cmp_branch(qt, kcp, vct, t_row, n_blk):
    b_col = lax.broadcasted_iota(jnp.int32, (kcp.shape[0], qt.shape[1]), 0)
    valid = (t_row >= (b_col + 1) * CMP_BLOCK - 1) & (b_col < n_blk)
    s = jnp.where(valid, jnp.dot(kcp, qt, preferred_element_type=F32), -jnp.inf)
    m = jnp.max(s, axis=0, keepdims=True)
    e = jnp.exp(s - jnp.where(m == -jnp.inf, 0.0, m))
    p = e / jnp.maximum(jnp.sum(e, axis=0, keepdims=True), 1e-30)
    return jnp.dot(vct, p.astype(BF16), preferred_element_type=F32), p


def _select_blocks(imp, t_row, n_blk, score_ref, unrolled=False):
    nb_pad, w = imp.shape
    b_col = lax.broadcasted_iota(jnp.int32, (nb_pad, w), 0)
    cur = t_row // CMP_BLOCK
    forced = (b_col == 0) | (b_col == cur) | (b_col == cur - 1)
    score = jnp.where(forced, FORCED_SCORE, jnp.where(b_col <= cur, imp, -FORCED_SCORE))
    score = jnp.where(b_col < n_blk, score, -jnp.inf)
    score_ref[...] = score

    if unrolled:
        n_tiles = nb_pad // SUBLANES
        tile = lambda a, k: a[k * SUBLANES:(k + 1) * SUBLANES, :]
        cnt = [jnp.zeros((SUBLANES, w), F32) for _ in range(n_tiles)]
        for bp in range(n_blk):
            row = jnp.broadcast_to(score_ref[pl.ds(bp, 1), :], (SUBLANES, w))
            for k in range(n_tiles):
                sc = tile(score, k)
                if (k + 1) * SUBLANES - 1 < bp:
                    ahead = row > sc
                elif k * SUBLANES > bp:
                    ahead = row >= sc
                else:
                    ahead = (row > sc) | ((row == sc) & (tile(b_col, k) > bp))
                cnt[k] = cnt[k] + jnp.where(ahead, 1.0, 0.0)
        cnt = jnp.concatenate(cnt, axis=0)
        return jnp.where((cnt < N_SEL) & (b_col < n_blk), 0.0, -jnp.inf)

    def rank_body(it, cnt):
        base = pl.multiple_of(it * RANK_UNROLL, RANK_UNROLL)
        for u in range(RANK_UNROLL):
            bp = base + u
            row = jnp.broadcast_to(score_ref[pl.ds(bp, 1), :], (nb_pad, w))
            ahead = (row > score) | ((row == score) & (b_col > bp))
            cnt = cnt + jnp.where(ahead, 1.0, 0.0)
        return cnt

    cnt = lax.fori_loop(0, nb_pad // RANK_UNROLL, rank_body, jnp.zeros((nb_pad, w), F32))
    return jnp.where((cnt < N_SEL) & (b_col < n_blk), 0.0, -jnp.inf)


def _sum_lane_chunks(p):
    imp = p[:, 0:TQ]
    for r in range(1, NSA_REP):
        imp = imp + p[:, r * TQ:(r + 1) * TQ]
    return imp


def _nsa_prompt_kernel(q_ref, gate_ref, kcp_ref, vct_ref, ks0_ref, ks1_ref, vs_ref, kw0_ref, kw1_ref, vw_ref, o_ref,
                       score_ref, selb_ref, *, n_blk):
    ks_refs, kw_refs = (ks0_ref, ks1_ref), (kw0_ref, kw1_ref)
    i = pl.program_id(1)
    t0 = i * TQ
    t_row = t0 + (lax.broadcasted_iota(jnp.int32, (1, QW), 1) & (TQ - 1))
    blocks_per_step = SEL_KC // CMP_BLOCK
    wk = WINDOW + TQ
    ws = pl.multiple_of(jnp.maximum(t0 - WINDOW, 0), TQ)
    dist = t_row - (ws + lax.broadcasted_iota(jnp.int32, (wk, QW), 0))
    wbias = jnp.where((dist >= 0) & (dist < WINDOW), 0.0, -jnp.inf)
    n_steps = (t0 + TQ + SEL_KC - 1) // SEL_KC

    q_t = (q_ref[...] * (HEAD_DIM ** -0.5)).T
    gate_t = gate_ref[...].T
    f_row = lax.broadcasted_iota(jnp.int32, (HEAD_DIM, TQ), 0)
    groups = range(NSA_KV_HEADS)
    qts, o_c = [], []
    for g in groups:
        cols = []
        for r in range(NSA_REP):
            h = g * NSA_REP + r
            feat = jnp.where(f_row == 0, ALIBI[h] * CMP_BLOCK, jnp.where(f_row == 1, ALIBI[h], 0.0))
            cols.append(jnp.concatenate([q_t[h * HEAD_DIM:(h + 1) * HEAD_DIM, :], feat], axis=0))
        qts.append(jnp.concatenate(cols, axis=1).astype(BF16))
        o, p = _cmp_branch(qts[g], kcp_ref[0, g], vct_ref[0, g], t_row, n_blk)
        o_c.append(o)
        selb = _select_blocks(_sum_lane_chunks(p), t_row[:, 0:TQ], n_blk, score_ref, unrolled=True)
        selb_ref[g] = jnp.concatenate([selb] * NSA_REP, axis=1)

    def step_scores(c, g):
        k0 = pl.multiple_of(c * SEL_KC, SEL_KC)
        s = jnp.dot(ks_refs[g][pl.ds(k0, SEL_KC), :], qts[g], preferred_element_type=F32)
        return jnp.concatenate(
            [s[b * CMP_BLOCK:(b + 1) * CMP_BLOCK, :] + selb_ref[g, pl.ds(c * blocks_per_step + b, 1), :]
             for b in range(blocks_per_step)], axis=0)

    def step_values(c):
        return vs_ref[pl.ds(pl.multiple_of(c * SEL_KC, SEL_KC), SEL_KC), :].astype(BF16)

    def sel_step(c, states):
        v = step_values(c)
        return tuple(_wide_step(states[g], step_scores(c, g), v_rows=v, group=g) for g in groups)

    def sel_body(c, states):
        first = pl.multiple_of(c * blocks_per_step, blocks_per_step)
        rows = functools.reduce(jnp.maximum, [selb_ref[g, pl.ds(first, blocks_per_step), 0:TQ] for g in groups])
        return lax.cond(jnp.max(rows) > -1.0, functools.partial(sel_step, c), lambda s: s, states)

    states = lax.fori_loop(0, n_steps - 1, sel_body, tuple(_wide_init() for _ in groups))
    c_last = n_steps - 1
    causal = c_last * SEL_KC + lax.broadcasted_iota(jnp.int32, (SEL_KC, QW), 0) <= t_row
    v = step_values(c_last)
    o_s = [_wide_out(_wide_step(states[g], jnp.where(causal, step_scores(c_last, g), -jnp.inf), v_rows=v, group=g))
           for g in groups]

    v = vw_ref[pl.ds(ws, wk), :].astype(BF16)
    o_w = [_wide_out(_wide_step(
        _wide_init(), jnp.dot(kw_refs[g][pl.ds(ws, wk), :], qts[g], preferred_element_type=F32) + wbias,
        v_rows=v, group=g)) for g in groups]

    outs = []
    for g in groups:
        for r in range(NSA_REP):
            lanes = slice(r * TQ, (r + 1) * TQ)
            row = (g * NSA_REP + r) * 3
            outs.append(gate_t[row:row + 1, :] * o_c[g][:, lanes] + gate_t[row + 1:row + 2, :] * o_s[g][:, lanes]
                        + gate_t[row + 2:row + 3, :] * o_w[g][:, lanes])
    o_ref[...] = jnp.concatenate(outs, axis=0).T


def _nsa_prompt(pn, kf, kcp, vct, *, n, t):
    G = NSA_KV_HEADS
    assert G == 2
    n_tiles = t // TQ
    n_blk = t // CMP_BLOCK
    col_block = lambda c: c // LANES
    full = lambda shape: pl.BlockSpec((1,) + shape, lambda b, i: (b,) + (0,) * len(shape))
    seq_cols = lambda c: pl.BlockSpec((t, LANES), lambda b, i: (b, col_block(c)))
    seq_slot = lambda s: pl.BlockSpec((t, LANES), lambda b, i: (b, s))
    return pl.pallas_call(
        functools.partial(_nsa_prompt_kernel, n_blk=n_blk),
        grid=(n, n_tiles),
        in_specs=[
            pl.BlockSpec((TQ, NSA_Q_W), lambda b, i: (b * n_tiles + i, 0)),
            pl.BlockSpec((TQ, LANES), lambda b, i: (b * n_tiles + i, col_block(NSA_Q_W + 6 * NSA_KV_W))),
            full((G, n_blk, LANES)),
            full((G, HEAD_DIM, n_blk)),
            seq_slot(0), seq_slot(1),
            seq_cols(NSA_Q_W + 3 * NSA_KV_W),
            seq_slot(2), seq_slot(3),
            seq_cols(NSA_Q_W + 5 * NSA_KV_W),
        ],
        out_specs=pl.BlockSpec((TQ, NSA_Q_W), lambda b, i: (b * n_tiles + i, 0)),
        out_shape=jax.ShapeDtypeStruct((n * t, NSA_Q_W), F32),
        scratch_shapes=[pltpu.VMEM((n_blk, TQ), F32), pltpu.VMEM((G, n_blk, QW), F32)],
        compiler_params=_cparams(("parallel", "arbitrary")),
        name="nsa_prompt",
    )(pn, pn, kcp, vct, kf, kf, pn, kf, kf, pn)


PAGES_PER_STEP = 4
PAGE_GROUP = 8


SEL_ROWS = 2 * NSA_KV_HEADS * HEAD_DIM


def _nsa_sample_cmp_kernel(qt_ref, kcp_ref, vct_ref, o_ref, imp_ref, *, past, n_new, n_blk):
    n_real = NSA_REP * n_new
    t_row = past + lax.broadcasted_iota(jnp.int32, (1, LANES), 1) % n_new
    li = lax.broadcasted_iota(jnp.int32, (LANES, LANES), 0)
    lj = lax.broadcasted_iota(jnp.int32, (LANES, LANES), 1)
    same_token = ((li % n_new == lj) & (li < n_real)).astype(F32)
    for g in range(NSA_KV_HEADS):
        o, p = _cmp_branch(qt_ref[0, g], kcp_ref[0, g], vct_ref[0, g], t_row, n_blk)
        o_ref[0, g] = o
        imp_ref[0, g] = jnp.dot(p, same_token, precision=lax.Precision.HIGHEST, preferred_element_type=F32)


def _nsa_sample_cmp(qt, kcp, vct, *, past, n_new):
    n, G, nb_pad = kcp.shape[:3]
    full = lambda shape: pl.BlockSpec((1,) + shape, lambda b: (b,) + (0,) * len(shape))
    return pl.pallas_call(
        functools.partial(_nsa_sample_cmp_kernel, past=past, n_new=n_new, n_blk=past // CMP_BLOCK + 1),
        grid=(n,),
        in_specs=[full((G, LANES, LANES)), full((G, nb_pad, LANES)), full((G, HEAD_DIM, nb_pad))],
        out_specs=[full((G, HEAD_DIM, LANES)), full((G, nb_pad, LANES))],
        out_shape=[jax.ShapeDtypeStruct((n, G, HEAD_DIM, LANES), F32),
                   jax.ShapeDtypeStruct((n, G, nb_pad, LANES), F32)],
        compiler_params=_cparams(("parallel",)),
        name="nsa_sample_cmp",
    )(qt, kcp, vct)


def _nsa_rank_kernel(imp_ref, selb_ref, score_ref, *, past, n_new, n_blk):
    t_row = past + lax.broadcasted_iota(jnp.int32, (1, imp_ref.shape[2]), 1) % n_new
    selb_ref[0] = _select_blocks(imp_ref[0], t_row, n_blk, score_ref)


def _nsa_rank(imp, *, past, n_new):
    G, nb_pad, w = imp.shape
    blk = pl.BlockSpec((1, nb_pad, w), lambda g: (g, 0, 0))
    return pl.pallas_call(
        functools.partial(_nsa_rank_kernel, past=past, n_new=n_new, n_blk=past // CMP_BLOCK + 1),
        grid=(G,),
        in_specs=[blk],
        out_specs=blk,
        out_shape=jax.ShapeDtypeStruct((G, nb_pad, w), F32),
        scratch_shapes=[pltpu.VMEM((nb_pad, w), F32)],
        compiler_params=_cparams(("parallel",)),
        name="nsa_rank",
    )(imp)


def _nsa_sample_kernel(pt_ref, qt_ref, oc_ref, selb_ref, win_ref, knew_ref, vnewt_ref, kwnew_ref, vwnewt_ref, gt_ref,
                       cache_ref, o_ref, buf_ref, slot_ref, sem, *, past, n_new, n_pages):
    n = pl.program_id(0)
    G = NSA_KV_HEADS
    n_real = NSA_REP * n_new
    blocks_per_page = PAGE_SIZE // CMP_BLOCK
    nb_past = past // CMP_BLOCK
    w_buf = min(WINDOW, past)
    lane = lax.broadcasted_iota(jnp.int32, (1, LANES), 1)
    t_row = past + lane % n_new
    real = lane < n_real
    selb_ref = selb_ref.at[0]

    def page_copy(lp, slot):
        return pltpu.make_async_copy(cache_ref.at[pt_ref[n, lp], pl.ds(CMP_ROWS, SEL_ROWS), :],
                                     buf_ref.at[pl.ds(slot * SEL_ROWS, SEL_ROWS), :], sem)

    def wanted_rows(first, count):
        rows = jnp.maximum(selb_ref[0, pl.ds(first, count), :], selb_ref[1, pl.ds(first, count), :])
        return jnp.max(jnp.where(real, rows, -jnp.inf)) > -1.0

    def page_body(lp, cnt):
        wanted = wanted_rows(lp * blocks_per_page, 1)
        for b in range(1, blocks_per_page):
            wanted = wanted | wanted_rows(lp * blocks_per_page + b, 1)

        @pl.when(wanted)
        def _():
            page_copy(lp, cnt).start()
            slot_ref[cnt] = lp

        return cnt + wanted.astype(jnp.int32)

    def group_body(pg, cnt):
        first = pl.multiple_of(pg * (PAGE_GROUP * blocks_per_page), PAGE_GROUP * blocks_per_page)
        return lax.cond(wanted_rows(first, PAGE_GROUP * blocks_per_page),
                        lambda c: lax.fori_loop(pg * PAGE_GROUP, (pg + 1) * PAGE_GROUP, page_body, c),
                        lambda c: c, cnt)

    n_slots = lax.fori_loop(0, n_pages // PAGE_GROUP, group_body, jnp.int32(0))
    n_steps = (n_slots + PAGES_PER_STEP - 1) // PAGES_PER_STEP

    def fill_body(j, c):
        slot = n_slots + j

        @pl.when(slot < n_steps * PAGES_PER_STEP)
        def _():
            page_copy(0, slot).start()
            slot_ref[slot] = -1

        return c

    lax.fori_loop(0, PAGES_PER_STEP - 1, fill_body, 0)

    def wait_body(s, c):
        page_copy(jnp.maximum(slot_ref[s], 0), s).wait()
        return c

    lax.fori_loop(0, n_steps * PAGES_PER_STEP, wait_body, 0)

    k_idx = lax.broadcasted_iota(jnp.int32, (PAGE_SIZE, LANES), 0)
    new_idx = lax.broadcasted_iota(jnp.int32, (knew_ref.shape[2], LANES), 0)
    pos_new = past + new_idx
    pos_win = past - w_buf + lax.broadcasted_iota(jnp.int32, (w_buf, LANES), 0)
    tn_dims = (((0,), (0,)), ((), ()))

    for g in range(G):
        q64 = qt_ref[0, g][0:HEAD_DIM, :]
        slope = jnp.zeros((1, LANES), F32)
        for r in range(NSA_REP):
            slope = jnp.where((lane >= r * n_new) & (lane < (r + 1) * n_new), ALIBI[g * NSA_REP + r], slope)

        def alibi(s, pos):
            return s - slope * (t_row - pos).astype(F32)

        def sel_body(st, state):
            kts, vts, pos, bias = [], [], [], []
            for j in range(PAGES_PER_STEP):
                slot = st * PAGES_PER_STEP + j
                base = pl.multiple_of(slot * SEL_ROWS, SEL_ROWS)
                kts.append(buf_ref[pl.ds(base + g * HEAD_DIM, HEAD_DIM), :])
                vts.append(buf_ref[pl.ds(base + (NSA_KV_HEADS + g) * HEAD_DIM, HEAD_DIM), :])
                lp = slot_ref[slot]
                live = jnp.where(lp >= 0, 0.0, -jnp.inf)
                lp = jnp.maximum(lp, 0)
                pos.append(lp * PAGE_SIZE + k_idx)
                bias += [jnp.broadcast_to(selb_ref[g, pl.ds(lp * blocks_per_page + b, 1), :] + live,
                                          (CMP_BLOCK, LANES)) for b in range(blocks_per_page)]
            kt = jnp.concatenate(kts, axis=1).astype(BF16)
            s = lax.dot_general(kt, q64, tn_dims, preferred_element_type=F32)
            s = alibi(s, jnp.concatenate(pos, axis=0)) + jnp.concatenate(bias, axis=0)
            return _wide_step(state, s, jnp.concatenate(vts, axis=1).astype(BF16))

        state = lax.fori_loop(0, n_steps, sel_body, _wide_init(LANES))
        s = alibi(jnp.dot(knew_ref[0, g], q64, preferred_element_type=F32), pos_new)
        s = jnp.where(pos_new <= t_row, s + selb_ref[g, pl.ds(nb_past, 1), :], -jnp.inf)
        o_s = _wide_out(_wide_step(state, s, vnewt_ref[0, g]))

        s = lax.dot_general(win_ref[0, 0, g].astype(BF16), q64, tn_dims, preferred_element_type=F32)
        s = jnp.where(t_row - pos_win < WINDOW, alibi(s, pos_win), -jnp.inf)
        state = _wide_step(_wide_init(LANES), s, win_ref[0, 1, g].astype(BF16))
        s = alibi(jnp.dot(kwnew_ref[0, g], q64, preferred_element_type=F32), pos_new)
        s = jnp.where((pos_new <= t_row) & (new_idx < n_new), s, -jnp.inf)
        o_w = _wide_out(_wide_step(state, s, vwnewt_ref[0, g]))

        o_ref[0, g] = (gt_ref[0, g, pl.ds(0, 1), :] * oc_ref[0, g] + gt_ref[0, g, pl.ds(1, 1), :] * o_s
                       + gt_ref[0, g, pl.ds(2, 1), :] * o_w)


def _nsa_sample(page_table, qt, o_c, selb, win_t, knew, vnewt, kwnew, vwnewt, gates_t, cache_t, *, past, n_new):
    n = qt.shape[0]
    G = NSA_KV_HEADS
    n_pages = page_table.shape[1]
    assert n_pages % PAGE_GROUP == 0
    nb_pad = selb.shape[2]
    w_buf = win_t.shape[4]
    assert w_buf == min(WINDOW, past)
    n_newp = knew.shape[2]
    full = lambda shape: pl.BlockSpec((1,) + shape, lambda b, pt: (b,) + (0,) * len(shape))
    kern = functools.partial(_nsa_sample_kernel, past=past, n_new=n_new, n_pages=n_pages)
    return pl.pallas_call(
        kern,
        grid_spec=pltpu.PrefetchScalarGridSpec(
            num_scalar_prefetch=1,
            grid=(n,),
            in_specs=[
                full((G, LANES, LANES)),
                full((G, HEAD_DIM, LANES)),
                full((G, nb_pad, LANES)),
                full((2, G, HEAD_DIM, w_buf)),
                full((G, n_newp, HEAD_DIM)),
                full((G, HEAD_DIM, n_newp)),
                full((G, n_newp, HEAD_DIM)),
                full((G, HEAD_DIM, n_newp)),
                full((G, 3, LANES)),
                pl.BlockSpec(memory_space=pl.ANY),
            ],
            out_specs=full((G, HEAD_DIM, LANES)),
            scratch_shapes=[
                pltpu.VMEM((n_pages * SEL_ROWS, PAGE_SIZE), F32),
                pltpu.SMEM((n_pages + PAGES_PER_STEP,), jnp.int32),
                pltpu.SemaphoreType.DMA(()),
            ],
        ),
        out_shape=jax.ShapeDtypeStruct((n, G, HEAD_DIM, LANES), F32),
        compiler_params=_cparams(("arbitrary",)),
        name="nsa_sample",
    )(page_table, qt, o_c, selb, win_t, knew, vnewt, kwnew, vwnewt, gates_t, cache_t)


def _alibi_features(pos):
    blk = (pos // CMP_BLOCK).astype(F32)
    off = (pos % CMP_BLOCK).astype(F32)
    pad = jnp.zeros(pos.shape + (HEAD_DIM - 2,), F32)
    return jnp.concatenate([blk[..., None], off[..., None], pad], axis=-1)


def _query_features(q):
    n, t, _ = q.shape
    qh = q.reshape(n, t, NSA_HEADS, HEAD_DIM) * (HEAD_DIM ** -0.5)
    slope = jnp.asarray(ALIBI, F32)
    feat = jnp.zeros((NSA_HEADS, HEAD_DIM), F32).at[:, 0].set(slope * CMP_BLOCK).at[:, 1].set(slope)
    feat = jnp.broadcast_to(feat, (n, t, NSA_HEADS, HEAD_DIM))
    return jnp.concatenate([qh, feat], axis=-1).reshape(n, t, NSA_HEADS * LANES).astype(BF16)


def _key_features(k, pos):
    n, l, _ = k.shape
    kh = k.reshape(n, l, NSA_KV_HEADS, HEAD_DIM).transpose(0, 2, 1, 3)
    feat = jnp.broadcast_to(_alibi_features(pos), (n, NSA_KV_HEADS, l, HEAD_DIM))
    return jnp.concatenate([kh, feat], axis=-1).astype(BF16)


def _rwkv_pre_kernel(p_ref, before_ref, shift_ref, mu_ref, w0_ref, wd_ref, a0_ref, wa_ref, wg_ref, kk_ref, ka_ref,
                     rk_ref, r_o, d_o, k_o, v_o, kk_o, b_o, g_o, bonus_o):
    p = p_ref[...]
    first = jnp.where(pl.program_id(1) == 0, shift_ref[0], before_ref[7:8, :])
    row = lax.broadcasted_iota(jnp.int32, p.shape, 0)
    prev = jnp.where(row == 0, first, pltpu.roll(p, 1, axis=0))
    xs = p + (prev - p) * mu_ref[...]
    W = RWKV_W
    r, k, v = xs[:, 0:W], xs[:, W:2 * W], xs[:, 2 * W:3 * W]
    lora = xs[:, 3 * W:3 * W + LANES]
    xg = xs[:, 3 * W + LANES:3 * W + 2 * LANES]
    z = w0_ref[...] + jnp.dot(jnp.tanh(lora).astype(BF16), wd_ref[...], preferred_element_type=F32)
    nz = -z
    softplus = jnp.maximum(nz, 0.0) + jnp.log(1.0 + jnp.exp(-jnp.abs(nz)))
    decay = jnp.exp(-jnp.exp(-softplus - 0.5))
    a = jax.nn.sigmoid(a0_ref[...] + jnp.dot(lora.astype(BF16), wa_ref[...], preferred_element_type=F32))
    g = jnp.dot(jax.nn.sigmoid(xg).astype(BF16), wg_ref[...], preferred_element_type=F32)
    kk = k * kk_ref[...]
    kk = kk * lax.rsqrt(jnp.maximum(_seg_sum(kk * kk, RWKV_HEAD_DIM), 1e-24))
    k_h = k * (1.0 + (a - 1.0) * ka_ref[...])
    r_o[...] = r
    d_o[...] = decay
    k_o[...] = k_h
    v_o[...] = v
    kk_o[...] = kk
    b_o[...] = kk * a
    g_o[...] = g
    bonus_o[...] = _seg_sum(r * k_h * rk_ref[...], RWKV_HEAD_DIM) * v


def _rwkv_pre(p, shift0, rw, *, n, t, tm):
    m = n * t
    assert tm % 8 == 0 and t % tm == 0
    tiles = t // tm
    row = lambda w: pl.BlockSpec((1, w), lambda b, i: (0, 0))
    mat = lambda a, c: pl.BlockSpec((a, c), lambda b, i: (0, 0))
    out = pl.BlockSpec((tm, RWKV_W), lambda b, i: (b * tiles + i, 0))
    return pl.pallas_call(
        _rwkv_pre_kernel,
        grid=(n, tiles),
        in_specs=[
            pl.BlockSpec((tm, RWKV_COLS), lambda b, i: (b * tiles + i, 0)),
            pl.BlockSpec((8, RWKV_COLS), lambda b, i: (jnp.maximum((b * tiles + i) * (tm // 8) - 1, 0), 0)),
            pl.BlockSpec((1, 1, RWKV_COLS), lambda b, i: (b, 0, 0)),
            row(RWKV_COLS), row(RWKV_W), mat(LANES, RWKV_W), row(RWKV_W), mat(LANES, RWKV_W),
            mat(GATE_LORA, RWKV_W), row(RWKV_W), row(RWKV_W), row(RWKV_W),
        ],
        out_specs=[out] * 8,
        out_shape=[jax.ShapeDtypeStruct((m, RWKV_W), F32)] * 8,
        compiler_params=_cparams(("parallel", "arbitrary")),
        name="rwkv_pre",
    )(p, p, shift0.reshape(n, 1, RWKV_COLS), *rw)


RW_J = RWKV_HEAD_DIM // 2
RW_PAIRS = LANES // 2


def _rwkv_scan_kernel(r_ref, d_ref, k_ref, kk_ref, b_ref, v_ref, s0_ref, o_ref, sout_ref, s_ref, *, tc):
    c = pl.program_id(1)

    @pl.when(c == 0)
    def _():
        s_ref[...] = s0_ref[0]

    def both_halves(x):
        return x + pltpu.roll(x, RW_PAIRS, axis=1)

    tiles = [pl.ds(k * SUBLANES, SUBLANES) for k in range(RWKV_HEAD_DIM // SUBLANES)]

    def key_row(ref, t, j):
        return jnp.broadcast_to(ref[0, t, pl.ds(j, 1), :], (SUBLANES, LANES))

    def step(t, sa_parts):
        t_next = jnp.minimum(t + 1, tc - 1)
        sa_next = []
        for i, rows in enumerate(tiles):
            u = -both_halves(sa_parts[i])
            vt = v_ref[0, t, rows, :]
            o = jnp.zeros((SUBLANES, LANES), F32)
            sa = jnp.zeros((SUBLANES, LANES), F32)
            for j in range(RW_J):
                h = (s_ref[j, rows, :] * key_row(d_ref, t, j) + u * key_row(b_ref, t, j)
                     + vt * key_row(k_ref, t, j))
                s_ref[j, rows, :] = h
                o = o + h * key_row(r_ref, t, j)
                sa = sa + h * key_row(kk_ref, t_next, j)
            o_ref[0, t, rows, :] = both_halves(o)
            sa_next.append(sa)
        return tuple(sa_next)

    sa0 = []
    for rows in tiles:
        sa = jnp.zeros((SUBLANES, LANES), F32)
        for j in range(RW_J):
            sa = sa + s_ref[j, rows, :] * key_row(kk_ref, 0, j)
        sa0.append(sa)
    lax.fori_loop(0, tc, step, tuple(sa0))
    sout_ref[0] = s_ref[...]


def _rwkv_scan(r, d, k, kk, b, v, s0, *, tc):
    pg, t = r.shape[:2]
    vec = pl.BlockSpec((1, tc, RW_J, LANES), lambda g, c: (g, c, 0, 0))
    val = pl.BlockSpec((1, tc, RWKV_HEAD_DIM, LANES), lambda g, c: (g, c, 0, 0))
    st = pl.BlockSpec((1, RW_J, RWKV_HEAD_DIM, LANES), lambda g, c: (g, 0, 0, 0))
    return pl.pallas_call(
        functools.partial(_rwkv_scan_kernel, tc=tc),
        grid=(pg, t // tc),
        in_specs=[vec, vec, vec, vec, vec, val, st],
        out_specs=[val, st],
        out_shape=[jax.ShapeDtypeStruct((pg, t, RWKV_HEAD_DIM, LANES), F32),
                   jax.ShapeDtypeStruct((pg, RW_J, RWKV_HEAD_DIM, LANES), F32)],
        scratch_shapes=[pltpu.VMEM((RW_J, RWKV_HEAD_DIM, LANES), F32)],
        compiler_params=_cparams(("parallel", "arbitrary")),
        name="rwkv_scan",
    )(r, d, k, kk, b, v, s0)


def _rwkv_post_kernel(o_ref, bonus_ref, g_ref, lg_ref, lb_ref, out_ref):
    o = o_ref[...]
    inv = 1.0 / RWKV_HEAD_DIM
    mean = _seg_sum(o, RWKV_HEAD_DIM) * inv
    cen = o - mean
    var = _seg_sum(cen * cen, RWKV_HEAD_DIM) * inv
    y = cen * lax.rsqrt(var + LNX_EPS) * lg_ref[...] + lb_ref[...]
    out_ref[...] = (y + bonus_ref[...]) * g_ref[...]


def _rwkv_post(o, bonus, g, lnx_g, lnx_b, *, tm):
    m = o.shape[0]
    blk = pl.BlockSpec((tm, RWKV_W), lambda i: (i, 0))
    row = pl.BlockSpec((1, RWKV_W), lambda i: (0, 0))
    return pl.pallas_call(
        _rwkv_post_kernel,
        grid=(m // tm,),
        in_specs=[blk, blk, blk, row, row],
        out_specs=blk,
        out_shape=jax.ShapeDtypeStruct((m, RWKV_W), F32),
        compiler_params=_cparams(("parallel",)),
        name="rwkv_post",
    )(o, bonus, g, lnx_g, lnx_b)


def _pairs_layout(x, n, t):
    pg = n * RWKV_HEADS // RW_PAIRS
    y = x.reshape(n, t, RWKV_HEADS, 2, RW_J).transpose(1, 4, 3, 0, 2).reshape(t, RW_J, 2, pg, RW_PAIRS)
    return y.transpose(3, 0, 1, 2, 4).reshape(pg, t, RW_J, LANES)


def _pairs_layout_v(x, n, t):
    pg = n * RWKV_HEADS // RW_PAIRS
    y = x.reshape(n, t, RWKV_HEADS, RWKV_HEAD_DIM).transpose(1, 3, 0, 2).reshape(t, RWKV_HEAD_DIM, pg, RW_PAIRS)
    y = y.transpose(2, 0, 1, 3)
    return jnp.concatenate([y, y], axis=-1)


def _pairs_unlayout_v(y, n, t):
    pg = y.shape[0]
    z = y[..., :RW_PAIRS].transpose(1, 2, 0, 3).reshape(t, RWKV_HEAD_DIM, n, RWKV_HEADS)
    return z.transpose(2, 0, 3, 1).reshape(n * t, RWKV_W)


def _state_layout(s0):
    n = s0.shape[0]
    pg = n * RWKV_HEADS // RW_PAIRS
    y = s0.reshape(pg, RW_PAIRS, RWKV_HEAD_DIM, 2, RW_J)
    return y.transpose(0, 4, 2, 3, 1).reshape(pg, RW_J, RWKV_HEAD_DIM, LANES)


def _state_unlayout(y, n):
    pg = y.shape[0]
    z = y.reshape(pg, RW_J, RWKV_HEAD_DIM, 2, RW_PAIRS).transpose(0, 4, 2, 3, 1)
    return z.reshape(n, RWKV_HEADS, RWKV_HEAD_DIM, RWKV_HEAD_DIM)


def _outproj_kernel(x_ref, a_ref, b_ref, wa_ref, wb_ref, o_ref):
    y = jnp.dot(a_ref[...].astype(BF16), wa_ref[...], preferred_element_type=F32)
    y = y + jnp.dot(b_ref[...].astype(BF16), wb_ref[...], preferred_element_type=F32)
    o_ref[...] = x_ref[...] + y


def _outproj(x, a, b, wa, wb, *, tm):
    m = x.shape[0]
    return pl.pallas_call(
        _outproj_kernel,
        grid=(m // tm,),
        in_specs=[
            pl.BlockSpec((tm, D_MODEL), lambda i: (i, 0)),
            pl.BlockSpec((tm, a.shape[1]), lambda i: (i, 0)),
            pl.BlockSpec((tm, b.shape[1]), lambda i: (i, 0)),
            pl.BlockSpec(wa.shape, lambda i: (0, 0)),
            pl.BlockSpec(wb.shape, lambda i: (0, 0)),
        ],
        out_specs=pl.BlockSpec((tm, D_MODEL), lambda i: (i, 0)),
        out_shape=jax.ShapeDtypeStruct((m, D_MODEL), F32),
        compiler_params=_cparams(("parallel",)),
        name="outproj",
    )(x, a, b, wa, wb)


def _memkv_kernel(x_ref, g_ref, w_ref, kg_ref, o_ref):
    xn = _rms(x_ref[...], g_ref[...]).astype(BF16)
    kv = jnp.dot(xn, w_ref[...], preferred_element_type=F32)
    xw = X_HEADS * X_HEAD_DIM
    o_ref[:, 0:xw] = _seg_rms(kv[:, 0:xw], kg_ref[...], X_HEAD_DIM)
    o_ref[:, xw:2 * xw] = kv[:, xw:2 * xw]


def _memkv(mem, g, w, kg, *, tm):
    m = mem.shape[0]
    xw = X_HEADS * X_HEAD_DIM
    return pl.pallas_call(
        _memkv_kernel,
        grid=(m // tm,),
        in_specs=[
            pl.BlockSpec((tm, D_MODEL), lambda i: (i, 0)),
            pl.BlockSpec((1, D_MODEL), lambda i: (0, 0)),
            pl.BlockSpec((D_MODEL, 2 * xw), lambda i: (0, 0)),
            pl.BlockSpec((1, xw), lambda i: (0, 0)),
        ],
        out_specs=pl.BlockSpec((tm, 2 * xw), lambda i: (i, 0)),
        out_shape=jax.ShapeDtypeStruct((m, 2 * xw), F32),
        compiler_params=_cparams(("parallel",)),
        name="memkv",
    )(mem, g, w, kg)


def _xattn_kernel(x_ref, mkv_ref, g_ref, wq_ref, qg_ref, wo_ref, o_ref):
    x = x_ref[0]
    xn = _rms(x, g_ref[...]).astype(BF16)
    q = jnp.dot(xn, wq_ref[...], preferred_element_type=F32)
    q = _seg_rms(q, qg_ref[...], X_HEAD_DIM) * (X_HEAD_DIM ** -0.5)
    xw = X_HEADS * X_HEAD_DIM
    outs = []
    for h in range(X_HEADS):
        lo = h * X_HEAD_DIM
        k = mkv_ref[0, :, lo:lo + X_HEAD_DIM].astype(BF16)
        v = mkv_ref[0, :, xw + lo:xw + lo + X_HEAD_DIM].astype(BF16)
        s = lax.dot_general(q[:, lo:lo + X_HEAD_DIM].astype(BF16), k, (((1,), (1,)), ((), ())),
                            preferred_element_type=F32)
        e = jnp.exp(s - jnp.max(s, axis=-1, keepdims=True))
        p = e / jnp.sum(e, axis=-1, keepdims=True)
        outs.append(jnp.dot(p.astype(BF16), v, preferred_element_type=F32))
    o = jnp.concatenate(outs, axis=-1).astype(BF16)
    o_ref[0] = x + jnp.dot(o, wo_ref[...], preferred_element_type=F32)


def _xattn(x, mkv, g, wq, qg, wo, *, tm):
    n, t, _ = x.shape
    xw = X_HEADS * X_HEAD_DIM
    n_mem = mkv.shape[1]
    return pl.pallas_call(
        _xattn_kernel,
        grid=(n, t // tm),
        in_specs=[
            pl.BlockSpec((1, tm, D_MODEL), lambda b, i: (b, i, 0)),
            pl.BlockSpec((1, n_mem, 2 * xw), lambda b, i: (b, 0, 0)),
            pl.BlockSpec((1, D_MODEL), lambda b, i: (0, 0)),
            pl.BlockSpec((D_MODEL, xw), lambda b, i: (0, 0)),
            pl.BlockSpec((1, xw), lambda b, i: (0, 0)),
            pl.BlockSpec((xw, D_MODEL), lambda b, i: (0, 0)),
        ],
        out_specs=pl.BlockSpec((1, tm, D_MODEL), lambda b, i: (b, i, 0)),
        out_shape=jax.ShapeDtypeStruct((n, t, D_MODEL), F32),
        compiler_params=_cparams(("parallel", "parallel")),
        name="xattn",
    )(x, mkv, g, wq, qg, wo)


def _row(v):
    return v.reshape(1, -1).astype(F32)


def _block_diag2(w):
    z = jnp.zeros_like(w)
    return jnp.concatenate([jnp.concatenate([w, z], axis=-1), jnp.concatenate([z, w], axis=-1)], axis=-2)


def _prep_weights(norm_ffn1_g, w_ffn1_gu, w_ffn1_down, norm_mix_g, w_in, w_out, q_norm_g, kc_norm_g, ks_norm_g,
                  kw_norm_g, gate_b, cmp_pe_k, cmp_pe_v, w_cmp_k1, w_cmp_k2, w_cmp_v1, w_cmp_v2, rwkv_mu, rwkv_w0,
                  w_decay2, rwkv_a0, w_iclr2, w_gate2, rwkv_k_k, rwkv_k_a, rwkv_r_k, lnx_g, lnx_b, norm_x_g,
                  norm_mem_g, w_xq, w_xkv, xq_norm_g, xk_norm_g, w_xo, norm_ffn2_g, w_ffn2_gu, w_ffn2_down,
                  norm_out_g):
    G = NSA_KV_HEADS
    W = {}
    W["ffn1"] = (_row(norm_ffn1_g), w_ffn1_gu.astype(BF16), w_ffn1_down.astype(BF16))
    W["ffn2"] = (_row(norm_ffn2_g), w_ffn2_gu.astype(BF16), w_ffn2_down.astype(BF16))
    W["norm_out_g"] = _row(norm_out_g)
    w_pad = jnp.concatenate([w_in[:, :NSA_COLS], jnp.zeros((D_MODEL, NSA_PAD - NSA_COLS), F32), w_in[:, NSA_COLS:]],
                            axis=1).astype(BF16)
    head_g = jnp.zeros((NSA_PAD,), F32)
    head_g = head_g.at[0:NSA_Q_W].set(jnp.tile(q_norm_g, NSA_HEADS))
    c = NSA_Q_W + 2 * NSA_KV_W
    head_g = head_g.at[c:c + NSA_KV_W].set(jnp.tile(ks_norm_g, G))
    c = NSA_Q_W + 4 * NSA_KV_W
    head_g = head_g.at[c:c + NSA_KV_W].set(jnp.tile(kw_norm_g, G))
    gb = jnp.zeros((LANES,), F32).at[0:3 * NSA_HEADS].set(gate_b)
    src = jnp.arange(NSA_KV_W)
    place = jnp.stack([jax.nn.one_hot((b * G + src // HEAD_DIM) * LANES + src % HEAD_DIM, KF_COLS, dtype=BF16)
                       for b in range(2)])
    W["proj"] = (_row(norm_mix_g), w_pad, _row(head_g), _row(gb), place)
    pe = jnp.concatenate([cmp_pe_k, cmp_pe_k, cmp_pe_v, cmp_pe_v], axis=-1)[:, None, :]
    W["cmp"] = (pe, _block_diag2(w_cmp_k1).astype(BF16), _block_diag2(w_cmp_v1).astype(BF16),
                _block_diag2(w_cmp_k2).astype(BF16), _block_diag2(w_cmp_v2).astype(BF16),
                _row(jnp.tile(kc_norm_g, G)))
    pe_t = jnp.concatenate([jnp.tile(cmp_pe_k.T, (1, 2)), jnp.tile(cmp_pe_v.T, (1, 2))], axis=0)[:, None, :]
    w1_t = jnp.stack([_block_diag2(w_cmp_k1.transpose(1, 0, 2)), _block_diag2(w_cmp_v1.transpose(1, 0, 2))])
    w2_t = jnp.stack([_block_diag2(w_cmp_k2), _block_diag2(w_cmp_v2)])
    W["cmp_paged"] = (pe_t, w1_t.astype(BF16), w2_t.astype(BF16), _row(jnp.tile(kc_norm_g, G)))
    zl = jnp.zeros((DECAY_LORA, RWKV_W), F32)
    W["rwkv_pre"] = (_row(rwkv_mu), _row(rwkv_w0), jnp.concatenate([w_decay2, zl], axis=0).astype(BF16),
                     _row(rwkv_a0), jnp.concatenate([zl, w_iclr2], axis=0).astype(BF16), w_gate2.astype(BF16),
                     _row(rwkv_k_k), _row(rwkv_k_a), _row(rwkv_r_k))
    W["rwkv_post"] = (_row(lnx_g), _row(lnx_b))
    W["w_out"] = (w_out[:NSA_Q_W].astype(BF16), w_out[NSA_Q_W:].astype(BF16))
    W["xattn"] = (_row(norm_x_g), w_xq.astype(BF16), _row(jnp.tile(xq_norm_g, X_HEADS)), w_xo.astype(BF16))
    W["memkv"] = (_row(norm_mem_g), w_xkv.astype(BF16), _row(jnp.tile(xk_norm_g, X_HEADS)))
    return W


def _tile_rows(m, pref):
    return pref if m % pref == 0 else m


def _rwkv_group(pr, shift0, s0, W, n, t):
    m = n * t
    r, d, k, v, kk, b, g, bonus = _rwkv_pre(pr, shift0, W["rwkv_pre"], n=n, t=t, tm=_tile_rows(t, 256))
    lay = lambda a: _pairs_layout(a, n, t)
    o, s_new = _rwkv_scan(lay(r), lay(d), lay(k), lay(kk), lay(b), _pairs_layout_v(v, n, t), _state_layout(s0),
                          tc=min(t, 64))
    o = _pairs_unlayout_v(o, n, t)
    o = _rwkv_post(o, bonus, g, *W["rwkv_post"], tm=_tile_rows(m, 512))
    return o, _state_unlayout(s_new, n)


def _nsa_prompt_group(pn, kf, W, n, t):
    G = NSA_KV_HEADS
    n_blk = t // CMP_BLOCK
    kc, vc = _compress_rows(pn, NSA_Q_W // LANES, W["cmp"], nblk=n_blk)
    end = (jnp.arange(n_blk, dtype=jnp.int32) + 1) * CMP_BLOCK - 1
    kcp = _key_features(kc.reshape(n, n_blk, NSA_KV_W), end)
    vct = vc.reshape(n, n_blk, G, HEAD_DIM).transpose(0, 2, 3, 1).astype(BF16)
    return _nsa_prompt(pn, kf, kcp, vct, n=n, t=t)


def _pad_axis(a, axis, size):
    pad = [(0, 0)] * a.ndim
    pad[axis] = (0, size - a.shape[axis])
    return jnp.pad(a, pad)


def _nsa_sample_group(pn, W, n, t_new, cache_kv, page_table, cache_win):
    G = NSA_KV_HEADS
    assert t_new <= CMP_BLOCK and t_new <= TQ
    past = page_table.shape[1] * PAGE_SIZE
    nb_past = past // CMP_BLOCK
    cache_t = cache_kv.transpose(0, 2, 3, 4, 1).reshape(cache_kv.shape[0], 4 * G * HEAD_DIM, PAGE_SIZE)
    win_t = cache_win.transpose(0, 2, 3, 4, 1)
    kc_past, vc_past = _compress_paged(page_table, cache_t, W["cmp_paged"])
    per_block = lambda a: a.reshape(n, G, nb_past, HEAD_DIM).transpose(0, 2, 1, 3).reshape(n, nb_past, NSA_KV_W)
    kc_past, vc_past = per_block(kc_past), per_block(vc_past)
    pn3 = pn.reshape(n, t_new, NSA_PAD)
    col = lambda i: pn3[:, :, NSA_Q_W + i * NSA_KV_W:NSA_Q_W + (i + 1) * NSA_KV_W]
    new_rows = _pad_axis(pn3[:, :, NSA_Q_W:NSA_Q_W + 2 * NSA_KV_W], 1, CMP_BLOCK).reshape(n * CMP_BLOCK, 2 * NSA_KV_W)
    kc_new, vc_new = _compress_rows(new_rows, 0, W["cmp"], nblk=n)
    nb_pad = -(-(nb_past + 1) // 16) * 16
    kc = _pad_axis(jnp.concatenate([kc_past, kc_new[:, None]], axis=1), 1, nb_pad)
    vc = _pad_axis(jnp.concatenate([vc_past, vc_new[:, None]], axis=1), 1, nb_pad)
    end_rel = (jnp.arange(nb_pad, dtype=jnp.int32) + 1) * CMP_BLOCK - 1 - past
    kcp = _key_features(kc, end_rel)
    vct = vc.reshape(n, nb_pad, G, HEAD_DIM).transpose(0, 2, 3, 1).astype(BF16)
    heads = lambda a: a.reshape(n, t_new, G, HEAD_DIM).transpose(0, 2, 1, 3)
    new_k = lambda i: _pad_axis(heads(col(i)), 2, KC).astype(BF16)
    new_vt = lambda i: _pad_axis(heads(col(i)).transpose(0, 1, 3, 2), 3, KC).astype(BF16)
    n_real = NSA_REP * t_new
    qt = _query_features(pn3[:, :, :NSA_Q_W]).reshape(n, t_new, G, NSA_REP, LANES)
    qt = _pad_axis(qt.transpose(0, 2, 4, 3, 1).reshape(n, G, LANES, n_real), 3, LANES)
    gates = pn3[:, :, NSA_Q_W + 6 * NSA_KV_W:NSA_COLS].reshape(n, t_new, G, NSA_REP, 3)
    gates_t = _pad_axis(gates.transpose(0, 2, 4, 3, 1).reshape(n, G, 3, n_real), 3, LANES)
    o_c, imp = _nsa_sample_cmp(qt, kcp, vct, past=past, n_new=t_new)
    imp_all = imp[:, :, :, :t_new].transpose(1, 2, 0, 3).reshape(G, nb_pad, n * t_new)
    selb = _nsa_rank(imp_all, past=past, n_new=t_new).reshape(G, nb_pad, n, t_new).transpose(2, 0, 1, 3)
    selb = jnp.pad(jnp.tile(selb, (1, 1, 1, NSA_REP)), ((0, 0),) * 3 + ((0, LANES - n_real),),
                   constant_values=-jnp.inf)
    o_t = _nsa_sample(page_table, qt, o_c, selb, win_t, new_k(2), new_vt(3), new_k(4), new_vt(5), gates_t, cache_t,
                      past=past, n_new=t_new)
    o = o_t[:, :, :, :n_real].reshape(n, G, HEAD_DIM, NSA_REP, t_new)
    return o.transpose(0, 4, 1, 3, 2).reshape(n * t_new, NSA_Q_W)


def _layer(x, mkv, shift0, s0, W, nsa_fn):
    n, t, _ = x.shape
    m = n * t
    G = NSA_KV_HEADS
    x2 = x.reshape(m, D_MODEL)
    tm = _tile_rows(m, 512)
    x2 = _ffn(x2, *W["ffn1"], W["norm_out_g"], final_norm=False, tm=tm, tf=D_FF // 2)
    pn, pr, kf = _proj(x2, *W["proj"], t=t, tm=_tile_rows(m, 256))
    o_nsa = nsa_fn(pn, kf)
    o_rwkv, s_new = _rwkv_group(pr, shift0, s0, W, n, t)
    x2 = _outproj(x2, o_nsa, o_rwkv, *W["w_out"], tm=tm)
    x3 = _xattn(x2.reshape(n, t, D_MODEL), mkv, *W["xattn"], tm=_tile_rows(t, 512))
    y = _ffn(x3.reshape(m, D_MODEL), *W["ffn2"], W["norm_out_g"], final_norm=True, tm=tm, tf=D_FF // 2)
    pn3 = pn.reshape(n, t, NSA_PAD)
    kv_rows = pn3[:, :, NSA_Q_W:NSA_Q_W + 4 * NSA_KV_W].reshape(n, t, 4, G, HEAD_DIM)
    win_new = pn3[:, :, NSA_Q_W + 4 * NSA_KV_W:NSA_Q_W + 6 * NSA_KV_W].reshape(n, t, 2, G, HEAD_DIM)
    shift_new = pr.reshape(n, t, RWKV_COLS)[:, -1]
    return y.reshape(n, t, D_MODEL), kv_rows, win_new, s_new, shift_new


def kernel(x_prompt, x_sample, cache_nsa_kv, cache_nsa_win, state_rwkv_s, state_rwkv_shift, cache_mem_kv, page_table, mem_prompt, norm_ffn1_g, w_ffn1_gu, w_ffn1_down, norm_mix_g, w_in, w_out, q_norm_g, kc_norm_g, ks_norm_g, kw_norm_g, gate_b, cmp_pe_k, cmp_pe_v, w_cmp_k1, w_cmp_k2, w_cmp_v1, w_cmp_v2, rwkv_mu, rwkv_w0, w_decay2, rwkv_a0, w_iclr2, w_gate2, rwkv_k_k, rwkv_k_a, rwkv_r_k, lnx_g, lnx_b, norm_x_g, norm_mem_g, w_xq, w_xkv, xq_norm_g, xk_norm_g, w_xo, norm_ffn2_g, w_ffn2_gu, w_ffn2_down, norm_out_g):
    layer_weights = (norm_ffn1_g, w_ffn1_gu, w_ffn1_down, norm_mix_g, w_in, w_out, q_norm_g, kc_norm_g, ks_norm_g,
                     kw_norm_g, gate_b, cmp_pe_k, cmp_pe_v, w_cmp_k1, w_cmp_k2, w_cmp_v1, w_cmp_v2, rwkv_mu, rwkv_w0,
                     w_decay2, rwkv_a0, w_iclr2, w_gate2, rwkv_k_k, rwkv_k_a, rwkv_r_k, lnx_g, lnx_b, norm_x_g,
                     norm_mem_g, w_xq, w_xkv, xq_norm_g, xk_norm_g, w_xo, norm_ffn2_g, w_ffn2_gu, w_ffn2_down,
                     norm_out_g)
    assert w_in.shape[0] == 1, "single-layer trunk"
    W = _prep_weights(*(w[0] for w in layer_weights))
    n_p, t_p, _ = x_prompt.shape
    n_s, t_s, _ = x_sample.shape
    n_mem = mem_prompt.shape[1]
    xw = X_HEADS * X_HEAD_DIM

    mkv_p = _memkv(mem_prompt.reshape(n_p * n_mem, D_MODEL), *W["memkv"], tm=_tile_rows(n_p * n_mem, 512))
    mkv_p = mkv_p.reshape(n_p, n_mem, 2 * xw)
    y_p, kv_p, win_p, rs_p, sh_p = _layer(
        x_prompt, mkv_p, jnp.zeros((n_p, RWKV_COLS), F32),
        jnp.zeros((n_p, RWKV_HEADS, RWKV_HEAD_DIM, RWKV_HEAD_DIM), F32), W,
        lambda pn, kf: _nsa_prompt_group(pn, kf, W, n_p, t_p))
    win_p = win_p[:, t_p - min(WINDOW, t_p):]

    mkv_s = cache_mem_kv[0].reshape(n_s, n_mem, 2 * xw)
    y_s, kv_s, win_new, rs_s, sh_s = _layer(
        x_sample, mkv_s, state_rwkv_shift[0], state_rwkv_s[0], W,
        lambda pn, kf: _nsa_sample_group(pn, W, n_s, t_s, cache_nsa_kv[0], page_table, cache_nsa_win[0]))
    win_s = jnp.concatenate([cache_nsa_win[0], win_new], axis=1)[:, t_s:]

    mkv_out = mkv_p.reshape(1, n_p, n_mem, 2, X_HEADS, X_HEAD_DIM)
    return (y_p, y_s, kv_p[None], kv_s[None], win_p[None], win_s[None], rs_p[None], rs_s[None], sh_p[None],
            sh_s[None], mkv_out)
```

```python
import functools
import math

import jax
import jax.numpy as jnp
from jax import lax
from jax.experimental import pallas as pl
from jax.experimental.pallas import tpu as pltpu

F32 = jnp.float32
BF16 = jnp.bfloat16

D_MODEL = 1024
NSA_HEADS = 8
NSA_KV_HEADS = 2
NSA_REP = NSA_HEADS // NSA_KV_HEADS
HEAD_DIM = 64
CMP_BLOCK = 64
N_SEL = 16
WINDOW = 512
CMP_HIDDEN = 128
PAGE_SIZE = 128
RWKV_HEADS = 8
RWKV_HEAD_DIM = 64
RWKV_W = RWKV_HEADS * RWKV_HEAD_DIM
DECAY_LORA = 64
ICLR_LORA = 64
GATE_LORA = 128
NSA_Q_W = NSA_HEADS * HEAD_DIM
NSA_KV_W = NSA_KV_HEADS * HEAD_DIM
NSA_COLS = NSA_Q_W + 6 * NSA_KV_W + 3 * NSA_HEADS
RWKV_COLS = 3 * RWKV_W + DECAY_LORA + ICLR_LORA + GATE_LORA
X_HEADS = 4
X_HEAD_DIM = 128
D_FF = 2816
RMS_EPS = 1e-6
LNX_EPS = 64e-5
FORCED_SCORE = 1e9

LANES = 128
SUBLANES = 8
NSA_PAD = 1408
P_COLS = NSA_PAD + RWKV_COLS
TQ = 128
KC = 128
VMEM_LIMIT = 56 * 1024 * 1024

ALIBI = tuple(2.0 ** (-8.0 * (h + 1.0) / NSA_HEADS) for h in range(NSA_HEADS))


def _cparams(sem):
    return pltpu.CompilerParams(dimension_semantics=sem, vmem_limit_bytes=VMEM_LIMIT)


def _rms(x, g):
    return x * lax.rsqrt(jnp.mean(x * x, axis=-1, keepdims=True) + RMS_EPS) * g


def _seg_ones(width, seg):
    r = lax.broadcasted_iota(jnp.int32, (width, width), 0) // seg
    c = lax.broadcasted_iota(jnp.int32, (width, width), 1) // seg
    return (r == c).astype(F32)


def _seg_sum(x, seg):
    ones = _seg_ones(LANES, seg)
    parts = [jnp.dot(x[:, c:c + LANES], ones, precision=lax.Precision.HIGHEST, preferred_element_type=F32)
             for c in range(0, x.shape[1], LANES)]
    return parts[0] if len(parts) == 1 else jnp.concatenate(parts, axis=1)


def _seg_rms(x, g, seg):
    if seg == LANES:
        parts = [_rms(x[:, c:c + LANES], g[:, c:c + LANES]) for c in range(0, x.shape[1], LANES)]
        return parts[0] if len(parts) == 1 else jnp.concatenate(parts, axis=1)
    return x * lax.rsqrt(_seg_sum(x * x, seg) * (1.0 / seg) + RMS_EPS) * g


def _ffn_kernel(x_ref, g_ref, wg_ref, wu_ref, wd_ref, go_ref, o_ref, xn_ref, acc_ref, *, final_norm):
    f = pl.program_id(1)

    @pl.when(f == 0)
    def _():
        xn_ref[...] = _rms(x_ref[...], g_ref[...]).astype(BF16)
        acc_ref[...] = jnp.zeros_like(acc_ref)

    xn = xn_ref[...]
    gate = jnp.dot(xn, wg_ref[...], preferred_element_type=F32)
    up = jnp.dot(xn, wu_ref[...], preferred_element_type=F32)
    h = (gate * jax.nn.sigmoid(gate) * up).astype(BF16)
    acc_ref[...] += jnp.dot(h, wd_ref[...], preferred_element_type=F32)

    @pl.when(f == pl.num_programs(1) - 1)
    def _():
        y = x_ref[...] + 0.5 * acc_ref[...]
        if final_norm:
            y = _rms(y, go_ref[...])
        o_ref[...] = y


def _ffn(x, g, w_gu, w_down, g_out, *, final_norm, tm, tf):
    m = x.shape[0]
    nf = D_FF // tf
    return pl.pallas_call(
        functools.partial(_ffn_kernel, final_norm=final_norm),
        grid=(m // tm, nf),
        in_specs=[
            pl.BlockSpec((tm, D_MODEL), lambda i, f: (i, 0)),
            pl.BlockSpec((1, D_MODEL), lambda i, f: (0, 0)),
            pl.BlockSpec((D_MODEL, tf), lambda i, f: (0, f)),
            pl.BlockSpec((D_MODEL, tf), lambda i, f: (0, nf + f)),
            pl.BlockSpec((tf, D_MODEL), lambda i, f: (f, 0)),
            pl.BlockSpec((1, D_MODEL), lambda i, f: (0, 0)),
        ],
        out_specs=pl.BlockSpec((tm, D_MODEL), lambda i, f: (i, 0)),
        out_shape=jax.ShapeDtypeStruct((m, D_MODEL), F32),
        scratch_shapes=[pltpu.VMEM((tm, D_MODEL), BF16), pltpu.VMEM((tm, D_MODEL), F32)],
        compiler_params=_cparams(("parallel", "arbitrary")),
        name="ffn",
    )(x, g, w_gu, w_gu, w_down, g_out)


KF_COLS = 2 * NSA_KV_HEADS * LANES


def _proj_kernel(x_ref, g_ref, w_ref, hg_ref, gb_ref, place_ref, on_ref, or_ref, kf_ref, *, t):
    xn = _rms(x_ref[...], g_ref[...]).astype(BF16)
    p = jnp.dot(xn, w_ref[...], preferred_element_type=F32)
    or_ref[...] = p[:, NSA_PAD:]
    on_ref[...] = p[:, 0:NSA_PAD]
    hg = hg_ref[...]
    on_ref[:, 0:NSA_Q_W] = _seg_rms(p[:, 0:NSA_Q_W], hg[:, 0:NSA_Q_W], HEAD_DIM)
    keys = []
    for c in (NSA_Q_W + 2 * NSA_KV_W, NSA_Q_W + 4 * NSA_KV_W):
        keys.append(_seg_rms(p[:, c:c + NSA_KV_W], hg[:, c:c + NSA_KV_W], HEAD_DIM))
        on_ref[:, c:c + NSA_KV_W] = keys[-1]
    c = NSA_Q_W + 6 * NSA_KV_W
    on_ref[:, c:c + LANES] = jax.nn.sigmoid(p[:, c:c + LANES] + gb_ref[...])
    tm = p.shape[0]
    kf = (jnp.dot(keys[0].astype(BF16), place_ref[0], preferred_element_type=F32)
          + jnp.dot(keys[1].astype(BF16), place_ref[1], preferred_element_type=F32))
    pos = (pl.program_id(0) * tm + lax.broadcasted_iota(jnp.int32, (tm, KF_COLS), 0)) % t
    lane = lax.broadcasted_iota(jnp.int32, (tm, KF_COLS), 1) % LANES
    feat = jnp.where(lane == HEAD_DIM, pos // CMP_BLOCK, jnp.where(lane == HEAD_DIM + 1, pos % CMP_BLOCK, 0))
    kf_ref[...] = (kf + feat.astype(F32)).astype(BF16)


def _proj(x, g, w_pad, head_g, gate_b, place, *, t, tm):
    m = x.shape[0]
    return pl.pallas_call(
        functools.partial(_proj_kernel, t=t),
        grid=(m // tm,),
        in_specs=[
            pl.BlockSpec((tm, D_MODEL), lambda i: (i, 0)),
            pl.BlockSpec((1, D_MODEL), lambda i: (0, 0)),
            pl.BlockSpec((D_MODEL, P_COLS), lambda i: (0, 0)),
            pl.BlockSpec((1, NSA_PAD), lambda i: (0, 0)),
            pl.BlockSpec((1, LANES), lambda i: (0, 0)),
            pl.BlockSpec((2, NSA_KV_W, KF_COLS), lambda i: (0, 0, 0)),
        ],
        out_specs=[pl.BlockSpec((tm, NSA_PAD), lambda i: (i, 0)), pl.BlockSpec((tm, RWKV_COLS), lambda i: (i, 0)),
                   pl.BlockSpec((tm, KF_COLS), lambda i: (i, 0))],
        out_shape=[jax.ShapeDtypeStruct((m, NSA_PAD), F32), jax.ShapeDtypeStruct((m, RWKV_COLS), F32),
                   jax.ShapeDtypeStruct((m, KF_COLS), BF16)],
        compiler_params=_cparams(("parallel",)),
        name="proj",
    )(x, g, w_pad, head_g, gate_b, place)


def _compress_core(xk_ref, xv_ref, nblk, pe_ref, w1k_ref, w1v_ref, w2k_ref, w2v_ref, kcg_ref):
    def body(j, carry):
        acc_k, acc_v = carry
        pe = pe_ref[j]
        xk = xk_ref[pl.ds(j, nblk, stride=CMP_BLOCK), :] + pe[:, 0:LANES]
        xv = xv_ref[pl.ds(j, nblk, stride=CMP_BLOCK), :] + pe[:, LANES:2 * LANES]
        acc_k = acc_k + jnp.dot(xk.astype(BF16), w1k_ref[j], preferred_element_type=F32)
        acc_v = acc_v + jnp.dot(xv.astype(BF16), w1v_ref[j], preferred_element_type=F32)
        return acc_k, acc_v

    zero = jnp.zeros((nblk, 2 * CMP_HIDDEN), F32)
    acc_k, acc_v = lax.fori_loop(0, CMP_BLOCK, body, (zero, zero))
    hk = jax.nn.gelu(acc_k).astype(BF16)
    hv = jax.nn.gelu(acc_v).astype(BF16)
    kc = jnp.dot(hk, w2k_ref[...], preferred_element_type=F32)
    vc = jnp.dot(hv, w2v_ref[...], preferred_element_type=F32)
    return _seg_rms(kc, kcg_ref[...], HEAD_DIM), vc


def _compress_kernel(xk_ref, xv_ref, pe_ref, w1k_ref, w1v_ref, w2k_ref, w2v_ref, kcg_ref, kc_ref, vc_ref, *, nblk):
    kc, vc = _compress_core(xk_ref, xv_ref, nblk, pe_ref, w1k_ref, w1v_ref, w2k_ref, w2v_ref, kcg_ref)
    kc_ref[...] = kc
    vc_ref[...] = vc


def _cmp_weight_specs(imap):
    return [
        pl.BlockSpec((CMP_BLOCK, 1, 2 * LANES), imap(3)),
        pl.BlockSpec((CMP_BLOCK, LANES, 2 * CMP_HIDDEN), imap(3)),
        pl.BlockSpec((CMP_BLOCK, LANES, 2 * CMP_HIDDEN), imap(3)),
        pl.BlockSpec((2 * CMP_HIDDEN, LANES), imap(2)),
        pl.BlockSpec((2 * CMP_HIDDEN, LANES), imap(2)),
        pl.BlockSpec((1, LANES), imap(2)),
    ]


def _compress_rows(rows, col_block, cw, *, nblk):
    m = rows.shape[0]
    steps = m // (nblk * CMP_BLOCK)
    imap = lambda nd: (lambda i: (0,) * nd)
    return pl.pallas_call(
        functools.partial(_compress_kernel, nblk=nblk),
        grid=(steps,),
        in_specs=[pl.BlockSpec((nblk * CMP_BLOCK, LANES), lambda i: (i, col_block)),
                  pl.BlockSpec((nblk * CMP_BLOCK, LANES), lambda i: (i, col_block + 1))] + _cmp_weight_specs(imap),
        out_specs=[pl.BlockSpec((nblk, LANES), lambda i: (i, 0))] * 2,
        out_shape=[jax.ShapeDtypeStruct((steps * nblk, LANES), F32)] * 2,
        compiler_params=_cparams(("parallel",)),
        name="compress",
    )(rows, rows, *cw)


CMP_ROWS = 2 * NSA_KV_HEADS * HEAD_DIM
CMP_FEATS = 8


def _compress_paged_kernel(pt_ref, cache_ref, pe_ref, w1_ref, w2_ref, kcg_ref, kc_ref, vc_ref, buf_ref, sem, *,
                           n_pages):
    n = pl.program_id(0)
    slot = n % 2

    def page_copy(entry, buf_slot, i):
        return pltpu.make_async_copy(cache_ref.at[pt_ref[entry, i], pl.ds(0, CMP_ROWS), :],
                                     buf_ref.at[buf_slot, :, i, :], sem.at[buf_slot])

    def start_entry(entry, buf_slot):
        def start(i, c):
            page_copy(entry, buf_slot, i).start()
            return c
        lax.fori_loop(0, n_pages, start, 0)

    @pl.when(n == 0)
    def _():
        start_entry(0, 0)

    @pl.when(n + 1 < pl.num_programs(0))
    def _():
        start_entry(n + 1, 1 - slot)

    def wait(i, c):
        page_copy(n, slot, i).wait()
        return c

    lax.fori_loop(0, n_pages, wait, 0)
    blocks_per_page = PAGE_SIZE // CMP_BLOCK
    for c, out_ref in enumerate((kc_ref, vc_ref)):
        for g in range(NSA_KV_HEADS):
            def body(it, acc):
                d0 = pl.multiple_of(it * CMP_FEATS, CMP_FEATS)
                x = jnp.concatenate(
                    [buf_ref[slot, (c * NSA_KV_HEADS + g) * HEAD_DIM + d0 + u] + pe_ref[c * HEAD_DIM + d0 + u]
                     for u in range(CMP_FEATS)], axis=1)
                w = w1_ref[c, pl.ds(d0, CMP_FEATS)].reshape(CMP_FEATS * PAGE_SIZE, blocks_per_page * CMP_HIDDEN)
                return acc + jnp.dot(x.astype(BF16), w, preferred_element_type=F32)

            acc = lax.fori_loop(0, HEAD_DIM // CMP_FEATS, body,
                                jnp.zeros((n_pages, blocks_per_page * CMP_HIDDEN), F32))
            out = jnp.dot(jax.nn.gelu(acc).astype(BF16), w2_ref[c], preferred_element_type=F32)
            if c == 0:
                out = _seg_rms(out, kcg_ref[...], HEAD_DIM)
            out_ref[0, g] = out


def _compress_paged(page_table, cache_t, cw):
    nb, n_pages = page_table.shape
    width = (PAGE_SIZE // CMP_BLOCK) * HEAD_DIM
    const = lambda shape: pl.BlockSpec(shape, lambda n, pt: (0,) * len(shape))
    out = pl.BlockSpec((1, NSA_KV_HEADS, n_pages, width), lambda n, pt: (n, 0, 0, 0))
    return pl.pallas_call(
        functools.partial(_compress_paged_kernel, n_pages=n_pages),
        grid_spec=pltpu.PrefetchScalarGridSpec(
            num_scalar_prefetch=1,
            grid=(nb,),
            in_specs=[pl.BlockSpec(memory_space=pl.ANY)] + [const(w.shape) for w in cw],
            out_specs=[out, out],
            scratch_shapes=[pltpu.VMEM((2, CMP_ROWS, n_pages, PAGE_SIZE), F32), pltpu.SemaphoreType.DMA((2,))],
        ),
        out_shape=[jax.ShapeDtypeStruct((nb, NSA_KV_HEADS, n_pages, width), F32)] * 2,
        compiler_params=_cparams(("arbitrary",)),
        name="compress_paged",
    )(page_table, cache_t, *cw)


QW = NSA_REP * TQ
SEL_KC = 512
RANK_UNROLL = 8


def _wide_init(width=QW):
    return (jnp.full((1, width), -jnp.inf, F32), jnp.zeros((1, width), F32), jnp.zeros((HEAD_DIM, width), F32))


def _wide_step(state, s, vt=None, v_rows=None, group=0):
    m, l, acc = state
    m_new = jnp.maximum(m, jnp.max(s, axis=0, keepdims=True))
    m_safe = jnp.where(m_new == -jnp.inf, 0.0, m_new)
    alpha = jnp.exp(m - m_safe)
    p = jnp.exp(s - m_safe)
    l = alpha * l + jnp.sum(p, axis=0, keepdims=True)
    if vt is not None:
        pv = jnp.dot(vt, p.astype(BF16), preferred_element_type=F32)
    else:
        pv = lax.dot_general(v_rows, p.astype(BF16), (((0,), (0,)), ((), ())), preferred_element_type=F32)
        pv = pv[group * HEAD_DIM:(group + 1) * HEAD_DIM, :]
    return m_new, l, alpha * acc + pv


def _wide_out(state):
    _, l, acc = state
    return acc / jnp.maximum(l, 1e-30)


def _cmp_branch(qt, kcp, vct, t_row, n_blk):
    b_col = lax.broadcasted_iota(jnp.int32, (kcp.shape[0], qt.shape[1]), 0)
    valid = (t_row >= (b_col + 1) * CMP_BLOCK - 1) & (b_col < n_blk)
    s = jnp.where(valid, jnp.dot(kcp, qt, preferred_element_type=F32), -jnp.inf)
    m = jnp.max(s, axis=0, keepdims=True)
    e = jnp.exp(s - jnp.where(m == -jnp.inf, 0.0, m))
    p = e / jnp.maximum(jnp.sum(e, axis=0, keepdims=True), 1e-30)
    return jnp.dot(vct, p.astype(BF16), preferred_element_type=F32), p


def _select_blocks(imp, t_row, n_blk, score_ref, unrolled=False):
    nb_pad, w = imp.shape
    b_col = lax.broadcasted_iota(jnp.int32, (nb_pad, w), 0)
    cur = t_row // CMP_BLOCK
    forced = (b_col == 0) | (b_col == cur) | (b_col == cur - 1)
    score = jnp.where(forced, FORCED_SCORE, jnp.where(b_col <= cur, imp, -FORCED_SCORE))
    score = jnp.where(b_col < n_blk, score, -jnp.inf)
    score_ref[...] = score

    if unrolled:
        n_tiles = nb_pad // SUBLANES
        tile = lambda a, k: a[k * SUBLANES:(k + 1) * SUBLANES, :]
        cnt = [jnp.zeros((SUBLANES, w), F32) for _ in range(n_tiles)]
        for bp in range(n_blk):
            row = jnp.broadcast_to(score_ref[pl.ds(bp, 1), :], (SUBLANES, w))
            for k in range(n_tiles):
                sc = tile(score, k)
                if (k + 1) * SUBLANES - 1 < bp:
                    ahead = row > sc
                elif k * SUBLANES > bp:
                    ahead = row >= sc
                else:
                    ahead = (row > sc) | ((row == sc) & (tile(b_col, k) > bp))
                cnt[k] = cnt[k] + jnp.where(ahead, 1.0, 0.0)
        cnt = jnp.concatenate(cnt, axis=0)
        return jnp.where((cnt < N_SEL) & (b_col < n_blk), 0.0, -jnp.inf)

    def rank_body(it, cnt):
        base = pl.multiple_of(it * RANK_UNROLL, RANK_UNROLL)
        for u in range(RANK_UNROLL):
            bp = base + u
            row = jnp.broadcast_to(score_ref[pl.ds(bp, 1), :], (nb_pad, w))
            ahead = (row > score) | ((row == score) & (b_col > bp))
            cnt = cnt + jnp.where(ahead, 1.0, 0.0)
        return cnt

    cnt = lax.fori_loop(0, nb_pad // RANK_UNROLL, rank_body, jnp.zeros((nb_pad, w), F32))
    return jnp.where((cnt < N_SEL) & (b_col < n_blk), 0.0, -jnp.inf)


def _sum_lane_chunks(p):
    imp = p[:, 0:TQ]
    for r in range(1, NSA_REP):
        imp = imp + p[:, r * TQ:(r + 1) * TQ]
    return imp


def _nsa_prompt_kernel(q_ref, gate_ref, kcp_ref, vct_ref, ks0_ref, ks1_ref, vs_ref, kw0_ref, kw1_ref, vw_ref, o_ref,
                       score_ref, selb_ref, need_ref, *, n_blk):
    ks_refs, kw_refs = (ks0_ref, ks1_ref), (kw0_ref, kw1_ref)
    i = pl.program_id(1)
    t0 = i * TQ
    t_row = t0 + (lax.broadcasted_iota(jnp.int32, (1, QW), 1) & (TQ - 1))
    blocks_per_step = SEL_KC // CMP_BLOCK
    wk = WINDOW + TQ
    ws = pl.multiple_of(jnp.maximum(t0 - WINDOW, 0), TQ)
    dist = t_row - (ws + lax.broadcasted_iota(jnp.int32, (wk, QW), 0))
    wbias = jnp.where((dist >= 0) & (dist < WINDOW), 0.0, -jnp.inf)
    n_steps = (t0 + TQ + SEL_KC - 1) // SEL_KC

    q_t = (q_ref[...] * (HEAD_DIM ** -0.5)).T
    gate_t = gate_ref[...].T
    f_row = lax.broadcasted_iota(jnp.int32, (HEAD_DIM, TQ), 0)
    groups = range(NSA_KV_HEADS)
    qts, o_c = [], []
    for g in groups:
        cols = []
        for r in range(NSA_REP):
            h = g * NSA_REP + r
            feat = jnp.where(f_row == 0, ALIBI[h] * CMP_BLOCK, jnp.where(f_row == 1, ALIBI[h], 0.0))
            cols.append(jnp.concatenate([q_t[h * HEAD_DIM:(h + 1) * HEAD_DIM, :], feat], axis=0))
        qts.append(jnp.concatenate(cols, axis=1).astype(BF16))
        o, p = _cmp_branch(qts[g], kcp_ref[0, g], vct_ref[0, g], t_row, n_blk)
        o_c.append(o)
        selb = _select_blocks(_sum_lane_chunks(p), t_row[:, 0:TQ], n_blk, score_ref, unrolled=True)
        selb_ref[g] = jnp.concatenate([selb] * NSA_REP, axis=1)

    def step_scores(c, g):
        k0 = pl.multiple_of(c * SEL_KC, SEL_KC)
        s = jnp.dot(ks_refs[g][pl.ds(k0, SEL_KC), :], qts[g], preferred_element_type=F32)
        return jnp.concatenate(
            [s[b * CMP_BLOCK:(b + 1) * CMP_BLOCK, :] + selb_ref[g, pl.ds(c * blocks_per_step + b, 1), :]
             for b in range(blocks_per_step)], axis=0)

    def step_values(c):
        return vs_ref[pl.ds(pl.multiple_of(c * SEL_KC, SEL_KC), SEL_KC), :].astype(BF16)

    def sel_step(c, states):
        v = step_values(c)
        return tuple(_wide_step(states[g], step_scores(c, g), v_rows=v, group=g) for g in groups)

    for c in range(n_blk // blocks_per_step):
        rows = functools.reduce(jnp.maximum, [selb_ref[g, c * blocks_per_step:(c + 1) * blocks_per_step, 0:TQ]
                                              for g in groups])
        need_ref[c] = (jnp.max(rows) > -1.0).astype(jnp.int32)

    def sel_body(c, states):
        return lax.cond(need_ref[c] > 0, functools.partial(sel_step, c), lambda s: s, states)

    states = lax.fori_loop(0, n_steps - 1, sel_body, tuple(_wide_init() for _ in groups))
    c_last = n_steps - 1
    causal = c_last * SEL_KC + lax.broadcasted_iota(jnp.int32, (SEL_KC, QW), 0) <= t_row
    v = step_values(c_last)
    o_s = [_wide_out(_wide_step(states[g], jnp.where(causal, step_scores(c_last, g), -jnp.inf), v_rows=v, group=g))
           for g in groups]

    v = vw_ref[pl.ds(ws, wk), :].astype(BF16)
    o_w = [_wide_out(_wide_step(
        _wide_init(), jnp.dot(kw_refs[g][pl.ds(ws, wk), :], qts[g], preferred_element_type=F32) + wbias,
        v_rows=v, group=g)) for g in groups]

    outs = []
    for g in groups:
        for r in range(NSA_REP):
            lanes = slice(r * TQ, (r + 1) * TQ)
            row = (g * NSA_REP + r) * 3
            outs.append(gate_t[row:row + 1, :] * o_c[g][:, lanes] + gate_t[row + 1:row + 2, :] * o_s[g][:, lanes]
                        + gate_t[row + 2:row + 3, :] * o_w[g][:, lanes])
    o_ref[...] = jnp.concatenate(outs, axis=0).T


def _nsa_prompt(pn, kf, kcp, vct, *, n, t):
    G = NSA_KV_HEADS
    assert G == 2
    n_tiles = t // TQ
    n_blk = t // CMP_BLOCK
    col_block = lambda c: c // LANES
    full = lambda shape: pl.BlockSpec((1,) + shape, lambda b, i: (b,) + (0,) * len(shape))
    seq_cols = lambda c: pl.BlockSpec((t, LANES), lambda b, i: (b, col_block(c)))
    seq_slot = lambda s: pl.BlockSpec((t, LANES), lambda b, i: (b, s))
    return pl.pallas_call(
        functools.partial(_nsa_prompt_kernel, n_blk=n_blk),
        grid=(n, n_tiles),
        in_specs=[
            pl.BlockSpec((TQ, NSA_Q_W), lambda b, i: (b * n_tiles + i, 0)),
            pl.BlockSpec((TQ, LANES), lambda b, i: (b * n_tiles + i, col_block(NSA_Q_W + 6 * NSA_KV_W))),
            full((G, n_blk, LANES)),
            full((G, HEAD_DIM, n_blk)),
            seq_slot(0), seq_slot(1),
            seq_cols(NSA_Q_W + 3 * NSA_KV_W),
            seq_slot(2), seq_slot(3),
            seq_cols(NSA_Q_W + 5 * NSA_KV_W),
        ],
        out_specs=pl.BlockSpec((TQ, NSA_Q_W), lambda b, i: (b * n_tiles + i, 0)),
        out_shape=jax.ShapeDtypeStruct((n * t, NSA_Q_W), F32),
        scratch_shapes=[pltpu.VMEM((n_blk, TQ), F32), pltpu.VMEM((G, n_blk, QW), F32),
                        pltpu.SMEM((n_blk // (SEL_KC // CMP_BLOCK),), jnp.int32)],
        compiler_params=_cparams(("parallel", "arbitrary")),
        name="nsa_prompt",
    )(pn, pn, kcp, vct, kf, kf, pn, kf, kf, pn)


PAGES_PER_STEP = 4
PAGE_GROUP = 8


SEL_ROWS = 2 * NSA_KV_HEADS * HEAD_DIM


def _nsa_sample_cmp_kernel(qt_ref, kcp_ref, vct_ref, o_ref, imp_ref, *, past, n_new, n_blk):
    n_real = NSA_REP * n_new
    t_row = past + lax.broadcasted_iota(jnp.int32, (1, LANES), 1) % n_new
    li = lax.broadcasted_iota(jnp.int32, (LANES, LANES), 0)
    lj = lax.broadcasted_iota(jnp.int32, (LANES, LANES), 1)
    same_token = ((li % n_new == lj) & (li < n_real)).astype(F32)
    for g in range(NSA_KV_HEADS):
        o, p = _cmp_branch(qt_ref[0, g], kcp_ref[0, g], vct_ref[0, g], t_row, n_blk)
        o_ref[0, g] = o
        imp_ref[0, g] = jnp.dot(p, same_token, precision=lax.Precision.HIGHEST, preferred_element_type=F32)


def _nsa_sample_cmp(qt, kcp, vct, *, past, n_new):
    n, G, nb_pad = kcp.shape[:3]
    full = lambda shape: pl.BlockSpec((1,) + shape, lambda b: (b,) + (0,) * len(shape))
    return pl.pallas_call(
        functools.partial(_nsa_sample_cmp_kernel, past=past, n_new=n_new, n_blk=past // CMP_BLOCK + 1),
        grid=(n,),
        in_specs=[full((G, LANES, LANES)), full((G, nb_pad, LANES)), full((G, HEAD_DIM, nb_pad))],
        out_specs=[full((G, HEAD_DIM, LANES)), full((G, nb_pad, LANES))],
        out_shape=[jax.ShapeDtypeStruct((n, G, HEAD_DIM, LANES), F32),
                   jax.ShapeDtypeStruct((n, G, nb_pad, LANES), F32)],
        compiler_params=_cparams(("parallel",)),
        name="nsa_sample_cmp",
    )(qt, kcp, vct)


def _nsa_rank_kernel(imp_ref, selb_ref, score_ref, *, past, n_new, n_blk):
    t_row = past + lax.broadcasted_iota(jnp.int32, (1, imp_ref.shape[2]), 1) % n_new
    selb_ref[0] = _select_blocks(imp_ref[0], t_row, n_blk, score_ref)


def _nsa_rank(imp, *, past, n_new):
    G, nb_pad, w = imp.shape
    blk = pl.BlockSpec((1, nb_pad, w), lambda g: (g, 0, 0))
    return pl.pallas_call(
        functools.partial(_nsa_rank_kernel, past=past, n_new=n_new, n_blk=past // CMP_BLOCK + 1),
        grid=(G,),
        in_specs=[blk],
        out_specs=blk,
        out_shape=jax.ShapeDtypeStruct((G, nb_pad, w), F32),
        scratch_shapes=[pltpu.VMEM((nb_pad, w), F32)],
        compiler_params=_cparams(("parallel",)),
        name="nsa_rank",
    )(imp)


def _nsa_sample_kernel(pt_ref, qt_ref, oc_ref, selb_ref, win_ref, knew_ref, vnewt_ref, kwnew_ref, vwnewt_ref, gt_ref,
                       cache_ref, o_ref, buf_ref, slot_ref, sem, *, past, n_new, n_pages):
    n = pl.program_id(0)
    G = NSA_KV_HEADS
    n_real = NSA_REP * n_new
    blocks_per_page = PAGE_SIZE // CMP_BLOCK
    nb_past = past // CMP_BLOCK
    w_buf = min(WINDOW, past)
    lane = lax.broadcasted_iota(jnp.int32, (1, LANES), 1)
    t_row = past + lane % n_new
    real = lane < n_real
    selb_ref = selb_ref.at[0]

    def page_copy(lp, slot):
        return pltpu.make_async_copy(cache_ref.at[pt_ref[n, lp], pl.ds(CMP_ROWS, SEL_ROWS), :],
                                     buf_ref.at[pl.ds(slot * SEL_ROWS, SEL_ROWS), :], sem)

    def wanted_rows(first, count):
        rows = jnp.maximum(selb_ref[0, pl.ds(first, count), :], selb_ref[1, pl.ds(first, count), :])
        return jnp.max(jnp.where(real, rows, -jnp.inf)) > -1.0

    def page_body(lp, cnt):
        wanted = wanted_rows(lp * blocks_per_page, 1)
        for b in range(1, blocks_per_page):
            wanted = wanted | wanted_rows(lp * blocks_per_page + b, 1)

        @pl.when(wanted)
        def _():
            page_copy(lp, cnt).start()
            slot_ref[cnt] = lp

        return cnt + wanted.astype(jnp.int32)

    def group_body(pg, cnt):
        first = pl.multiple_of(pg * (PAGE_GROUP * blocks_per_page), PAGE_GROUP * blocks_per_page)
        return lax.cond(wanted_rows(first, PAGE_GROUP * blocks_per_page),
                        lambda c: lax.fori_loop(pg * PAGE_GROUP, (pg + 1) * PAGE_GROUP, page_body, c),
                        lambda c: c, cnt)

    n_slots = lax.fori_loop(0, n_pages // PAGE_GROUP, group_body, jnp.int32(0))
    n_steps = (n_slots + PAGES_PER_STEP - 1) // PAGES_PER_STEP

    def fill_body(j, c):
        slot = n_slots + j

        @pl.when(slot < n_steps * PAGES_PER_STEP)
        def _():
            page_copy(0, slot).start()
            slot_ref[slot] = -1

        return c

    lax.fori_loop(0, PAGES_PER_STEP - 1, fill_body, 0)

    def wait_body(s, c):
        page_copy(jnp.maximum(slot_ref[s], 0), s).wait()
        return c

    lax.fori_loop(0, n_steps * PAGES_PER_STEP, wait_body, 0)

    k_idx = lax.broadcasted_iota(jnp.int32, (PAGE_SIZE, LANES), 0)
    new_idx = lax.broadcasted_iota(jnp.int32, (knew_ref.shape[2], LANES), 0)
    pos_new = past + new_idx
    pos_win = past - w_buf + lax.broadcasted_iota(jnp.int32, (w_buf, LANES), 0)
    tn_dims = (((0,), (0,)), ((), ()))

    for g in range(G):
        q64 = qt_ref[0, g][0:HEAD_DIM, :]
        slope = jnp.zeros((1, LANES), F32)
        for r in range(NSA_REP):
            slope = jnp.where((lane >= r * n_new) & (lane < (r + 1) * n_new), ALIBI[g * NSA_REP + r], slope)

        def alibi(s, pos):
            return s - slope * (t_row - pos).astype(F32)

        def sel_body(st, state):
            kts, vts, pos, bias = [], [], [], []
            for j in range(PAGES_PER_STEP):
                slot = st * PAGES_PER_STEP + j
                base = pl.multiple_of(slot * SEL_ROWS, SEL_ROWS)
                kts.append(buf_ref[pl.ds(base + g * HEAD_DIM, HEAD_DIM), :])
                vts.append(buf_ref[pl.ds(base + (NSA_KV_HEADS + g) * HEAD_DIM, HEAD_DIM), :])
                lp = slot_ref[slot]
                live = jnp.where(lp >= 0, 0.0, -jnp.inf)
                lp = jnp.maximum(lp, 0)
                pos.append(lp * PAGE_SIZE + k_idx)
                bias += [jnp.broadcast_to(selb_ref[g, pl.ds(lp * blocks_per_page + b, 1), :] + live,
                                          (CMP_BLOCK, LANES)) for b in range(blocks_per_page)]
            kt = jnp.concatenate(kts, axis=1).astype(BF16)
            s = lax.dot_general(kt, q64, tn_dims, preferred_element_type=F32)
            s = alibi(s, jnp.concatenate(pos, axis=0)) + jnp.concatenate(bias, axis=0)
            return _wide_step(state, s, jnp.concatenate(vts, axis=1).astype(BF16))

        state = lax.fori_loop(0, n_steps, sel_body, _wide_init(LANES))
        s = alibi(jnp.dot(knew_ref[0, g], q64, preferred_element_type=F32), pos_new)
        s = jnp.where(pos_new <= t_row, s + selb_ref[g, pl.ds(nb_past, 1), :], -jnp.inf)
        o_s = _wide_out(_wide_step(state, s, vnewt_ref[0, g]))

        s = lax.dot_general(win_ref[0, 0, g].astype(BF16), q64, tn_dims, preferred_element_type=F32)
        s = jnp.where(t_row - pos_win < WINDOW, alibi(s, pos_win), -jnp.inf)
        state = _wide_step(_wide_init(LANES), s, win_ref[0, 1, g].astype(BF16))
        s = alibi(jnp.dot(kwnew_ref[0, g], q64, preferred_element_type=F32), pos_new)
        s = jnp.where((pos_new <= t_row) & (new_idx < n_new), s, -jnp.inf)
        o_w = _wide_out(_wide_step(state, s, vwnewt_ref[0, g]))

        o_ref[0, g] = (gt_ref[0, g, pl.ds(0, 1), :] * oc_ref[0, g] + gt_ref[0, g, pl.ds(1, 1), :] * o_s
                       + gt_ref[0, g, pl.ds(2, 1), :] * o_w)


def _nsa_sample(page_table, qt, o_c, selb, win_t, knew, vnewt, kwnew, vwnewt, gates_t, cache_t, *, past, n_new):
    n = qt.shape[0]
    G = NSA_KV_HEADS
    n_pages = page_table.shape[1]
    assert n_pages % PAGE_GROUP == 0
    nb_pad = selb.shape[2]
    w_buf = win_t.shape[4]
    assert w_buf == min(WINDOW, past)
    n_newp = knew.shape[2]
    full = lambda shape: pl.BlockSpec((1,) + shape, lambda b, pt: (b,) + (0,) * len(shape))
    kern = functools.partial(_nsa_sample_kernel, past=past, n_new=n_new, n_pages=n_pages)
    return pl.pallas_call(
        kern,
        grid_spec=pltpu.PrefetchScalarGridSpec(
            num_scalar_prefetch=1,
            grid=(n,),
            in_specs=[
                full((G, LANES, LANES)),
                full((G, HEAD_DIM, LANES)),
                full((G, nb_pad, LANES)),
                full((2, G, HEAD_DIM, w_buf)),
                full((G, n_newp, HEAD_DIM)),
                full((G, HEAD_DIM, n_newp)),
                full((G, n_newp, HEAD_DIM)),
                full((G, HEAD_DIM, n_newp)),
                full((G, 3, LANES)),
                pl.BlockSpec(memory_space=pl.ANY),
            ],
            out_specs=full((G, HEAD_DIM, LANES)),
            scratch_shapes=[
                pltpu.VMEM((n_pages * SEL_ROWS, PAGE_SIZE), F32),
                pltpu.SMEM((n_pages + PAGES_PER_STEP,), jnp.int32),
                pltpu.SemaphoreType.DMA(()),
            ],
        ),
        out_shape=jax.ShapeDtypeStruct((n, G, HEAD_DIM, LANES), F32),
        compiler_params=_cparams(("arbitrary",)),
        name="nsa_sample",
    )(page_table, qt, o_c, selb, win_t, knew, vnewt, kwnew, vwnewt, gates_t, cache_t)


def _alibi_features(pos):
    blk = (pos // CMP_BLOCK).astype(F32)
    off = (pos % CMP_BLOCK).astype(F32)
    pad = jnp.zeros(pos.shape + (HEAD_DIM - 2,), F32)
    return jnp.concatenate([blk[..., None], off[..., None], pad], axis=-1)


def _query_features(q):
    n, t, _ = q.shape
    qh = q.reshape(n, t, NSA_HEADS, HEAD_DIM) * (HEAD_DIM ** -0.5)
    slope = jnp.asarray(ALIBI, F32)
    feat = jnp.zeros((NSA_HEADS, HEAD_DIM), F32).at[:, 0].set(slope * CMP_BLOCK).at[:, 1].set(slope)
    feat = jnp.broadcast_to(feat, (n, t, NSA_HEADS, HEAD_DIM))
    return jnp.concatenate([qh, feat], axis=-1).reshape(n, t, NSA_HEADS * LANES).astype(BF16)


def _key_features(k, pos):
    n, l, _ = k.shape
    kh = k.reshape(n, l, NSA_KV_HEADS, HEAD_DIM).transpose(0, 2, 1, 3)
    feat = jnp.broadcast_to(_alibi_features(pos), (n, NSA_KV_HEADS, l, HEAD_DIM))
    return jnp.concatenate([kh, feat], axis=-1).astype(BF16)


def _rwkv_pre_kernel(p_ref, before_ref, shift_ref, mu_ref, w0_ref, wd_ref, a0_ref, wa_ref, wg_ref, kk_ref, ka_ref,
                     rk_ref, r_o, d_o, k_o, v_o, kk_o, b_o, g_o, bonus_o):
    p = p_ref[...]
    first = jnp.where(pl.program_id(1) == 0, shift_ref[0], before_ref[7:8, :])
    row = lax.broadcasted_iota(jnp.int32, p.shape, 0)
    prev = jnp.where(row == 0, first, pltpu.roll(p, 1, axis=0))
    xs = p + (prev - p) * mu_ref[...]
    W = RWKV_W
    r, k, v = xs[:, 0:W], xs[:, W:2 * W], xs[:, 2 * W:3 * W]
    lora = xs[:, 3 * W:3 * W + LANES]
    xg = xs[:, 3 * W + LANES:3 * W + 2 * LANES]
    z = w0_ref[...] + jnp.dot(jnp.tanh(lora).astype(BF16), wd_ref[...], preferred_element_type=F32)
    nz = -z
    softplus = jnp.maximum(nz, 0.0) + jnp.log(1.0 + jnp.exp(-jnp.abs(nz)))
    decay = jnp.exp(-jnp.exp(-softplus - 0.5))
    a = jax.nn.sigmoid(a0_ref[...] + jnp.dot(lora.astype(BF16), wa_ref[...], preferred_element_type=F32))
    g = jnp.dot(jax.nn.sigmoid(xg).astype(BF16), wg_ref[...], preferred_element_type=F32)
    kk = k * kk_ref[...]
    kk = kk * lax.rsqrt(jnp.maximum(_seg_sum(kk * kk, RWKV_HEAD_DIM), 1e-24))
    k_h = k * (1.0 + (a - 1.0) * ka_ref[...])
    r_o[...] = r
    d_o[...] = decay
    k_o[...] = k_h
    v_o[...] = v
    kk_o[...] = kk
    b_o[...] = kk * a
    g_o[...] = g
    bonus_o[...] = _seg_sum(r * k_h * rk_ref[...], RWKV_HEAD_DIM) * v


def _rwkv_pre(p, shift0, rw, *, n, t, tm):
    m = n * t
    assert tm % 8 == 0 and t % tm == 0
    tiles = t // tm
    row = lambda w: pl.BlockSpec((1, w), lambda b, i: (0, 0))
    mat = lambda a, c: pl.BlockSpec((a, c), lambda b, i: (0, 0))
    out = pl.BlockSpec((tm, RWKV_W), lambda b, i: (b * tiles + i, 0))
    return pl.pallas_call(
        _rwkv_pre_kernel,
        grid=(n, tiles),
        in_specs=[
            pl.BlockSpec((tm, RWKV_COLS), lambda b, i: (b * tiles + i, 0)),
            pl.BlockSpec((8, RWKV_COLS), lambda b, i: (jnp.maximum((b * tiles + i) * (tm // 8) - 1, 0), 0)),
            pl.BlockSpec((1, 1, RWKV_COLS), lambda b, i: (b, 0, 0)),
            row(RWKV_COLS), row(RWKV_W), mat(LANES, RWKV_W), row(RWKV_W), mat(LANES, RWKV_W),
            mat(GATE_LORA, RWKV_W), row(RWKV_W), row(RWKV_W), row(RWKV_W),
        ],
        out_specs=[out] * 8,
        out_shape=[jax.ShapeDtypeStruct((m, RWKV_W), F32)] * 8,
        compiler_params=_cparams(("parallel", "arbitrary")),
        name="rwkv_pre",
    )(p, p, shift0.reshape(n, 1, RWKV_COLS), *rw)


RW_J = RWKV_HEAD_DIM // 2
RW_PAIRS = LANES // 2


def _rwkv_scan_kernel(r_ref, d_ref, k_ref, kk_ref, b_ref, v_ref, s0_ref, o_ref, sout_ref, s_ref, *, tc):
    c = pl.program_id(1)

    @pl.when(c == 0)
    def _():
        s_ref[...] = s0_ref[0]

    def both_halves(x):
        return x + pltpu.roll(x, RW_PAIRS, axis=1)

    tiles = [pl.ds(k * SUBLANES, SUBLANES) for k in range(RWKV_HEAD_DIM // SUBLANES)]

    def key_row(ref, t, j):
        return jnp.broadcast_to(ref[0, t, pl.ds(j, 1), :], (SUBLANES, LANES))

    def step(t, sa_parts):
        t_next = jnp.minimum(t + 1, tc - 1)
        sa_next = []
        for i, rows in enumerate(tiles):
            u = -both_halves(sa_parts[i])
            vt = v_ref[0, t, rows, :]
            o = jnp.zeros((SUBLANES, LANES), F32)
            sa = jnp.zeros((SUBLANES, LANES), F32)
            for j in range(RW_J):
                h = (s_ref[j, rows, :] * key_row(d_ref, t, j) + u * key_row(b_ref, t, j)
                     + vt * key_row(k_ref, t, j))
                s_ref[j, rows, :] = h
                o = o + h * key_row(r_ref, t, j)
                sa = sa + h * key_row(kk_ref, t_next, j)
            o_ref[0, t, rows, :] = both_halves(o)
            sa_next.append(sa)
        return tuple(sa_next)

    sa0 = []
    for rows in tiles:
        sa = jnp.zeros((SUBLANES, LANES), F32)
        for j in range(RW_J):
            sa = sa + s_ref[j, rows, :] * key_row(kk_ref, 0, j)
        sa0.append(sa)
    lax.fori_loop(0, tc, step, tuple(sa0))
    sout_ref[0] = s_ref[...]


def _rwkv_scan(r, d, k, kk, b, v, s0, *, tc):
    pg, t = r.shape[:2]
    vec = pl.BlockSpec((1, tc, RW_J, LANES), lambda g, c: (g, c, 0, 0))
    val = pl.BlockSpec((1, tc, RWKV_HEAD_DIM, LANES), lambda g, c: (g, c, 0, 0))
    st = pl.BlockSpec((1, RW_J, RWKV_HEAD_DIM, LANES), lambda g, c: (g, 0, 0, 0))
    return pl.pallas_call(
        functools.partial(_rwkv_scan_kernel, tc=tc),
        grid=(pg, t // tc),
        in_specs=[vec, vec, vec, vec, vec, val, st],
        out_specs=[val, st],
        out_shape=[jax.ShapeDtypeStruct((pg, t, RWKV_HEAD_DIM, LANES), F32),
                   jax.ShapeDtypeStruct((pg, RW_J, RWKV_HEAD_DIM, LANES), F32)],
        scratch_shapes=[pltpu.VMEM((RW_J, RWKV_HEAD_DIM, LANES), F32)],
        compiler_params=_cparams(("parallel", "arbitrary")),
        name="rwkv_scan",
    )(r, d, k, kk, b, v, s0)


def _rwkv_post_kernel(o_ref, bonus_ref, g_ref, lg_ref, lb_ref, out_ref):
    o = o_ref[...]
    inv = 1.0 / RWKV_HEAD_DIM
    mean = _seg_sum(o, RWKV_HEAD_DIM) * inv
    cen = o - mean
    var = _seg_sum(cen * cen, RWKV_HEAD_DIM) * inv
    y = cen * lax.rsqrt(var + LNX_EPS) * lg_ref[...] + lb_ref[...]
    out_ref[...] = (y + bonus_ref[...]) * g_ref[...]


def _rwkv_post(o, bonus, g, lnx_g, lnx_b, *, tm):
    m = o.shape[0]
    blk = pl.BlockSpec((tm, RWKV_W), lambda i: (i, 0))
    row = pl.BlockSpec((1, RWKV_W), lambda i: (0, 0))
    return pl.pallas_call(
        _rwkv_post_kernel,
        grid=(m // tm,),
        in_specs=[blk, blk, blk, row, row],
        out_specs=blk,
        out_shape=jax.ShapeDtypeStruct((m, RWKV_W), F32),
        compiler_params=_cparams(("parallel",)),
        name="rwkv_post",
    )(o, bonus, g, lnx_g, lnx_b)


def _pairs_layout(x, n, t):
    pg = n * RWKV_HEADS // RW_PAIRS
    y = x.reshape(n, t, RWKV_HEADS, 2, RW_J).transpose(1, 4, 3, 0, 2).reshape(t, RW_J, 2, pg, RW_PAIRS)
    return y.transpose(3, 0, 1, 2, 4).reshape(pg, t, RW_J, LANES)


def _pairs_layout_v(x, n, t):
    pg = n * RWKV_HEADS // RW_PAIRS
    y = x.reshape(n, t, RWKV_HEADS, RWKV_HEAD_DIM).transpose(1, 3, 0, 2).reshape(t, RWKV_HEAD_DIM, pg, RW_PAIRS)
    y = y.transpose(2, 0, 1, 3)
    return jnp.concatenate([y, y], axis=-1)


def _pairs_unlayout_v(y, n, t):
    pg = y.shape[0]
    z = y[..., :RW_PAIRS].transpose(1, 2, 0, 3).reshape(t, RWKV_HEAD_DIM, n, RWKV_HEADS)
    return z.transpose(2, 0, 3, 1).reshape(n * t, RWKV_W)


def _state_layout(s0):
    n = s0.shape[0]
    pg = n * RWKV_HEADS // RW_PAIRS
    y = s0.reshape(pg, RW_PAIRS, RWKV_HEAD_DIM, 2, RW_J)
    return y.transpose(0, 4, 2, 3, 1).reshape(pg, RW_J, RWKV_HEAD_DIM, LANES)


def _state_unlayout(y, n):
    pg = y.shape[0]
    z = y.reshape(pg, RW_J, RWKV_HEAD_DIM, 2, RW_PAIRS).transpose(0, 4, 2, 3, 1)
    return z.reshape(n, RWKV_HEADS, RWKV_HEAD_DIM, RWKV_HEAD_DIM)


def _outproj_kernel(x_ref, a_ref, b_ref, wa_ref, wb_ref, o_ref):
    y = jnp.dot(a_ref[...].astype(BF16), wa_ref[...], preferred_element_type=F32)
    y = y + jnp.dot(b_ref[...].astype(BF16), wb_ref[...], preferred_element_type=F32)
    o_ref[...] = x_ref[...] + y


def _outproj(x, a, b, wa, wb, *, tm):
    m = x.shape[0]
    return pl.pallas_call(
        _outproj_kernel,
        grid=(m // tm,),
        in_specs=[
            pl.BlockSpec((tm, D_MODEL), lambda i: (i, 0)),
            pl.BlockSpec((tm, a.shape[1]), lambda i: (i, 0)),
            pl.BlockSpec((tm, b.shape[1]), lambda i: (i, 0)),
            pl.BlockSpec(wa.shape, lambda i: (0, 0)),
            pl.BlockSpec(wb.shape, lambda i: (0, 0)),
        ],
        out_specs=pl.BlockSpec((tm, D_MODEL), lambda i: (i, 0)),
        out_shape=jax.ShapeDtypeStruct((m, D_MODEL), F32),
        compiler_params=_cparams(("parallel",)),
        name="outproj",
    )(x, a, b, wa, wb)


def _memkv_kernel(x_ref, g_ref, w_ref, kg_ref, o_ref):
    xn = _rms(x_ref[...], g_ref[...]).astype(BF16)
    kv = jnp.dot(xn, w_ref[...], preferred_element_type=F32)
    xw = X_HEADS * X_HEAD_DIM
    o_ref[:, 0:xw] = _seg_rms(kv[:, 0:xw], kg_ref[...], X_HEAD_DIM)
    o_ref[:, xw:2 * xw] = kv[:, xw:2 * xw]


def _memkv(mem, g, w, kg, *, tm):
    m = mem.shape[0]
    xw = X_HEADS * X_HEAD_DIM
    return pl.pallas_call(
        _memkv_kernel,
        grid=(m // tm,),
        in_specs=[
            pl.BlockSpec((tm, D_MODEL), lambda i: (i, 0)),
            pl.BlockSpec((1, D_MODEL), lambda i: (0, 0)),
            pl.BlockSpec((D_MODEL, 2 * xw), lambda i: (0, 0)),
            pl.BlockSpec((1, xw), lambda i: (0, 0)),
        ],
        out_specs=pl.BlockSpec((tm, 2 * xw), lambda i: (i, 0)),
        out_shape=jax.ShapeDtypeStruct((m, 2 * xw), F32),
        compiler_params=_cparams(("parallel",)),
        name="memkv",
    )(mem, g, w, kg)


def _xattn_kernel(x_ref, mkv_ref, g_ref, wq_ref, qg_ref, wo_ref, o_ref):
    x = x_ref[0]
    xn = _rms(x, g_ref[...]).astype(BF16)
    q = jnp.dot(xn, wq_ref[...], preferred_element_type=F32)
    q = _seg_rms(q, qg_ref[...], X_HEAD_DIM) * (X_HEAD_DIM ** -0.5)
    xw = X_HEADS * X_HEAD_DIM
    outs = []
    for h in range(X_HEADS):
        lo = h * X_HEAD_DIM
        k = mkv_ref[0, :, lo:lo + X_HEAD_DIM].astype(BF16)
        v = mkv_ref[0, :, xw + lo:xw + lo + X_HEAD_DIM].astype(BF16)
        s = lax.dot_general(q[:, lo:lo + X_HEAD_DIM].astype(BF16), k, (((1,), (1,)), ((), ())),
                            preferred_element_type=F32)
        e = jnp.exp(s - jnp.max(s, axis=-1, keepdims=True))
        p = e / jnp.sum(e, axis=-1, keepdims=True)
        outs.append(jnp.dot(p.astype(BF16), v, preferred_element_type=F32))
    o = jnp.concatenate(outs, axis=-1).astype(BF16)
    o_ref[0] = x + jnp.dot(o, wo_ref[...], preferred_element_type=F32)


def _xattn(x, mkv, g, wq, qg, wo, *, tm):
    n, t, _ = x.shape
    xw = X_HEADS * X_HEAD_DIM
    n_mem = mkv.shape[1]
    return pl.pallas_call(
        _xattn_kernel,
        grid=(n, t // tm),
        in_specs=[
            pl.BlockSpec((1, tm, D_MODEL), lambda b, i: (b, i, 0)),
            pl.BlockSpec((1, n_mem, 2 * xw), lambda b, i: (b, 0, 0)),
            pl.BlockSpec((1, D_MODEL), lambda b, i: (0, 0)),
            pl.BlockSpec((D_MODEL, xw), lambda b, i: (0, 0)),
            pl.BlockSpec((1, xw), lambda b, i: (0, 0)),
            pl.BlockSpec((xw, D_MODEL), lambda b, i: (0, 0)),
        ],
        out_specs=pl.BlockSpec((1, tm, D_MODEL), lambda b, i: (b, i, 0)),
        out_shape=jax.ShapeDtypeStruct((n, t, D_MODEL), F32),
        compiler_params=_cparams(("parallel", "parallel")),
        name="xattn",
    )(x, mkv, g, wq, qg, wo)


def _row(v):
    return v.reshape(1, -1).astype(F32)


def _block_diag2(w):
    z = jnp.zeros_like(w)
    return jnp.concatenate([jnp.concatenate([w, z], axis=-1), jnp.concatenate([z, w], axis=-1)], axis=-2)


def _prep_weights(norm_ffn1_g, w_ffn1_gu, w_ffn1_down, norm_mix_g, w_in, w_out, q_norm_g, kc_norm_g, ks_norm_g,
                  kw_norm_g, gate_b, cmp_pe_k, cmp_pe_v, w_cmp_k1, w_cmp_k2, w_cmp_v1, w_cmp_v2, rwkv_mu, rwkv_w0,
                  w_decay2, rwkv_a0, w_iclr2, w_gate2, rwkv_k_k, rwkv_k_a, rwkv_r_k, lnx_g, lnx_b, norm_x_g,
                  norm_mem_g, w_xq, w_xkv, xq_norm_g, xk_norm_g, w_xo, norm_ffn2_g, w_ffn2_gu, w_ffn2_down,
                  norm_out_g):
    G = NSA_KV_HEADS
    W = {}
    W["ffn1"] = (_row(norm_ffn1_g), w_ffn1_gu.astype(BF16), w_ffn1_down.astype(BF16))
    W["ffn2"] = (_row(norm_ffn2_g), w_ffn2_gu.astype(BF16), w_ffn2_down.astype(BF16))
    W["norm_out_g"] = _row(norm_out_g)
    w_pad = jnp.concatenate([w_in[:, :NSA_COLS], jnp.zeros((D_MODEL, NSA_PAD - NSA_COLS), F32), w_in[:, NSA_COLS:]],
                            axis=1).astype(BF16)
    head_g = jnp.zeros((NSA_PAD,), F32)
    head_g = head_g.at[0:NSA_Q_W].set(jnp.tile(q_norm_g, NSA_HEADS))
    c = NSA_Q_W + 2 * NSA_KV_W
    head_g = head_g.at[c:c + NSA_KV_W].set(jnp.tile(ks_norm_g, G))
    c = NSA_Q_W + 4 * NSA_KV_W
    head_g = head_g.at[c:c + NSA_KV_W].set(jnp.tile(kw_norm_g, G))
    gb = jnp.zeros((LANES,), F32).at[0:3 * NSA_HEADS].set(gate_b)
    src = jnp.arange(NSA_KV_W)
    place = jnp.stack([jax.nn.one_hot((b * G + src // HEAD_DIM) * LANES + src % HEAD_DIM, KF_COLS, dtype=BF16)
                       for b in range(2)])
    W["proj"] = (_row(norm_mix_g), w_pad, _row(head_g), _row(gb), place)
    pe = jnp.concatenate([cmp_pe_k, cmp_pe_k, cmp_pe_v, cmp_pe_v], axis=-1)[:, None, :]
    W["cmp"] = (pe, _block_diag2(w_cmp_k1).astype(BF16), _block_diag2(w_cmp_v1).astype(BF16),
                _block_diag2(w_cmp_k2).astype(BF16), _block_diag2(w_cmp_v2).astype(BF16),
                _row(jnp.tile(kc_norm_g, G)))
    pe_t = jnp.concatenate([jnp.tile(cmp_pe_k.T, (1, 2)), jnp.tile(cmp_pe_v.T, (1, 2))], axis=0)[:, None, :]
    w1_t = jnp.stack([_block_diag2(w_cmp_k1.transpose(1, 0, 2)), _block_diag2(w_cmp_v1.transpose(1, 0, 2))])
    w2_t = jnp.stack([_block_diag2(w_cmp_k2), _block_diag2(w_cmp_v2)])
    W["cmp_paged"] = (pe_t, w1_t.astype(BF16), w2_t.astype(BF16), _row(jnp.tile(kc_norm_g, G)))
    zl = jnp.zeros((DECAY_LORA, RWKV_W), F32)
    W["rwkv_pre"] = (_row(rwkv_mu), _row(rwkv_w0), jnp.concatenate([w_decay2, zl], axis=0).astype(BF16),
                     _row(rwkv_a0), jnp.concatenate([zl, w_iclr2], axis=0).astype(BF16), w_gate2.astype(BF16),
                     _row(rwkv_k_k), _row(rwkv_k_a), _row(rwkv_r_k))
    W["rwkv_post"] = (_row(lnx_g), _row(lnx_b))
    W["w_out"] = (w_out[:NSA_Q_W].astype(BF16), w_out[NSA_Q_W:].astype(BF16))
    W["xattn"] = (_row(norm_x_g), w_xq.astype(BF16), _row(jnp.tile(xq_norm_g, X_HEADS)), w_xo.astype(BF16))
    W["memkv"] = (_row(norm_mem_g), w_xkv.astype(BF16), _row(jnp.tile(xk_norm_g, X_HEADS)))
    return W


def _tile_rows(m, pref):
    return pref if m % pref == 0 else m


def _rwkv_group(pr, shift0, s0, W, n, t):
    m = n * t
    r, d, k, v, kk, b, g, bonus = _rwkv_pre(pr, shift0, W["rwkv_pre"], n=n, t=t, tm=_tile_rows(t, 256))
    lay = lambda a: _pairs_layout(a, n, t)
    o, s_new = _rwkv_scan(lay(r), lay(d), lay(k), lay(kk), lay(b), _pairs_layout_v(v, n, t), _state_layout(s0),
                          tc=min(t, 32))
    o = _pairs_unlayout_v(o, n, t)
    o = _rwkv_post(o, bonus, g, *W["rwkv_post"], tm=_tile_rows(m, 512))
    return o, _state_unlayout(s_new, n)


def _nsa_prompt_group(pn, kf, W, n, t):
    G = NSA_KV_HEADS
    n_blk = t // CMP_BLOCK
    kc, vc = _compress_rows(pn, NSA_Q_W // LANES, W["cmp"], nblk=n_blk)
    end = (jnp.arange(n_blk, dtype=jnp.int32) + 1) * CMP_BLOCK - 1
    kcp = _key_features(kc.reshape(n, n_blk, NSA_KV_W), end)
    vct = vc.reshape(n, n_blk, G, HEAD_DIM).transpose(0, 2, 3, 1).astype(BF16)
    return _nsa_prompt(pn, kf, kcp, vct, n=n, t=t)


def _pad_axis(a, axis, size):
    pad = [(0, 0)] * a.ndim
    pad[axis] = (0, size - a.shape[axis])
    return jnp.pad(a, pad)


def _nsa_sample_group(pn, W, n, t_new, cache_kv, page_table, cache_win):
    G = NSA_KV_HEADS
    assert t_new <= CMP_BLOCK and t_new <= TQ
    past = page_table.shape[1] * PAGE_SIZE
    nb_past = past // CMP_BLOCK
    cache_t = cache_kv.transpose(0, 2, 3, 4, 1).reshape(cache_kv.shape[0], 4 * G * HEAD_DIM, PAGE_SIZE)
    win_t = cache_win.transpose(0, 2, 3, 4, 1)
    kc_past, vc_past = _compress_paged(page_table, cache_t, W["cmp_paged"])
    per_block = lambda a: a.reshape(n, G, nb_past, HEAD_DIM).transpose(0, 2, 1, 3).reshape(n, nb_past, NSA_KV_W)
    kc_past, vc_past = per_block(kc_past), per_block(vc_past)
    pn3 = pn.reshape(n, t_new, NSA_PAD)
    col = lambda i: pn3[:, :, NSA_Q_W + i * NSA_KV_W:NSA_Q_W + (i + 1) * NSA_KV_W]
    new_rows = _pad_axis(pn3[:, :, NSA_Q_W:NSA_Q_W + 2 * NSA_KV_W], 1, CMP_BLOCK).reshape(n * CMP_BLOCK, 2 * NSA_KV_W)
    kc_new, vc_new = _compress_rows(new_rows, 0, W["cmp"], nblk=n)
    nb_pad = -(-(nb_past + 1) // 16) * 16
    kc = _pad_axis(jnp.concatenate([kc_past, kc_new[:, None]], axis=1), 1, nb_pad)
    vc = _pad_axis(jnp.concatenate([vc_past, vc_new[:, None]], axis=1), 1, nb_pad)
    end_rel = (jnp.arange(nb_pad, dtype=jnp.int32) + 1) * CMP_BLOCK - 1 - past
    kcp = _key_features(kc, end_rel)
    vct = vc.reshape(n, nb_pad, G, HEAD_DIM).transpose(0, 2, 3, 1).astype(BF16)
    heads = lambda a: a.reshape(n, t_new, G, HEAD_DIM).transpose(0, 2, 1, 3)
    new_k = lambda i: _pad_axis(heads(col(i)), 2, KC).astype(BF16)
    new_vt = lambda i: _pad_axis(heads(col(i)).transpose(0, 1, 3, 2), 3, KC).astype(BF16)
    n_real = NSA_REP * t_new
    qt = _query_features(pn3[:, :, :NSA_Q_W]).reshape(n, t_new, G, NSA_REP, LANES)
    qt = _pad_axis(qt.transpose(0, 2, 4, 3, 1).reshape(n, G, LANES, n_real), 3, LANES)
    gates = pn3[:, :, NSA_Q_W + 6 * NSA_KV_W:NSA_COLS].reshape(n, t_new, G, NSA_REP, 3)
    gates_t = _pad_axis(gates.transpose(0, 2, 4, 3, 1).reshape(n, G, 3, n_real), 3, LANES)
    o_c, imp = _nsa_sample_cmp(qt, kcp, vct, past=past, n_new=t_new)
    imp_all = imp[:, :, :, :t_new].transpose(1, 2, 0, 3).reshape(G, nb_pad, n * t_new)
    selb = _nsa_rank(imp_all, past=past, n_new=t_new).reshape(G, nb_pad, n, t_new).transpose(2, 0, 1, 3)
    selb = jnp.pad(jnp.tile(selb, (1, 1, 1, NSA_REP)), ((0, 0),) * 3 + ((0, LANES - n_real),),
                   constant_values=-jnp.inf)
    o_t = _nsa_sample(page_table, qt, o_c, selb, win_t, new_k(2), new_vt(3), new_k(4), new_vt(5), gates_t, cache_t,
                      past=past, n_new=t_new)
    o = o_t[:, :, :, :n_real].reshape(n, G, HEAD_DIM, NSA_REP, t_new)
    return o.transpose(0, 4, 1, 3, 2).reshape(n * t_new, NSA_Q_W)


def _layer(x, mkv, shift0, s0, W, nsa_fn):
    n, t, _ = x.shape
    m = n * t
    G = NSA_KV_HEADS
    x2 = x.reshape(m, D_MODEL)
    tm = _tile_rows(m, 512)
    x2 = _ffn(x2, *W["ffn1"], W["norm_out_g"], final_norm=False, tm=tm, tf=D_FF // 2)
    pn, pr, kf = _proj(x2, *W["proj"], t=t, tm=_tile_rows(m, 256))
    o_nsa = nsa_fn(pn, kf)
    o_rwkv, s_new = _rwkv_group(pr, shift0, s0, W, n, t)
    x2 = _outproj(x2, o_nsa, o_rwkv, *W["w_out"], tm=tm)
    x3 = _xattn(x2.reshape(n, t, D_MODEL), mkv, *W["xattn"], tm=_tile_rows(t, 512))
    y = _ffn(x3.reshape(m, D_MODEL), *W["ffn2"], W["norm_out_g"], final_norm=True, tm=tm, tf=D_FF // 2)
    pn3 = pn.reshape(n, t, NSA_PAD)
    kv_rows = pn3[:, :, NSA_Q_W:NSA_Q_W + 4 * NSA_KV_W].reshape(n, t, 4, G, HEAD_DIM)
    win_new = pn3[:, :, NSA_Q_W + 4 * NSA_KV_W:NSA_Q_W + 6 * NSA_KV_W].reshape(n, t, 2, G, HEAD_DIM)
    shift_new = pr.reshape(n, t, RWKV_COLS)[:, -1]
    return y.reshape(n, t, D_MODEL), kv_rows, win_new, s_new, shift_new


def kernel(x_prompt, x_sample, cache_nsa_kv, cache_nsa_win, state_rwkv_s, state_rwkv_shift, cache_mem_kv, page_table, mem_prompt, norm_ffn1_g, w_ffn1_gu, w_ffn1_down, norm_mix_g, w_in, w_out, q_norm_g, kc_norm_g, ks_norm_g, kw_norm_g, gate_b, cmp_pe_k, cmp_pe_v, w_cmp_k1, w_cmp_k2, w_cmp_v1, w_cmp_v2, rwkv_mu, rwkv_w0, w_decay2, rwkv_a0, w_iclr2, w_gate2, rwkv_k_k, rwkv_k_a, rwkv_r_k, lnx_g, lnx_b, norm_x_g, norm_mem_g, w_xq, w_xkv, xq_norm_g, xk_norm_g, w_xo, norm_ffn2_g, w_ffn2_gu, w_ffn2_down, norm_out_g):
    layer_weights = (norm_ffn1_g, w_ffn1_gu, w_ffn1_down, norm_mix_g, w_in, w_out, q_norm_g, kc_norm_g, ks_norm_g,
                     kw_norm_g, gate_b, cmp_pe_k, cmp_pe_v, w_cmp_k1, w_cmp_k2, w_cmp_v1, w_cmp_v2, rwkv_mu, rwkv_w0,
                     w_decay2, rwkv_a0, w_iclr2, w_gate2, rwkv_k_k, rwkv_k_a, rwkv_r_k, lnx_g, lnx_b, norm_x_g,
                     norm_mem_g, w_xq, w_xkv, xq_norm_g, xk_norm_g, w_xo, norm_ffn2_g, w_ffn2_gu, w_ffn2_down,
                     norm_out_g)
    assert w_in.shape[0] == 1, "single-layer trunk"
    W = _prep_weights(*(w[0] for w in layer_weights))
    n_p, t_p, _ = x_prompt.shape
    n_s, t_s, _ = x_sample.shape
    n_mem = mem_prompt.shape[1]
    xw = X_HEADS * X_HEAD_DIM

    mkv_p = _memkv(mem_prompt.reshape(n_p * n_mem, D_MODEL), *W["memkv"], tm=_tile_rows(n_p * n_mem, 512))
    mkv_p = mkv_p.reshape(n_p, n_mem, 2 * xw)
    y_p, kv_p, win_p, rs_p, sh_p = _layer(
        x_prompt, mkv_p, jnp.zeros((n_p, RWKV_COLS), F32),
        jnp.zeros((n_p, RWKV_HEADS, RWKV_HEAD_DIM, RWKV_HEAD_DIM), F32), W,
        lambda pn, kf: _nsa_prompt_group(pn, kf, W, n_p, t_p))
    win_p = win_p[:, t_p - min(WINDOW, t_p):]

    mkv_s = cache_mem_kv[0].reshape(n_s, n_mem, 2 * xw)
    y_s, kv_s, win_new, rs_s, sh_s = _layer(
        x_sample, mkv_s, state_rwkv_shift[0], state_rwkv_s[0], W,
        lambda pn, kf: _nsa_sample_group(pn, W, n_s, t_s, cache_nsa_kv[0], page_table, cache_nsa_win[0]))
    win_s = jnp.concatenate([cache_nsa_win[0], win_new], axis=1)[:, t_s:]

    mkv_out = mkv_p.reshape(1, n_p, n_mem, 2, X_HEADS, X_HEAD_DIM)
    return (y_p, y_s, kv_p[None], kv_s[None], win_p[None], win_s[None], rs_p[None], rs_s[None], sh_p[None],
            sh_s[None], mkv_out)
```

```python
import functools
import math

import jax
import jax.numpy as jnp
from jax import lax
from jax.experimental import pallas as pl
from jax.experimental.pallas import tpu as pltpu

F32 = jnp.float32
BF16 = jnp.bfloat16

D_MODEL = 1024
NSA_HEADS = 8
NSA_KV_HEADS = 2
NSA_REP = NSA_HEADS // NSA_KV_HEADS
HEAD_DIM = 64
CMP_BLOCK = 64
N_SEL = 16
WINDOW = 512
CMP_HIDDEN = 128
PAGE_SIZE = 128
RWKV_HEADS = 8
RWKV_HEAD_DIM = 64
RWKV_W = RWKV_HEADS * RWKV_HEAD_DIM
DECAY_LORA = 64
ICLR_LORA = 64
GATE_LORA = 128
NSA_Q_W = NSA_HEADS * HEAD_DIM
NSA_KV_W = NSA_KV_HEADS * HEAD_DIM
NSA_COLS = NSA_Q_W + 6 * NSA_KV_W + 3 * NSA_HEADS
RWKV_COLS = 3 * RWKV_W + DECAY_LORA + ICLR_LORA + GATE_LORA
X_HEADS = 4
X_HEAD_DIM = 128
D_FF = 2816
RMS_EPS = 1e-6
LNX_EPS = 64e-5
FORCED_SCORE = 1e9

LANES = 128
SUBLANES = 8
NSA_PAD = 1408
P_COLS = NSA_PAD + RWKV_COLS
TQ = 128
KC = 128
VMEM_LIMIT = 56 * 1024 * 1024

ALIBI = tuple(2.0 ** (-8.0 * (h + 1.0) / NSA_HEADS) for h in range(NSA_HEADS))


def _cparams(sem):
    return pltpu.CompilerParams(dimension_semantics=sem, vmem_limit_bytes=VMEM_LIMIT)


def _rms(x, g):
    return x * lax.rsqrt(jnp.mean(x * x, axis=-1, keepdims=True) + RMS_EPS) * g


def _seg_ones(width, seg):
    r = lax.broadcasted_iota(jnp.int32, (width, width), 0) // seg
    c = lax.broadcasted_iota(jnp.int32, (width, width), 1) // seg
    return (r == c).astype(F32)


def _seg_sum(x, seg):
    ones = _seg_ones(LANES, seg)
    parts = [jnp.dot(x[:, c:c + LANES], ones, precision=lax.Precision.HIGHEST, preferred_element_type=F32)
             for c in range(0, x.shape[1], LANES)]
    return parts[0] if len(parts) == 1 else jnp.concatenate(parts, axis=1)


def _seg_rms(x, g, seg):
    if seg == LANES:
        parts = [_rms(x[:, c:c + LANES], g[:, c:c + LANES]) for c in range(0, x.shape[1], LANES)]
        return parts[0] if len(parts) == 1 else jnp.concatenate(parts, axis=1)
    return x * lax.rsqrt(_seg_sum(x * x, seg) * (1.0 / seg) + RMS_EPS) * g


def _ffn_kernel(x_ref, g_ref, wg_ref, wu_ref, wd_ref, go_ref, o_ref, xn_ref, acc_ref, *, final_norm):
    f = pl.program_id(1)

    @pl.when(f == 0)
    def _():
        xn_ref[...] = _rms(x_ref[...], g_ref[...]).astype(BF16)
        acc_ref[...] = jnp.zeros_like(acc_ref)

    xn = xn_ref[...]
    gate = jnp.dot(xn, wg_ref[...], preferred_element_type=F32)
    up = jnp.dot(xn, wu_ref[...], preferred_element_type=F32)
    h = (gate * jax.nn.sigmoid(gate) * up).astype(BF16)
    acc_ref[...] += jnp.dot(h, wd_ref[...], preferred_element_type=F32)

    @pl.when(f == pl.num_programs(1) - 1)
    def _():
        y = x_ref[...] + 0.5 * acc_ref[...]
        if final_norm:
            y = _rms(y, go_ref[...])
        o_ref[...] = y


def _ffn(x, g, w_gu, w_down, g_out, *, final_norm, tm, tf):
    m = x.shape[0]
    nf = D_FF // tf
    return pl.pallas_call(
        functools.partial(_ffn_kernel, final_norm=final_norm),
        grid=(m // tm, nf),
        in_specs=[
            pl.BlockSpec((tm, D_MODEL), lambda i, f: (i, 0)),
            pl.BlockSpec((1, D_MODEL), lambda i, f: (0, 0)),
            pl.BlockSpec((D_MODEL, tf), lambda i, f: (0, f)),
            pl.BlockSpec((D_MODEL, tf), lambda i, f: (0, nf + f)),
            pl.BlockSpec((tf, D_MODEL), lambda i, f: (f, 0)),
            pl.BlockSpec((1, D_MODEL), lambda i, f: (0, 0)),
        ],
        out_specs=pl.BlockSpec((tm, D_MODEL), lambda i, f: (i, 0)),
        out_shape=jax.ShapeDtypeStruct((m, D_MODEL), F32),
        scratch_shapes=[pltpu.VMEM((tm, D_MODEL), BF16), pltpu.VMEM((tm, D_MODEL), F32)],
        compiler_params=_cparams(("parallel", "arbitrary")),
        name="ffn",
    )(x, g, w_gu, w_gu, w_down, g_out)


KF_COLS = 2 * NSA_KV_HEADS * LANES


def _proj_kernel(x_ref, g_ref, w_ref, hg_ref, gb_ref, place_ref, on_ref, or_ref, kf_ref, *, t):
    xn = _rms(x_ref[...], g_ref[...]).astype(BF16)
    p = jnp.dot(xn, w_ref[...], preferred_element_type=F32)
    or_ref[...] = p[:, NSA_PAD:]
    on_ref[...] = p[:, 0:NSA_PAD]
    hg = hg_ref[...]
    on_ref[:, 0:NSA_Q_W] = _seg_rms(p[:, 0:NSA_Q_W], hg[:, 0:NSA_Q_W], HEAD_DIM)
    keys = []
    for c in (NSA_Q_W + 2 * NSA_KV_W, NSA_Q_W + 4 * NSA_KV_W):
        keys.append(_seg_rms(p[:, c:c + NSA_KV_W], hg[:, c:c + NSA_KV_W], HEAD_DIM))
        on_ref[:, c:c + NSA_KV_W] = keys[-1]
    c = NSA_Q_W + 6 * NSA_KV_W
    on_ref[:, c:c + LANES] = jax.nn.sigmoid(p[:, c:c + LANES] + gb_ref[...])
    tm = p.shape[0]
    kf = (jnp.dot(keys[0].astype(BF16), place_ref[0], preferred_element_type=F32)
          + jnp.dot(keys[1].astype(BF16), place_ref[1], preferred_element_type=F32))
    pos = (pl.program_id(0) * tm + lax.broadcasted_iota(jnp.int32, (tm, KF_COLS), 0)) % t
    lane = lax.broadcasted_iota(jnp.int32, (tm, KF_COLS), 1) % LANES
    feat = jnp.where(lane == HEAD_DIM, pos // CMP_BLOCK, jnp.where(lane == HEAD_DIM + 1, pos % CMP_BLOCK, 0))
    kf_ref[...] = (kf + feat.astype(F32)).astype(BF16)


def _proj(x, g, w_pad, head_g, gate_b, place, *, t, tm):
    m = x.shape[0]
    return pl.pallas_call(
        functools.partial(_proj_kernel, t=t),
        grid=(m // tm,),
        in_specs=[
            pl.BlockSpec((tm, D_MODEL), lambda i: (i, 0)),
            pl.BlockSpec((1, D_MODEL), lambda i: (0, 0)),
            pl.BlockSpec((D_MODEL, P_COLS), lambda i: (0, 0)),
            pl.BlockSpec((1, NSA_PAD), lambda i: (0, 0)),
            pl.BlockSpec((1, LANES), lambda i: (0, 0)),
            pl.BlockSpec((2, NSA_KV_W, KF_COLS), lambda i: (0, 0, 0)),
        ],
        out_specs=[pl.BlockSpec((tm, NSA_PAD), lambda i: (i, 0)), pl.BlockSpec((tm, RWKV_COLS), lambda i: (i, 0)),
                   pl.BlockSpec((tm, KF_COLS), lambda i: (i, 0))],
        out_shape=[jax.ShapeDtypeStruct((m, NSA_PAD), F32), jax.ShapeDtypeStruct((m, RWKV_COLS), F32),
                   jax.ShapeDtypeStruct((m, KF_COLS), BF16)],
        compiler_params=_cparams(("parallel",)),
        name="proj",
    )(x, g, w_pad, head_g, gate_b, place)


def _compress_core(xk_ref, xv_ref, nblk, pe_ref, w1k_ref, w1v_ref, w2k_ref, w2v_ref, kcg_ref):
    def body(j, carry):
        acc_k, acc_v = carry
        pe = pe_ref[j]
        xk = xk_ref[pl.ds(j, nblk, stride=CMP_BLOCK), :] + pe[:, 0:LANES]
        xv = xv_ref[pl.ds(j, nblk, stride=CMP_BLOCK), :] + pe[:, LANES:2 * LANES]
        acc_k = acc_k + jnp.dot(xk.astype(BF16), w1k_ref[j], preferred_element_type=F32)
        acc_v = acc_v + jnp.dot(xv.astype(BF16), w1v_ref[j], preferred_element_type=F32)
        return acc_k, acc_v

    zero = jnp.zeros((nblk, 2 * CMP_HIDDEN), F32)
    acc_k, acc_v = lax.fori_loop(0, CMP_BLOCK, body, (zero, zero))
    hk = jax.nn.gelu(acc_k).astype(BF16)
    hv = jax.nn.gelu(acc_v).astype(BF16)
    kc = jnp.dot(hk, w2k_ref[...], preferred_element_type=F32)
    vc = jnp.dot(hv, w2v_ref[...], preferred_element_type=F32)
    return _seg_rms(kc, kcg_ref[...], HEAD_DIM), vc


def _compress_kernel(xk_ref, xv_ref, pe_ref, w1k_ref, w1v_ref, w2k_ref, w2v_ref, kcg_ref, kc_ref, vc_ref, *, nblk):
    kc, vc = _compress_core(xk_ref, xv_ref, nblk, pe_ref, w1k_ref, w1v_ref, w2k_ref, w2v_ref, kcg_ref)
    kc_ref[...] = kc
    vc_ref[...] = vc


def _cmp_weight_specs(imap):
    return [
        pl.BlockSpec((CMP_BLOCK, 1, 2 * LANES), imap(3)),
        pl.BlockSpec((CMP_BLOCK, LANES, 2 * CMP_HIDDEN), imap(3)),
        pl.BlockSpec((CMP_BLOCK, LANES, 2 * CMP_HIDDEN), imap(3)),
        pl.BlockSpec((2 * CMP_HIDDEN, LANES), imap(2)),
        pl.BlockSpec((2 * CMP_HIDDEN, LANES), imap(2)),
        pl.BlockSpec((1, LANES), imap(2)),
    ]


def _compress_rows(rows, col_block, cw, *, nblk):
    m = rows.shape[0]
    steps = m // (nblk * CMP_BLOCK)
    imap = lambda nd: (lambda i: (0,) * nd)
    return pl.pallas_call(
        functools.partial(_compress_kernel, nblk=nblk),
        grid=(steps,),
        in_specs=[pl.BlockSpec((nblk * CMP_BLOCK, LANES), lambda i: (i, col_block)),
                  pl.BlockSpec((nblk * CMP_BLOCK, LANES), lambda i: (i, col_block + 1))] + _cmp_weight_specs(imap),
        out_specs=[pl.BlockSpec((nblk, LANES), lambda i: (i, 0))] * 2,
        out_shape=[jax.ShapeDtypeStruct((steps * nblk, LANES), F32)] * 2,
        compiler_params=_cparams(("parallel",)),
        name="compress",
    )(rows, rows, *cw)


CMP_ROWS = 2 * NSA_KV_HEADS * HEAD_DIM
CMP_FEATS = 8


def _compress_paged_kernel(pt_ref, cache_ref, pe_ref, w1_ref, w2_ref, kcg_ref, kc_ref, vc_ref, buf_ref, sem, *,
                           n_pages):
    n = pl.program_id(0)
    slot = n % 2

    def page_copy(entry, buf_slot, i):
        return pltpu.make_async_copy(cache_ref.at[pt_ref[entry, i], pl.ds(0, CMP_ROWS), :],
                                     buf_ref.at[buf_slot, :, i, :], sem.at[buf_slot])

    def start_entry(entry, buf_slot):
        def start(i, c):
            page_copy(entry, buf_slot, i).start()
            return c
        lax.fori_loop(0, n_pages, start, 0)

    @pl.when(n == 0)
    def _():
        start_entry(0, 0)

    @pl.when(n + 1 < pl.num_programs(0))
    def _():
        start_entry(n + 1, 1 - slot)

    def wait(i, c):
        page_copy(n, slot, i).wait()
        return c

    lax.fori_loop(0, n_pages, wait, 0)
    blocks_per_page = PAGE_SIZE // CMP_BLOCK
    for c, out_ref in enumerate((kc_ref, vc_ref)):
        for g in range(NSA_KV_HEADS):
            def body(it, acc):
                d0 = pl.multiple_of(it * CMP_FEATS, CMP_FEATS)
                x = jnp.concatenate(
                    [buf_ref[slot, (c * NSA_KV_HEADS + g) * HEAD_DIM + d0 + u] + pe_ref[c * HEAD_DIM + d0 + u]
                     for u in range(CMP_FEATS)], axis=1)
                w = w1_ref[c, pl.ds(d0, CMP_FEATS)].reshape(CMP_FEATS * PAGE_SIZE, blocks_per_page * CMP_HIDDEN)
                return acc + jnp.dot(x.astype(BF16), w, preferred_element_type=F32)

            acc = lax.fori_loop(0, HEAD_DIM // CMP_FEATS, body,
                                jnp.zeros((n_pages, blocks_per_page * CMP_HIDDEN), F32))
            out = jnp.dot(jax.nn.gelu(acc).astype(BF16), w2_ref[c], preferred_element_type=F32)
            if c == 0:
                out = _seg_rms(out, kcg_ref[...], HEAD_DIM)
            out_ref[0, g] = out


def _compress_paged(page_table, cache_t, cw):
    nb, n_pages = page_table.shape
    width = (PAGE_SIZE // CMP_BLOCK) * HEAD_DIM
    const = lambda shape: pl.BlockSpec(shape, lambda n, pt: (0,) * len(shape))
    out = pl.BlockSpec((1, NSA_KV_HEADS, n_pages, width), lambda n, pt: (n, 0, 0, 0))
    return pl.pallas_call(
        functools.partial(_compress_paged_kernel, n_pages=n_pages),
        grid_spec=pltpu.PrefetchScalarGridSpec(
            num_scalar_prefetch=1,
            grid=(nb,),
            in_specs=[pl.BlockSpec(memory_space=pl.ANY)] + [const(w.shape) for w in cw],
            out_specs=[out, out],
            scratch_shapes=[pltpu.VMEM((2, CMP_ROWS, n_pages, PAGE_SIZE), F32), pltpu.SemaphoreType.DMA((2,))],
        ),
        out_shape=[jax.ShapeDtypeStruct((nb, NSA_KV_HEADS, n_pages, width), F32)] * 2,
        compiler_params=_cparams(("arbitrary",)),
        name="compress_paged",
    )(page_table, cache_t, *cw)


QW = NSA_REP * TQ
SEL_KC = 512
RANK_UNROLL = 8


def _wide_init(width=QW):
    return (jnp.full((1, width), -jnp.inf, F32), jnp.zeros((1, width), F32), jnp.zeros((HEAD_DIM, width), F32))


def _wide_step(state, s, vt=None, v_rows=None, group=0):
    m, l, acc = state
    m_new = jnp.maximum(m, jnp.max(s, axis=0, keepdims=True))
    m_safe = jnp.where(m_new == -jnp.inf, 0.0, m_new)
    alpha = jnp.exp(m - m_safe)
    p = jnp.exp(s - m_safe)
    l = alpha * l + jnp.sum(p, axis=0, keepdims=True)
    if vt is not None:
        pv = jnp.dot(vt, p.astype(BF16), preferred_element_type=F32)
    else:
        pv = lax.dot_general(v_rows, p.astype(BF16), (((0,), (0,)), ((), ())), preferred_element_type=F32)
        pv = pv[group * HEAD_DIM:(group + 1) * HEAD_DIM, :]
    return m_new, l, alpha * acc + pv


def _wide_out(state):
    _, l, acc = state
    return acc / jnp.maximum(l, 1e-30)


def _cmp_branch(qt, kcp, vct, t_row, n_blk):
    b_col = lax.broadcasted_iota(jnp.int32, (kcp.shape[0], qt.shape[1]), 0)
    valid = (t_row >= (b_col + 1) * CMP_BLOCK - 1) & (b_col < n_blk)
    s = jnp.where(valid, jnp.dot(kcp, qt, preferred_element_type=F32), -jnp.inf)
    m = jnp.max(s, axis=0, keepdims=True)
    e = jnp.exp(s - jnp.where(m == -jnp.inf, 0.0, m))
    p = e / jnp.maximum(jnp.sum(e, axis=0, keepdims=True), 1e-30)
    return jnp.dot(vct, p.astype(BF16), preferred_element_type=F32), p


def _select_blocks(imp, t_row, n_blk, score_ref, unrolled=False):
    nb_pad, w = imp.shape
    b_col = lax.broadcasted_iota(jnp.int32, (nb_pad, w), 0)
    cur = t_row // CMP_BLOCK
    forced = (b_col == 0) | (b_col == cur) | (b_col == cur - 1)
    score = jnp.where(forced, FORCED_SCORE, jnp.where(b_col <= cur, imp, -FORCED_SCORE))
    score = jnp.where(b_col < n_blk, score, -jnp.inf)
    score_ref[...] = score

    if unrolled:
        n_tiles = nb_pad // SUBLANES
        tile = lambda a, k: a[k * SUBLANES:(k + 1) * SUBLANES, :]
        cnt = [jnp.zeros((SUBLANES, w), F32) for _ in range(n_tiles)]
        for bp in range(n_blk):
            row = jnp.broadcast_to(score_ref[pl.ds(bp, 1), :], (SUBLANES, w))
            for k in range(n_tiles):
                sc = tile(score, k)
                if (k + 1) * SUBLANES - 1 < bp:
                    ahead = row > sc
                elif k * SUBLANES > bp:
                    ahead = row >= sc
                else:
                    ahead = (row > sc) | ((row == sc) & (tile(b_col, k) > bp))
                cnt[k] = cnt[k] + jnp.where(ahead, 1.0, 0.0)
        cnt = jnp.concatenate(cnt, axis=0)
        return jnp.where((cnt < N_SEL) & (b_col < n_blk), 0.0, -jnp.inf)

    def rank_body(it, cnt):
        base = pl.multiple_of(it * RANK_UNROLL, RANK_UNROLL)
        for u in range(RANK_UNROLL):
            bp = base + u
            row = jnp.broadcast_to(score_ref[pl.ds(bp, 1), :], (nb_pad, w))
            ahead = (row > score) | ((row == score) & (b_col > bp))
            cnt = cnt + jnp.where(ahead, 1.0, 0.0)
        return cnt

    cnt = lax.fori_loop(0, nb_pad // RANK_UNROLL, rank_body, jnp.zeros((nb_pad, w), F32))
    return jnp.where((cnt < N_SEL) & (b_col < n_blk), 0.0, -jnp.inf)


def _sum_lane_chunks(p):
    imp = p[:, 0:TQ]
    for r in range(1, NSA_REP):
        imp = imp + p[:, r * TQ:(r + 1) * TQ]
    return imp


def _nsa_prompt_kernel(q_ref, gate_ref, kcp_ref, vct_ref, ks0_ref, ks1_ref, vs_ref, kw0_ref, kw1_ref, vw_ref, o_ref,
                       score_ref, selb_ref, need_ref, *, n_blk):
    ks_refs, kw_refs = (ks0_ref, ks1_ref), (kw0_ref, kw1_ref)
    i = pl.program_id(1)
    t0 = i * TQ
    t_row = t0 + (lax.broadcasted_iota(jnp.int32, (1, QW), 1) & (TQ - 1))
    blocks_per_step = SEL_KC // CMP_BLOCK
    wk = WINDOW + TQ
    ws = pl.multiple_of(jnp.maximum(t0 - WINDOW, 0), TQ)
    dist = t_row - (ws + lax.broadcasted_iota(jnp.int32, (wk, QW), 0))
    wbias = jnp.where((dist >= 0) & (dist < WINDOW), 0.0, -jnp.inf)
    n_steps = (t0 + TQ + SEL_KC - 1) // SEL_KC

    q_t = (q_ref[...] * (HEAD_DIM ** -0.5)).T
    gate_t = gate_ref[...].T
    f_row = lax.broadcasted_iota(jnp.int32, (HEAD_DIM, TQ), 0)
    groups = range(NSA_KV_HEADS)
    qts, o_c = [], []
    for g in groups:
        cols = []
        for r in range(NSA_REP):
            h = g * NSA_REP + r
            feat = jnp.where(f_row == 0, ALIBI[h] * CMP_BLOCK, jnp.where(f_row == 1, ALIBI[h], 0.0))
            cols.append(jnp.concatenate([q_t[h * HEAD_DIM:(h + 1) * HEAD_DIM, :], feat], axis=0))
        qts.append(jnp.concatenate(cols, axis=1).astype(BF16))
        o, p = _cmp_branch(qts[g], kcp_ref[0, g], vct_ref[0, g], t_row, n_blk)
        o_c.append(o)
        selb = _select_blocks(_sum_lane_chunks(p), t_row[:, 0:TQ], n_blk, score_ref, unrolled=True)
        selb_ref[g] = jnp.concatenate([selb] * NSA_REP, axis=1)

    def step_scores(c, g):
        k0 = pl.multiple_of(c * SEL_KC, SEL_KC)
        s = jnp.dot(ks_refs[g][pl.ds(k0, SEL_KC), :], qts[g], preferred_element_type=F32)
        return jnp.concatenate(
            [s[b * CMP_BLOCK:(b + 1) * CMP_BLOCK, :] + selb_ref[g, pl.ds(c * blocks_per_step + b, 1), :]
             for b in range(blocks_per_step)], axis=0)

    def step_values(c):
        return vs_ref[pl.ds(pl.multiple_of(c * SEL_KC, SEL_KC), SEL_KC), :].astype(BF16)

    def sel_step(c, states):
        v = step_values(c)
        return tuple(_wide_step(states[g], step_scores(c, g), v_rows=v, group=g) for g in groups)

    for c in range(n_blk // blocks_per_step):
        rows = functools.reduce(jnp.maximum, [selb_ref[g, c * blocks_per_step:(c + 1) * blocks_per_step, 0:TQ]
                                              for g in groups])
        need_ref[c] = (jnp.max(rows) > -1.0).astype(jnp.int32)

    def sel_body(c, states):
        return lax.cond(need_ref[c] > 0, functools.partial(sel_step, c), lambda s: s, states)

    states = lax.fori_loop(0, n_steps - 1, sel_body, tuple(_wide_init() for _ in groups))
    c_last = n_steps - 1
    causal = c_last * SEL_KC + lax.broadcasted_iota(jnp.int32, (SEL_KC, QW), 0) <= t_row
    v = step_values(c_last)
    o_s = [_wide_out(_wide_step(states[g], jnp.where(causal, step_scores(c_last, g), -jnp.inf), v_rows=v, group=g))
           for g in groups]

    v = vw_ref[pl.ds(ws, wk), :].astype(BF16)
    o_w = [_wide_out(_wide_step(
        _wide_init(), jnp.dot(kw_refs[g][pl.ds(ws, wk), :], qts[g], preferred_element_type=F32) + wbias,
        v_rows=v, group=g)) for g in groups]

    outs = []
    for g in groups:
        for r in range(NSA_REP):
            lanes = slice(r * TQ, (r + 1) * TQ)
            row = (g * NSA_REP + r) * 3
            outs.append(gate_t[row:row + 1, :] * o_c[g][:, lanes] + gate_t[row + 1:row + 2, :] * o_s[g][:, lanes]
                        + gate_t[row + 2:row + 3, :] * o_w[g][:, lanes])
    o_ref[...] = jnp.concatenate(outs, axis=0).T


def _nsa_prompt(pn, kf, kcp, vct, *, n, t):
    G = NSA_KV_HEADS
    assert G == 2
    n_tiles = t // TQ
    n_blk = t // CMP_BLOCK
    col_block = lambda c: c // LANES
    full = lambda shape: pl.BlockSpec((1,) + shape, lambda b, i: (b,) + (0,) * len(shape))
    seq_cols = lambda c: pl.BlockSpec((t, LANES), lambda b, i: (b, col_block(c)))
    seq_slot = lambda s: pl.BlockSpec((t, LANES), lambda b, i: (b, s))
    return pl.pallas_call(
        functools.partial(_nsa_prompt_kernel, n_blk=n_blk),
        grid=(n, n_tiles),
        in_specs=[
            pl.BlockSpec((TQ, NSA_Q_W), lambda b, i: (b * n_tiles + i, 0)),
            pl.BlockSpec((TQ, LANES), lambda b, i: (b * n_tiles + i, col_block(NSA_Q_W + 6 * NSA_KV_W))),
            full((G, n_blk, LANES)),
            full((G, HEAD_DIM, n_blk)),
            seq_slot(0), seq_slot(1),
            seq_cols(NSA_Q_W + 3 * NSA_KV_W),
            seq_slot(2), seq_slot(3),
            seq_cols(NSA_Q_W + 5 * NSA_KV_W),
        ],
        out_specs=pl.BlockSpec((TQ, NSA_Q_W), lambda b, i: (b * n_tiles + i, 0)),
        out_shape=jax.ShapeDtypeStruct((n * t, NSA_Q_W), F32),
        scratch_shapes=[pltpu.VMEM((n_blk, TQ), F32), pltpu.VMEM((G, n_blk, QW), F32),
                        pltpu.SMEM((n_blk // (SEL_KC // CMP_BLOCK),), jnp.int32)],
        compiler_params=_cparams(("parallel", "arbitrary")),
        name="nsa_prompt",
    )(pn, pn, kcp, vct, kf, kf, pn, kf, kf, pn)


PAGES_PER_STEP = 4
PAGE_GROUP = 8


SEL_ROWS = 2 * NSA_KV_HEADS * HEAD_DIM


def _nsa_sample_cmp_kernel(qt_ref, kcp_ref, vct_ref, o_ref, imp_ref, *, past, n_new, n_blk):
    n_real = NSA_REP * n_new
    t_row = past + lax.broadcasted_iota(jnp.int32, (1, LANES), 1) % n_new
    li = lax.broadcasted_iota(jnp.int32, (LANES, LANES), 0)
    lj = lax.broadcasted_iota(jnp.int32, (LANES, LANES), 1)
    same_token = ((li % n_new == lj) & (li < n_real)).astype(F32)
    for g in range(NSA_KV_HEADS):
        o, p = _cmp_branch(qt_ref[0, g], kcp_ref[0, g], vct_ref[0, g], t_row, n_blk)
        o_ref[0, g] = o
        imp_ref[0, g] = jnp.dot(p, same_token, precision=lax.Precision.HIGHEST, preferred_element_type=F32)


def _nsa_sample_cmp(qt, kcp, vct, *, past, n_new):
    n, G, nb_pad = kcp.shape[:3]
    full = lambda shape: pl.BlockSpec((1,) + shape, lambda b: (b,) + (0,) * len(shape))
    return pl.pallas_call(
        functools.partial(_nsa_sample_cmp_kernel, past=past, n_new=n_new, n_blk=past // CMP_BLOCK + 1),
        grid=(n,),
        in_specs=[full((G, LANES, LANES)), full((G, nb_pad, LANES)), full((G, HEAD_DIM, nb_pad))],
        out_specs=[full((G, HEAD_DIM, LANES)), full((G, nb_pad, LANES))],
        out_shape=[jax.ShapeDtypeStruct((n, G, HEAD_DIM, LANES), F32),
                   jax.ShapeDtypeStruct((n, G, nb_pad, LANES), F32)],
        compiler_params=_cparams(("parallel",)),
        name="nsa_sample_cmp",
    )(qt, kcp, vct)


def _nsa_rank_kernel(imp_ref, selb_ref, score_ref, *, past, n_new, n_blk):
    t_row = past + lax.broadcasted_iota(jnp.int32, (1, imp_ref.shape[2]), 1) % n_new
    selb_ref[0] = _select_blocks(imp_ref[0], t_row, n_blk, score_ref)


def _nsa_rank(imp, *, past, n_new):
    G, nb_pad, w = imp.shape
    blk = pl.BlockSpec((1, nb_pad, w), lambda g: (g, 0, 0))
    return pl.pallas_call(
        functools.partial(_nsa_rank_kernel, past=past, n_new=n_new, n_blk=past // CMP_BLOCK + 1),
        grid=(G,),
        in_specs=[blk],
        out_specs=blk,
        out_shape=jax.ShapeDtypeStruct((G, nb_pad, w), F32),
        scratch_shapes=[pltpu.VMEM((nb_pad, w), F32)],
        compiler_params=_cparams(("parallel",)),
        name="nsa_rank",
    )(imp)


def _nsa_sample_kernel(pt_ref, qt_ref, oc_ref, selb_ref, win_ref, knew_ref, vnewt_ref, kwnew_ref, vwnewt_ref, gt_ref,
                       cache_ref, o_ref, buf_ref, slot_ref, sem, *, past, n_new, n_pages):
    n = pl.program_id(0)
    G = NSA_KV_HEADS
    n_real = NSA_REP * n_new
    blocks_per_page = PAGE_SIZE // CMP_BLOCK
    nb_past = past // CMP_BLOCK
    w_buf = min(WINDOW, past)
    lane = lax.broadcasted_iota(jnp.int32, (1, LANES), 1)
    t_row = past + lane % n_new
    real = lane < n_real
    selb_ref = selb_ref.at[0]

    def page_copy(lp, slot):
        return pltpu.make_async_copy(cache_ref.at[pt_ref[n, lp], pl.ds(CMP_ROWS, SEL_ROWS), :],
                                     buf_ref.at[pl.ds(slot * SEL_ROWS, SEL_ROWS), :], sem)

    def wanted_rows(first, count):
        rows = jnp.maximum(selb_ref[0, pl.ds(first, count), :], selb_ref[1, pl.ds(first, count), :])
        return jnp.max(jnp.where(real, rows, -jnp.inf)) > -1.0

    def page_body(lp, cnt):
        wanted = wanted_rows(lp * blocks_per_page, 1)
        for b in range(1, blocks_per_page):
            wanted = wanted | wanted_rows(lp * blocks_per_page + b, 1)

        @pl.when(wanted)
        def _():
            page_copy(lp, cnt).start()
            slot_ref[cnt] = lp

        return cnt + wanted.astype(jnp.int32)

    def group_body(pg, cnt):
        first = pl.multiple_of(pg * (PAGE_GROUP * blocks_per_page), PAGE_GROUP * blocks_per_page)
        return lax.cond(wanted_rows(first, PAGE_GROUP * blocks_per_page),
                        lambda c: lax.fori_loop(pg * PAGE_GROUP, (pg + 1) * PAGE_GROUP, page_body, c),
                        lambda c: c, cnt)

    n_slots = lax.fori_loop(0, n_pages // PAGE_GROUP, group_body, jnp.int32(0))
    n_steps = (n_slots + PAGES_PER_STEP - 1) // PAGES_PER_STEP

    def fill_body(j, c):
        slot = n_slots + j

        @pl.when(slot < n_steps * PAGES_PER_STEP)
        def _():
            page_copy(0, slot).start()
            slot_ref[slot] = -1

        return c

    lax.fori_loop(0, PAGES_PER_STEP - 1, fill_body, 0)

    def wait_body(s, c):
        page_copy(jnp.maximum(slot_ref[s], 0), s).wait()
        return c

    lax.fori_loop(0, n_steps * PAGES_PER_STEP, wait_body, 0)

    k_idx = lax.broadcasted_iota(jnp.int32, (PAGE_SIZE, LANES), 0)
    new_idx = lax.broadcasted_iota(jnp.int32, (knew_ref.shape[2], LANES), 0)
    pos_new = past + new_idx
    pos_win = past - w_buf + lax.broadcasted_iota(jnp.int32, (w_buf, LANES), 0)
    tn_dims = (((0,), (0,)), ((), ()))

    for g in range(G):
        q64 = qt_ref[0, g][0:HEAD_DIM, :]
        slope = jnp.zeros((1, LANES), F32)
        for r in range(NSA_REP):
            slope = jnp.where((lane >= r * n_new) & (lane < (r + 1) * n_new), ALIBI[g * NSA_REP + r], slope)

        def alibi(s, pos):
            return s - slope * (t_row - pos).astype(F32)

        def sel_body(st, state):
            kts, vts, pos, bias = [], [], [], []
            for j in range(PAGES_PER_STEP):
                slot = st * PAGES_PER_STEP + j
                base = pl.multiple_of(slot * SEL_ROWS, SEL_ROWS)
                kts.append(buf_ref[pl.ds(base + g * HEAD_DIM, HEAD_DIM), :])
                vts.append(buf_ref[pl.ds(base + (NSA_KV_HEADS + g) * HEAD_DIM, HEAD_DIM), :])
                lp = slot_ref[slot]
                live = jnp.where(lp >= 0, 0.0, -jnp.inf)
                lp = jnp.maximum(lp, 0)
                pos.append(lp * PAGE_SIZE + k_idx)
                bias += [jnp.broadcast_to(selb_ref[g, pl.ds(lp * blocks_per_page + b, 1), :] + live,
                                          (CMP_BLOCK, LANES)) for b in range(blocks_per_page)]
            kt = jnp.concatenate(kts, axis=1).astype(BF16)
            s = lax.dot_general(kt, q64, tn_dims, preferred_element_type=F32)
            s = alibi(s, jnp.concatenate(pos, axis=0)) + jnp.concatenate(bias, axis=0)
            return _wide_step(state, s, jnp.concatenate(vts, axis=1).astype(BF16))

        state = lax.fori_loop(0, n_steps, sel_body, _wide_init(LANES))
        s = alibi(jnp.dot(knew_ref[0, g], q64, preferred_element_type=F32), pos_new)
        s = jnp.where(pos_new <= t_row, s + selb_ref[g, pl.ds(nb_past, 1), :], -jnp.inf)
        o_s = _wide_out(_wide_step(state, s, vnewt_ref[0, g]))

        s = lax.dot_general(win_ref[0, 0, g].astype(BF16), q64, tn_dims, preferred_element_type=F32)
        s = jnp.where(t_row - pos_win < WINDOW, alibi(s, pos_win), -jnp.inf)
        state = _wide_step(_wide_init(LANES), s, win_ref[0, 1, g].astype(BF16))
        s = alibi(jnp.dot(kwnew_ref[0, g], q64, preferred_element_type=F32), pos_new)
        s = jnp.where((pos_new <= t_row) & (new_idx < n_new), s, -jnp.inf)
        o_w = _wide_out(_wide_step(state, s, vwnewt_ref[0, g]))

        o_ref[0, g] = (gt_ref[0, g, pl.ds(0, 1), :] * oc_ref[0, g] + gt_ref[0, g, pl.ds(1, 1), :] * o_s
                       + gt_ref[0, g, pl.ds(2, 1), :] * o_w)


def _nsa_sample(page_table, qt, o_c, selb, win_t, knew, vnewt, kwnew, vwnewt, gates_t, cache_t, *, past, n_new):
    n = qt.shape[0]
    G = NSA_KV_HEADS
    n_pages = page_table.shape[1]
    assert n_pages % PAGE_GROUP == 0
    nb_pad = selb.shape[2]
    w_buf = win_t.shape[4]
    assert w_buf == min(WINDOW, past)
    n_newp = knew.shape[2]
    full = lambda shape: pl.BlockSpec((1,) + shape, lambda b, pt: (b,) + (0,) * len(shape))
    kern = functools.partial(_nsa_sample_kernel, past=past, n_new=n_new, n_pages=n_pages)
    return pl.pallas_call(
        kern,
        grid_spec=pltpu.PrefetchScalarGridSpec(
            num_scalar_prefetch=1,
            grid=(n,),
            in_specs=[
                full((G, LANES, LANES)),
                full((G, HEAD_DIM, LANES)),
                full((G, nb_pad, LANES)),
                full((2, G, HEAD_DIM, w_buf)),
                full((G, n_newp, HEAD_DIM)),
                full((G, HEAD_DIM, n_newp)),
                full((G, n_newp, HEAD_DIM)),
                full((G, HEAD_DIM, n_newp)),
                full((G, 3, LANES)),
                pl.BlockSpec(memory_space=pl.ANY),
            ],
            out_specs=full((G, HEAD_DIM, LANES)),
            scratch_shapes=[
                pltpu.VMEM((n_pages * SEL_ROWS, PAGE_SIZE), F32),
                pltpu.SMEM((n_pages + PAGES_PER_STEP,), jnp.int32),
                pltpu.SemaphoreType.DMA(()),
            ],
        ),
        out_shape=jax.ShapeDtypeStruct((n, G, HEAD_DIM, LANES), F32),
        compiler_params=_cparams(("arbitrary",)),
        name="nsa_sample",
    )(page_table, qt, o_c, selb, win_t, knew, vnewt, kwnew, vwnewt, gates_t, cache_t)


def _alibi_features(pos):
    blk = (pos // CMP_BLOCK).astype(F32)
    off = (pos % CMP_BLOCK).astype(F32)
    pad = jnp.zeros(pos.shape + (HEAD_DIM - 2,), F32)
    return jnp.concatenate([blk[..., None], off[..., None], pad], axis=-1)


def _query_features(q):
    n, t, _ = q.shape
    qh = q.reshape(n, t, NSA_HEADS, HEAD_DIM) * (HEAD_DIM ** -0.5)
    slope = jnp.asarray(ALIBI, F32)
    feat = jnp.zeros((NSA_HEADS, HEAD_DIM), F32).at[:, 0].set(slope * CMP_BLOCK).at[:, 1].set(slope)
    feat = jnp.broadcast_to(feat, (n, t, NSA_HEADS, HEAD_DIM))
    return jnp.concatenate([qh, feat], axis=-1).reshape(n, t, NSA_HEADS * LANES).astype(BF16)


def _key_features(k, pos):
    n, l, _ = k.shape
    kh = k.reshape(n, l, NSA_KV_HEADS, HEAD_DIM).transpose(0, 2, 1, 3)
    feat = jnp.broadcast_to(_alibi_features(pos), (n, NSA_KV_HEADS, l, HEAD_DIM))
    return jnp.concatenate([kh, feat], axis=-1).astype(BF16)


def _rwkv_pre_kernel(p_ref, before_ref, shift_ref, mu_ref, w0_ref, wd_ref, a0_ref, wa_ref, wg_ref, kk_ref, ka_ref,
                     rk_ref, r_o, d_o, k_o, v_o, kk_o, b_o, g_o, bonus_o):
    p = p_ref[...]
    first = jnp.where(pl.program_id(1) == 0, shift_ref[0], before_ref[7:8, :])
    row = lax.broadcasted_iota(jnp.int32, p.shape, 0)
    prev = jnp.where(row == 0, first, pltpu.roll(p, 1, axis=0))
    xs = p + (prev - p) * mu_ref[...]
    W = RWKV_W
    r, k, v = xs[:, 0:W], xs[:, W:2 * W], xs[:, 2 * W:3 * W]
    lora = xs[:, 3 * W:3 * W + LANES]
    xg = xs[:, 3 * W + LANES:3 * W + 2 * LANES]
    z = w0_ref[...] + jnp.dot(jnp.tanh(lora).astype(BF16), wd_ref[...], preferred_element_type=F32)
    nz = -z
    softplus = jnp.maximum(nz, 0.0) + jnp.log(1.0 + jnp.exp(-jnp.abs(nz)))
    decay = jnp.exp(-jnp.exp(-softplus - 0.5))
    a = jax.nn.sigmoid(a0_ref[...] + jnp.dot(lora.astype(BF16), wa_ref[...], preferred_element_type=F32))
    g = jnp.dot(jax.nn.sigmoid(xg).astype(BF16), wg_ref[...], preferred_element_type=F32)
    kk = k * kk_ref[...]
    kk = kk * lax.rsqrt(jnp.maximum(_seg_sum(kk * kk, RWKV_HEAD_DIM), 1e-24))
    k_h = k * (1.0 + (a - 1.0) * ka_ref[...])
    r_o[...] = r
    d_o[...] = decay
    k_o[...] = k_h
    v_o[...] = v
    kk_o[...] = kk
    b_o[...] = kk * a
    g_o[...] = g
    bonus_o[...] = _seg_sum(r * k_h * rk_ref[...], RWKV_HEAD_DIM) * v


def _rwkv_pre(p, shift0, rw, *, n, t, tm):
    m = n * t
    assert tm % 8 == 0 and t % tm == 0
    tiles = t // tm
    row = lambda w: pl.BlockSpec((1, w), lambda b, i: (0, 0))
    mat = lambda a, c: pl.BlockSpec((a, c), lambda b, i: (0, 0))
    out = pl.BlockSpec((tm, RWKV_W), lambda b, i: (b * tiles + i, 0))
    return pl.pallas_call(
        _rwkv_pre_kernel,
        grid=(n, tiles),
        in_specs=[
            pl.BlockSpec((tm, RWKV_COLS), lambda b, i: (b * tiles + i, 0)),
            pl.BlockSpec((8, RWKV_COLS), lambda b, i: (jnp.maximum((b * tiles + i) * (tm // 8) - 1, 0), 0)),
            pl.BlockSpec((1, 1, RWKV_COLS), lambda b, i: (b, 0, 0)),
            row(RWKV_COLS), row(RWKV_W), mat(LANES, RWKV_W), row(RWKV_W), mat(LANES, RWKV_W),
            mat(GATE_LORA, RWKV_W), row(RWKV_W), row(RWKV_W), row(RWKV_W),
        ],
        out_specs=[out] * 8,
        out_shape=[jax.ShapeDtypeStruct((m, RWKV_W), F32)] * 8,
        compiler_params=_cparams(("parallel", "arbitrary")),
        name="rwkv_pre",
    )(p, p, shift0.reshape(n, 1, RWKV_COLS), *rw)


RW_J = RWKV_HEAD_DIM // 2
RW_PAIRS = LANES // 2


def _rwkv_scan_kernel(r_ref, d_ref, k_ref, kk_ref, b_ref, v_ref, s0_ref, o_ref, sout_ref, s_ref, *, tc):
    c = pl.program_id(1)

    @pl.when(c == 0)
    def _():
        s_ref[...] = s0_ref[0]

    def both_halves(x):
        return x + pltpu.roll(x, RW_PAIRS, axis=1)

    tiles = [pl.ds(k * SUBLANES, SUBLANES) for k in range(RWKV_HEAD_DIM // SUBLANES)]

    def key_row(ref, t, j):
        return jnp.broadcast_to(ref[0, t, pl.ds(j, 1), :], (SUBLANES, LANES))

    def step(t, sa_parts):
        t_next = jnp.minimum(t + 1, tc - 1)
        sa_next = []
        for i, rows in enumerate(tiles):
            u = -both_halves(sa_parts[i])
            vt = v_ref[0, t, rows, :]
            o = jnp.zeros((SUBLANES, LANES), F32)
            sa = jnp.zeros((SUBLANES, LANES), F32)
            for j in range(RW_J):
                h = (s_ref[j, rows, :] * key_row(d_ref, t, j) + u * key_row(b_ref, t, j)
                     + vt * key_row(k_ref, t, j))
                s_ref[j, rows, :] = h
                o = o + h * key_row(r_ref, t, j)
                sa = sa + h * key_row(kk_ref, t_next, j)
            o_ref[0, t, rows, :] = both_halves(o)
            sa_next.append(sa)
        return tuple(sa_next)

    sa0 = []
    for rows in tiles:
        sa = jnp.zeros((SUBLANES, LANES), F32)
        for j in range(RW_J):
            sa = sa + s_ref[j, rows, :] * key_row(kk_ref, 0, j)
        sa0.append(sa)
    lax.fori_loop(0, tc, step, tuple(sa0))
    sout_ref[0] = s_ref[...]


def _rwkv_scan(r, d, k, kk, b, v, s0, *, tc):
    pg, t = r.shape[:2]
    vec = pl.BlockSpec((1, tc, RW_J, LANES), lambda g, c: (g, c, 0, 0))
    val = pl.BlockSpec((1, tc, RWKV_HEAD_DIM, LANES), lambda g, c: (g, c, 0, 0))
    st = pl.BlockSpec((1, RW_J, RWKV_HEAD_DIM, LANES), lambda g, c: (g, 0, 0, 0))
    return pl.pallas_call(
        functools.partial(_rwkv_scan_kernel, tc=tc),
        grid=(pg, t // tc),
        in_specs=[vec, vec, vec, vec, vec, val, st],
        out_specs=[val, st],
        out_shape=[jax.ShapeDtypeStruct((pg, t, RWKV_HEAD_DIM, LANES), F32),
                   jax.ShapeDtypeStruct((pg, RW_J, RWKV_HEAD_DIM, LANES), F32)],
        scratch_shapes=[pltpu.VMEM((RW_J, RWKV_HEAD_DIM, LANES), F32)],
        compiler_params=_cparams(("parallel", "arbitrary")),
        name="rwkv_scan",
    )(r, d, k, kk, b, v, s0)


def _rwkv_post_kernel(o_ref, bonus_ref, g_ref, lg_ref, lb_ref, out_ref):
    o = o_ref[...]
    inv = 1.0 / RWKV_HEAD_DIM
    mean = _seg_sum(o, RWKV_HEAD_DIM) * inv
    cen = o - mean
    var = _seg_sum(cen * cen, RWKV_HEAD_DIM) * inv
    y = cen * lax.rsqrt(var + LNX_EPS) * lg_ref[...] + lb_ref[...]
    out_ref[...] = (y + bonus_ref[...]) * g_ref[...]


def _rwkv_post(o, bonus, g, lnx_g, lnx_b, *, tm):
    m = o.shape[0]
    blk = pl.BlockSpec((tm, RWKV_W), lambda i: (i, 0))
    row = pl.BlockSpec((1, RWKV_W), lambda i: (0, 0))
    return pl.pallas_call(
        _rwkv_post_kernel,
        grid=(m // tm,),
        in_specs=[blk, blk, blk, row, row],
        out_specs=blk,
        out_shape=jax.ShapeDtypeStruct((m, RWKV_W), F32),
        compiler_params=_cparams(("parallel",)),
        name="rwkv_post",
    )(o, bonus, g, lnx_g, lnx_b)


def _pairs_layout(x, n, t):
    pg = n * RWKV_HEADS // RW_PAIRS
    y = x.reshape(n, t, RWKV_HEADS, 2, RW_J).transpose(1, 4, 3, 0, 2).reshape(t, RW_J, 2, pg, RW_PAIRS)
    return y.transpose(3, 0, 1, 2, 4).reshape(pg, t, RW_J, LANES)


def _pairs_layout_v(x, n, t):
    pg = n * RWKV_HEADS // RW_PAIRS
    y = x.reshape(n, t, RWKV_HEADS, RWKV_HEAD_DIM).transpose(1, 3, 0, 2).reshape(t, RWKV_HEAD_DIM, pg, RW_PAIRS)
    y = y.transpose(2, 0, 1, 3)
    return jnp.concatenate([y, y], axis=-1)


def _pairs_unlayout_v(y, n, t):
    pg = y.shape[0]
    z = y[..., :RW_PAIRS].transpose(1, 2, 0, 3).reshape(t, RWKV_HEAD_DIM, n, RWKV_HEADS)
    return z.transpose(2, 0, 3, 1).reshape(n * t, RWKV_W)


def _state_layout(s0):
    n = s0.shape[0]
    pg = n * RWKV_HEADS // RW_PAIRS
    y = s0.reshape(pg, RW_PAIRS, RWKV_HEAD_DIM, 2, RW_J)
    return y.transpose(0, 4, 2, 3, 1).reshape(pg, RW_J, RWKV_HEAD_DIM, LANES)


def _state_unlayout(y, n):
    pg = y.shape[0]
    z = y.reshape(pg, RW_J, RWKV_HEAD_DIM, 2, RW_PAIRS).transpose(0, 4, 2, 3, 1)
    return z.reshape(n, RWKV_HEADS, RWKV_HEAD_DIM, RWKV_HEAD_DIM)


def _outproj_kernel(x_ref, a_ref, b_ref, wa_ref, wb_ref, o_ref):
    y = jnp.dot(a_ref[...].astype(BF16), wa_ref[...], preferred_element_type=F32)
    y = y + jnp.dot(b_ref[...].astype(BF16), wb_ref[...], preferred_element_type=F32)
    o_ref[...] = x_ref[...] + y


def _outproj(x, a, b, wa, wb, *, tm):
    m = x.shape[0]
    return pl.pallas_call(
        _outproj_kernel,
        grid=(m // tm,),
        in_specs=[
            pl.BlockSpec((tm, D_MODEL), lambda i: (i, 0)),
            pl.BlockSpec((tm, a.shape[1]), lambda i: (i, 0)),
            pl.BlockSpec((tm, b.shape[1]), lambda i: (i, 0)),
            pl.BlockSpec(wa.shape, lambda i: (0, 0)),
            pl.BlockSpec(wb.shape, lambda i: (0, 0)),
        ],
        out_specs=pl.BlockSpec((tm, D_MODEL), lambda i: (i, 0)),
        out_shape=jax.ShapeDtypeStruct((m, D_MODEL), F32),
        compiler_params=_cparams(("parallel",)),
        name="outproj",
    )(x, a, b, wa, wb)


def _memkv_kernel(x_ref, g_ref, w_ref, kg_ref, o_ref):
    xn = _rms(x_ref[...], g_ref[...]).astype(BF16)
    kv = jnp.dot(xn, w_ref[...], preferred_element_type=F32)
    xw = X_HEADS * X_HEAD_DIM
    o_ref[:, 0:xw] = _seg_rms(kv[:, 0:xw], kg_ref[...], X_HEAD_DIM)
    o_ref[:, xw:2 * xw] = kv[:, xw:2 * xw]


def _memkv(mem, g, w, kg, *, tm):
    m = mem.shape[0]
    xw = X_HEADS * X_HEAD_DIM
    return pl.pallas_call(
        _memkv_kernel,
        grid=(m // tm,),
        in_specs=[
            pl.BlockSpec((tm, D_MODEL), lambda i: (i, 0)),
            pl.BlockSpec((1, D_MODEL), lambda i: (0, 0)),
            pl.BlockSpec((D_MODEL, 2 * xw), lambda i: (0, 0)),
            pl.BlockSpec((1, xw), lambda i: (0, 0)),
        ],
        out_specs=pl.BlockSpec((tm, 2 * xw), lambda i: (i, 0)),
        out_shape=jax.ShapeDtypeStruct((m, 2 * xw), F32),
        compiler_params=_cparams(("parallel",)),
        name="memkv",
    )(mem, g, w, kg)


def _xattn_kernel(x_ref, mkv_ref, g_ref, wq_ref, qg_ref, wo_ref, o_ref):
    x = x_ref[0]
    xn = _rms(x, g_ref[...]).astype(BF16)
    q = jnp.dot(xn, wq_ref[...], preferred_element_type=F32)
    q = _seg_rms(q, qg_ref[...], X_HEAD_DIM) * (X_HEAD_DIM ** -0.5)
    xw = X_HEADS * X_HEAD_DIM
    outs = []
    for h in range(X_HEADS):
        lo = h * X_HEAD_DIM
        k = mkv_ref[0, :, lo:lo + X_HEAD_DIM].astype(BF16)
        v = mkv_ref[0, :, xw + lo:xw + lo + X_HEAD_DIM].astype(BF16)
        s = lax.dot_general(q[:, lo:lo + X_HEAD_DIM].astype(BF16), k, (((1,), (1,)), ((), ())),
                            preferred_element_type=F32)
        e = jnp.exp(s - jnp.max(s, axis=-1, keepdims=True))
        p = e / jnp.sum(e, axis=-1, keepdims=True)
        outs.append(jnp.dot(p.astype(BF16), v, preferred_element_type=F32))
    o = jnp.concatenate(outs, axis=-1).astype(BF16)
    o_ref[0] = x + jnp.dot(o, wo_ref[...], preferred_element_type=F32)


def _xattn(x, mkv, g, wq, qg, wo, *, tm):
    n, t, _ = x.shape
    xw = X_HEADS * X_HEAD_DIM
    n_mem = mkv.shape[1]
    return pl.pallas_call(
        _xattn_kernel,
        grid=(n, t // tm),
        in_specs=[
            pl.BlockSpec((1, tm, D_MODEL), lambda b, i: (b, i, 0)),
            pl.BlockSpec((1, n_mem, 2 * xw), lambda b, i: (b, 0, 0)),
            pl.BlockSpec((1, D_MODEL), lambda b, i: (0, 0)),
            pl.BlockSpec((D_MODEL, xw), lambda b, i: (0, 0)),
            pl.BlockSpec((1, xw), lambda b, i: (0, 0)),
            pl.BlockSpec((xw, D_MODEL), lambda b, i: (0, 0)),
        ],
        out_specs=pl.BlockSpec((1, tm, D_MODEL), lambda b, i: (b, i, 0)),
        out_shape=jax.ShapeDtypeStruct((n, t, D_MODEL), F32),
        compiler_params=_cparams(("parallel", "parallel")),
        name="xattn",
    )(x, mkv, g, wq, qg, wo)


def _row(v):
    return v.reshape(1, -1).astype(F32)


def _block_diag2(w):
    z = jnp.zeros_like(w)
    return jnp.concatenate([jnp.concatenate([w, z], axis=-1), jnp.concatenate([z, w], axis=-1)], axis=-2)


def _prep_weights(norm_ffn1_g, w_ffn1_gu, w_ffn1_down, norm_mix_g, w_in, w_out, q_norm_g, kc_norm_g, ks_norm_g,
                  kw_norm_g, gate_b, cmp_pe_k, cmp_pe_v, w_cmp_k1, w_cmp_k2, w_cmp_v1, w_cmp_v2, rwkv_mu, rwkv_w0,
                  w_decay2, rwkv_a0, w_iclr2, w_gate2, rwkv_k_k, rwkv_k_a, rwkv_r_k, lnx_g, lnx_b, norm_x_g,
                  norm_mem_g, w_xq, w_xkv, xq_norm_g, xk_norm_g, w_xo, norm_ffn2_g, w_ffn2_gu, w_ffn2_down,
                  norm_out_g):
    G = NSA_KV_HEADS
    W = {}
    W["ffn1"] = (_row(norm_ffn1_g), w_ffn1_gu.astype(BF16), w_ffn1_down.astype(BF16))
    W["ffn2"] = (_row(norm_ffn2_g), w_ffn2_gu.astype(BF16), w_ffn2_down.astype(BF16))
    W["norm_out_g"] = _row(norm_out_g)
    w_pad = jnp.concatenate([w_in[:, :NSA_COLS], jnp.zeros((D_MODEL, NSA_PAD - NSA_COLS), F32), w_in[:, NSA_COLS:]],
                            axis=1).astype(BF16)
    head_g = jnp.zeros((NSA_PAD,), F32)
    head_g = head_g.at[0:NSA_Q_W].set(jnp.tile(q_norm_g, NSA_HEADS))
    c = NSA_Q_W + 2 * NSA_KV_W
    head_g = head_g.at[c:c + NSA_KV_W].set(jnp.tile(ks_norm_g, G))
    c = NSA_Q_W + 4 * NSA_KV_W
    head_g = head_g.at[c:c + NSA_KV_W].set(jnp.tile(kw_norm_g, G))
    gb = jnp.zeros((LANES,), F32).at[0:3 * NSA_HEADS].set(gate_b)
    src = jnp.arange(NSA_KV_W)
    place = jnp.stack([jax.nn.one_hot((b * G + src // HEAD_DIM) * LANES + src % HEAD_DIM, KF_COLS, dtype=BF16)
                       for b in range(2)])
    W["proj"] = (_row(norm_mix_g), w_pad, _row(head_g), _row(gb), place)
    pe = jnp.concatenate([cmp_pe_k, cmp_pe_k, cmp_pe_v, cmp_pe_v], axis=-1)[:, None, :]
    W["cmp"] = (pe, _block_diag2(w_cmp_k1).astype(BF16), _block_diag2(w_cmp_v1).astype(BF16),
                _block_diag2(w_cmp_k2).astype(BF16), _block_diag2(w_cmp_v2).astype(BF16),
                _row(jnp.tile(kc_norm_g, G)))
    pe_t = jnp.concatenate([jnp.tile(cmp_pe_k.T, (1, 2)), jnp.tile(cmp_pe_v.T, (1, 2))], axis=0)[:, None, :]
    w1_t = jnp.stack([_block_diag2(w_cmp_k1.transpose(1, 0, 2)), _block_diag2(w_cmp_v1.transpose(1, 0, 2))])
    w2_t = jnp.stack([_block_diag2(w_cmp_k2), _block_diag2(w_cmp_v2)])
    W["cmp_paged"] = (pe_t, w1_t.astype(BF16), w2_t.astype(BF16), _row(jnp.tile(kc_norm_g, G)))
    zl = jnp.zeros((DECAY_LORA, RWKV_W), F32)
    W["rwkv_pre"] = (_row(rwkv_mu), _row(rwkv_w0), jnp.concatenate([w_decay2, zl], axis=0).astype(BF16),
                     _row(rwkv_a0), jnp.concatenate([zl, w_iclr2], axis=0).astype(BF16), w_gate2.astype(BF16),
                     _row(rwkv_k_k), _row(rwkv_k_a), _row(rwkv_r_k))
    W["rwkv_post"] = (_row(lnx_g), _row(lnx_b))
    W["w_out"] = (w_out[:NSA_Q_W].astype(BF16), w_out[NSA_Q_W:].astype(BF16))
    W["xattn"] = (_row(norm_x_g), w_xq.astype(BF16), _row(jnp.tile(xq_norm_g, X_HEADS)), w_xo.astype(BF16))
    W["memkv"] = (_row(norm_mem_g), w_xkv.astype(BF16), _row(jnp.tile(xk_norm_g, X_HEADS)))
    return W


def _tile_rows(m, pref):
    return pref if m % pref == 0 else m


def _rwkv_group(pr, shift0, s0, W, n, t):
    m = n * t
    r, d, k, v, kk, b, g, bonus = _rwkv_pre(pr, shift0, W["rwkv_pre"], n=n, t=t, tm=_tile_rows(t, 256))
    lay = lambda a: _pairs_layout(a, n, t)
    o, s_new = _rwkv_scan(lay(r), lay(d), lay(k), lay(kk), lay(b), _pairs_layout_v(v, n, t), _state_layout(s0),
                          tc=min(t, 32))
    o = _pairs_unlayout_v(o, n, t)
    o = _rwkv_post(o, bonus, g, *W["rwkv_post"], tm=_tile_rows(m, 512))
    return o, _state_unlayout(s_new, n)


def _nsa_prompt_group(pn, kf, W, n, t):
    G = NSA_KV_HEADS
    n_blk = t // CMP_BLOCK
    kc, vc = _compress_rows(pn, NSA_Q_W // LANES, W["cmp"], nblk=n_blk)
    end = (jnp.arange(n_blk, dtype=jnp.int32) + 1) * CMP_BLOCK - 1
    kcp = _key_features(kc.reshape(n, n_blk, NSA_KV_W), end)
    vct = vc.reshape(n, n_blk, G, HEAD_DIM).transpose(0, 2, 3, 1).astype(BF16)
    return _nsa_prompt(pn, kf, kcp, vct, n=n, t=t)


def _pad_axis(a, axis, size):
    pad = [(0, 0)] * a.ndim
    pad[axis] = (0, size - a.shape[axis])
    return jnp.pad(a, pad)


def _nsa_sample_group(pn, W, n, t_new, cache_kv, page_table, cache_win):
    G = NSA_KV_HEADS
    assert t_new <= CMP_BLOCK and t_new <= TQ
    past = page_table.shape[1] * PAGE_SIZE
    nb_past = past // CMP_BLOCK
    cache_t = cache_kv.transpose(0, 2, 3, 4, 1).reshape(cache_kv.shape[0], 4 * G * HEAD_DIM, PAGE_SIZE)
    win_t = cache_win.transpose(0, 2, 3, 4, 1)
    kc_past, vc_past = _compress_paged(page_table, cache_t, W["cmp_paged"])
    per_block = lambda a: a.reshape(n, G, nb_past, HEAD_DIM).transpose(0, 2, 1, 3).reshape(n, nb_past, NSA_KV_W)
    kc_past, vc_past = per_block(kc_past), per_block(vc_past)
    pn3 = pn.reshape(n, t_new, NSA_PAD)
    col = lambda i: pn3[:, :, NSA_Q_W + i * NSA_KV_W:NSA_Q_W + (i + 1) * NSA_KV_W]
    new_rows = _pad_axis(pn3[:, :, NSA_Q_W:NSA_Q_W + 2 * NSA_KV_W], 1, CMP_BLOCK).reshape(n * CMP_BLOCK, 2 * NSA_KV_W)
    kc_new, vc_new = _compress_rows(new_rows, 0, W["cmp"], nblk=n)
    nb_pad = -(-(nb_past + 1) // 16) * 16
    kc = _pad_axis(jnp.concatenate([kc_past, kc_new[:, None]], axis=1), 1, nb_pad)
    vc = _pad_axis(jnp.concatenate([vc_past, vc_new[:, None]], axis=1), 1, nb_pad)
    end_rel = (jnp.arange(nb_pad, dtype=jnp.int32) + 1) * CMP_BLOCK - 1 - past
    kcp = _key_features(kc, end_rel)
    vct = vc.reshape(n, nb_pad, G, HEAD_DIM).transpose(0, 2, 3, 1).astype(BF16)
    heads = lambda a: a.reshape(n, t_new, G, HEAD_DIM).transpose(0, 2, 1, 3)
    new_k = lambda i: _pad_axis(heads(col(i)), 2, KC).astype(BF16)
    new_vt = lambda i: _pad_axis(heads(col(i)).transpose(0, 1, 3, 2), 3, KC).astype(BF16)
    n_real = NSA_REP * t_new
    qt = _query_features(pn3[:, :, :NSA_Q_W]).reshape(n, t_new, G, NSA_REP, LANES)
    qt = _pad_axis(qt.transpose(0, 2, 4, 3, 1).reshape(n, G, LANES, n_real), 3, LANES)
    gates = pn3[:, :, NSA_Q_W + 6 * NSA_KV_W:NSA_COLS].reshape(n, t_new, G, NSA_REP, 3)
    gates_t = _pad_axis(gates.transpose(0, 2, 4, 3, 1).reshape(n, G, 3, n_real), 3, LANES)
    o_c, imp = _nsa_sample_cmp(qt, kcp, vct, past=past, n_new=t_new)
    imp_all = imp[:, :, :, :t_new].transpose(1, 2, 0, 3).reshape(G, nb_pad, n * t_new)
    selb = _nsa_rank(imp_all, past=past, n_new=t_new).reshape(G, nb_pad, n, t_new).transpose(2, 0, 1, 3)
    selb = jnp.pad(jnp.tile(selb, (1, 1, 1, NSA_REP)), ((0, 0),) * 3 + ((0, LANES - n_real),),
                   constant_values=-jnp.inf)
    o_t = _nsa_sample(page_table, qt, o_c, selb, win_t, new_k(2), new_vt(3), new_k(4), new_vt(5), gates_t, cache_t,
                      past=past, n_new=t_new)
    o = o_t[:, :, :, :n_real].reshape(n, G, HEAD_DIM, NSA_REP, t_new)
    return o.transpose(0, 4, 1, 3, 2).reshape(n * t_new, NSA_Q_W)


def _layer(x, mkv, shift0, s0, W, nsa_fn):
    n, t, _ = x.shape
    m = n * t
    G = NSA_KV_HEADS
    x2 = x.reshape(m, D_MODEL)
    tm = _tile_rows(m, 512)
    tm_ffn = _tile_rows(m, 1024)
    x2 = _ffn(x2, *W["ffn1"], W["norm_out_g"], final_norm=False, tm=tm_ffn, tf=D_FF // 2)
    pn, pr, kf = _proj(x2, *W["proj"], t=t, tm=_tile_rows(m, 512))
    o_nsa = nsa_fn(pn, kf)
    o_rwkv, s_new = _rwkv_group(pr, shift0, s0, W, n, t)
    x2 = _outproj(x2, o_nsa, o_rwkv, *W["w_out"], tm=tm)
    x3 = _xattn(x2.reshape(n, t, D_MODEL), mkv, *W["xattn"], tm=_tile_rows(t, 512))
    y = _ffn(x3.reshape(m, D_MODEL), *W["ffn2"], W["norm_out_g"], final_norm=True, tm=tm_ffn, tf=D_FF // 2)
    pn3 = pn.reshape(n, t, NSA_PAD)
    kv_rows = pn3[:, :, NSA_Q_W:NSA_Q_W + 4 * NSA_KV_W].reshape(n, t, 4, G, HEAD_DIM)
    win_new = pn3[:, :, NSA_Q_W + 4 * NSA_KV_W:NSA_Q_W + 6 * NSA_KV_W].reshape(n, t, 2, G, HEAD_DIM)
    shift_new = pr.reshape(n, t, RWKV_COLS)[:, -1]
    return y.reshape(n, t, D_MODEL), kv_rows, win_new, s_new, shift_new


def kernel(x_prompt, x_sample, cache_nsa_kv, cache_nsa_win, state_rwkv_s, state_rwkv_shift, cache_mem_kv, page_table, mem_prompt, norm_ffn1_g, w_ffn1_gu, w_ffn1_down, norm_mix_g, w_in, w_out, q_norm_g, kc_norm_g, ks_norm_g, kw_norm_g, gate_b, cmp_pe_k, cmp_pe_v, w_cmp_k1, w_cmp_k2, w_cmp_v1, w_cmp_v2, rwkv_mu, rwkv_w0, w_decay2, rwkv_a0, w_iclr2, w_gate2, rwkv_k_k, rwkv_k_a, rwkv_r_k, lnx_g, lnx_b, norm_x_g, norm_mem_g, w_xq, w_xkv, xq_norm_g, xk_norm_g, w_xo, norm_ffn2_g, w_ffn2_gu, w_ffn2_down, norm_out_g):
    layer_weights = (norm_ffn1_g, w_ffn1_gu, w_ffn1_down, norm_mix_g, w_in, w_out, q_norm_g, kc_norm_g, ks_norm_g,
                     kw_norm_g, gate_b, cmp_pe_k, cmp_pe_v, w_cmp_k1, w_cmp_k2, w_cmp_v1, w_cmp_v2, rwkv_mu, rwkv_w0,
                     w_decay2, rwkv_a0, w_iclr2, w_gate2, rwkv_k_k, rwkv_k_a, rwkv_r_k, lnx_g, lnx_b, norm_x_g,
                     norm_mem_g, w_xq, w_xkv, xq_norm_g, xk_norm_g, w_xo, norm_ffn2_g, w_ffn2_gu, w_ffn2_down,
                     norm_out_g)
    assert w_in.shape[0] == 1, "single-layer trunk"
    W = _prep_weights(*(w[0] for w in layer_weights))
    n_p, t_p, _ = x_prompt.shape
    n_s, t_s, _ = x_sample.shape
    n_mem = mem_prompt.shape[1]
    xw = X_HEADS * X_HEAD_DIM

    mkv_p = _memkv(mem_prompt.reshape(n_p * n_mem, D_MODEL), *W["memkv"], tm=_tile_rows(n_p * n_mem, 512))
    mkv_p = mkv_p.reshape(n_p, n_mem, 2 * xw)
    y_p, kv_p, win_p, rs_p, sh_p = _layer(
        x_prompt, mkv_p, jnp.zeros((n_p, RWKV_COLS), F32),
        jnp.zeros((n_p, RWKV_HEADS, RWKV_HEAD_DIM, RWKV_HEAD_DIM), F32), W,
        lambda pn, kf: _nsa_prompt_group(pn, kf, W, n_p, t_p))
    win_p = win_p[:, t_p - min(WINDOW, t_p):]

    mkv_s = cache_mem_kv[0].reshape(n_s, n_mem, 2 * xw)
    y_s, kv_s, win_new, rs_s, sh_s = _layer(
        x_sample, mkv_s, state_rwkv_shift[0], state_rwkv_s[0], W,
        lambda pn, kf: _nsa_sample_group(pn, W, n_s, t_s, cache_nsa_kv[0], page_table, cache_nsa_win[0]))
    win_s = jnp.concatenate([cache_nsa_win[0], win_new], axis=1)[:, t_s:]

    mkv_out = mkv_p.reshape(1, n_p, n_mem, 2, X_HEADS, X_HEAD_DIM)
    return (y_p, y_s, kv_p[None], kv_s[None], win_p[None], win_s[None], rs_p[None], rs_s[None], sh_p[None],
            sh_s[None], mkv_out)
```
